```python
import math
import jax, jax.numpy as jnp
from jax import lax
import numpy as np

D_MODEL = 2048
BATCH = 4
SEQ = 4096
DEPTH = 2

GRID_W = 64
CTX_LEN = 256
N_GROUPS = 4
D_GROUP = D_MODEL // N_GROUPS
D_IN = 13 * D_GROUP
N_MOD = 6
RMS_EPS = 1e-6

HY_ORDER = 2
HY_EMB = 33
HY_FFN = 64
HY_TARGET = 1e-2
HY_STRONG_DECAY_PCT = 0.3
HY_WEAK_DECAY_PCT = 1.5
HY_MIN_DECAY = math.log(HY_TARGET) / HY_WEAK_DECAY_PCT
HY_MAX_DECAY = math.log(HY_TARGET) / HY_STRONG_DECAY_PCT

RG_HEADS = 8
RG_CONV = 4
RG_C = 8.0

RW_HEAD = 64
RW_HEADS = D_GROUP // RW_HEAD
RW_DECAY_LORA = 64
RW_A_LORA = 64
RW_V_LORA = 32
RW_G_LORA = 128
RW_GN_EPS = 64e-5

RT_HEADS = 4
RT_DK = D_GROUP // RT_HEADS
RT_DV = D_GROUP // RT_HEADS
RT_CHUNK = 128
RT_GN_EPS = 1e-6
ROPE_BASE = 10000.0

N_EXPERTS = 32
TOP_K = 4
D_EXPERT = D_MODEL // 2
SWIGLU_LIMIT = 7.0
SWIGLU_ALPHA = 1.702

kernel_name = "hybrid_prefix_dit_trunk"


def rms_norm(x, w):
    xf = x.astype(jnp.float32)
    y = xf * lax.rsqrt(jnp.mean(xf * xf, axis=-1, keepdims=True) + RMS_EPS)
    return (y * w.astype(jnp.float32)).astype(x.dtype)


def modulate(x, w, shift, scale):
    return rms_norm(x, w) * (1 + scale) + shift


def shift_prev(x, n):
    if n == 0:
        return x
    return jnp.pad(x, ((0, 0), (n, 0), (0, 0)))[:, : x.shape[1]]


def shift_next(x, n):
    return jnp.pad(x, ((0, 0), (0, n), (0, 0)))[:, n:]


def linrec_combine(lhs, rhs):
    a1, b1 = lhs
    a2, b2 = rhs
    return a1 * a2, a2 * b1 + b2


def rope_2d(x, rows, cols):
    half = x.shape[-1] // 2
    quarter = half // 2
    inv = jnp.power(ROPE_BASE, -jnp.arange(quarter, dtype=jnp.float32) / quarter)

    def rot(xp, pos):
        ang = pos[:, None] * inv[None, :]
        cos = jnp.cos(ang)[None, :, None, :]
        sin = jnp.sin(ang)[None, :, None, :]
        x1, x2 = xp[..., :quarter], xp[..., quarter:]
        return jnp.concatenate([x1 * cos - x2 * sin, x1 * sin + x2 * cos], axis=-1)

    return jnp.concatenate([rot(x[..., :half], rows), rot(x[..., half:], cols)], axis=-1)


def bidirectional(fn, ctx_in, lat_in, state0, dir_params):
    out_c, out_l = 0.0, 0.0
    for d in range(2):
        p = tuple(a[d] for a in dir_params)
        flip = (lambda t: jnp.flip(t, axis=1)) if d == 1 else (lambda t: t)
        oc, s_ctx = fn(tuple(flip(t) for t in ctx_in), state0, p, d)
        ol, _ = fn(tuple(flip(t) for t in lat_in), s_ctx, p, d)
        out_c = out_c + flip(oc)
        out_l = out_l + flip(ol)
    return out_c, out_l


def hyena_filter_spectrum(L, w1, b1, w2, b2, w3, b3, w4, freq):
    f32 = jnp.float32
    t = jnp.linspace(0.0, 1.0, L, dtype=f32)[:, None]
    bands = (HY_EMB - 1) // 2
    fr = jnp.linspace(1e-4, bands - 1, bands, dtype=f32)
    ang = (2.0 * math.pi / L) * jnp.arange(L, dtype=f32)[:, None] * fr[None, :]
    z = jnp.concatenate([t, jnp.cos(ang), -jnp.sin(ang)], axis=-1)
    h = jnp.sin(freq * (z @ w1 + b1))
    h = jnp.sin(freq * (h @ w2 + b2))
    h = jnp.sin(freq * (h @ w3 + b3))
    h = (h @ w4).reshape(L, 2, HY_ORDER, D_GROUP)
    deltas = jnp.abs(jnp.linspace(HY_MIN_DECAY, HY_MAX_DECAY, D_GROUP, dtype=f32))
    h = h * jnp.exp(-t * deltas)[:, None, None, :]
    taps = jnp.concatenate(
        [h[:, 0], jnp.zeros((1, HY_ORDER, D_GROUP), f32), h[:0:-1, 1]], axis=0)
    taps = taps / jnp.sum(jnp.abs(taps), axis=0, keepdims=True)
    return jnp.fft.rfft(taps, axis=0)


def hyena_mixer(u, short_w, short_b, fw1, fb1, fw2, fb2, fw3, fb3, fw4, ffreq, bias):
    L = u.shape[1]
    u = short_w[0] * shift_prev(u, 1) + short_w[1] * u + short_w[2] * shift_next(u, 1) + short_b
    v, x1, x2 = jnp.split(u, 3, axis=-1)
    spec = hyena_filter_spectrum(L, fw1, fb1, fw2, fb2, fw3, fb3, fw4, ffreq)
    z = v
    for o, gate in enumerate((x1, x2)):
        zf = jnp.fft.rfft(z, n=2 * L, axis=1)
        z = jnp.fft.irfft(zf * spec[None, :, o], n=2 * L, axis=1)[:, :L] + bias[o] * z
        z = gate * z
    return z


def rglru_direction(inp, h0, p, d):
    (xb,) = inp
    conv_w, conv_b, wa, ba, wx, bx, lam = p
    xc = conv_b + sum(conv_w[j] * shift_prev(xb, RG_CONV - 1 - j) for j in range(RG_CONV))
    B_, L = xb.shape[0], xb.shape[1]
    xh = xc.reshape(B_, L, RG_HEADS, -1)
    r = jax.nn.sigmoid(jnp.einsum('blhi,hij->blhj', xh, wa).reshape(B_, L, -1) + ba)
    i = jax.nn.sigmoid(jnp.einsum('blhi,hij->blhj', xh, wx).reshape(B_, L, -1) + bx)
    log_a = -RG_C * r * jax.nn.softplus(-lam)
    a = jnp.exp(log_a)
    b = jnp.sqrt(-jnp.expm1(2.0 * log_a)) * (i * xc)
    a_cum, h = lax.associative_scan(linrec_combine, (a, b), axis=1)
    h = h + a_cum * h0[:, None, :]
    return h, h[:, -1]


def rglru_mixer(s_ctx, s_lat, conv_w, conv_b, wa, ba, wx, bx, lam):
    xc_, gc_ = jnp.split(s_ctx, 2, axis=-1)
    xl_, gl_ = jnp.split(s_lat, 2, axis=-1)
    h0 = jnp.zeros((s_ctx.shape[0], D_GROUP), jnp.float32)
    oc, ol = bidirectional(rglru_direction, (xc_,), (xl_,), h0,
                           (conv_w, conv_b, wa, ba, wx, bx, lam))
    return oc * jax.nn.gelu(gc_), ol * jax.nn.gelu(gl_)


def rwkv_mixer(s_ctx, s_lat, vf_ctx, vf_lat, vmix, mu, w0, w1, w2, a0, a1, a2,
               g1, g2, k_k, k_a, r_k, ln_w, ln_b):
    def prep(s, vfirst):
        r, k, v, z = jnp.split(s, 4, axis=-1)
        if vmix is None:
            vfirst = v
        else:
            v0, v1, v2 = vmix
            v = v + (vfirst - v) * jax.nn.sigmoid(v0 + (z @ v1) @ v2)
        g = jax.nn.sigmoid(z @ g1) @ g2
        return (r, k, v, z), g, vfirst

    in_c, g_c, vf_c = prep(s_ctx, vf_ctx)
    in_l, g_l, vf_l = prep(s_lat, vf_lat)

    def direction(inp, S0, p, d):
        mu_d, w0_d, w1_d, w2_d, a0_d, a1_d, a2_d = p
        r, k, v, z = [s + (shift_prev(s, 1) - s) * mu_d[j] for j, s in enumerate(inp)]
        w_log = -jax.nn.softplus(-(w0_d + jnp.tanh(z @ w1_d) @ w2_d)) - 0.5
        decay = jnp.exp(-jnp.exp(w_log))
        a = jax.nn.sigmoid(a0_d + (z @ a1_d) @ a2_d)
        B_, L = r.shape[0], r.shape[1]
        heads = lambda t: t.reshape(B_, L, RW_HEADS, RW_HEAD)
        kk = heads(k * k_k)
        kk = kk / jnp.maximum(jnp.sqrt(jnp.sum(kk * kk, axis=-1, keepdims=True)), 1e-12)
        k = k * (1 + (a - 1) * k_a)
        rh, wh, kh, vh, ah = (heads(t) for t in (r, decay, k, v, a))
        xs = tuple(jnp.moveaxis(t, 1, 0) for t in (rh, wh, kh, vh, kk, kk * ah))

        def step(S, xt):
            r_t, w_t, k_t, v_t, kk_t, kka_t = xt
            S = (S * w_t[..., None, :]
                 - jnp.einsum('bhij,bhj->bhi', S, kk_t)[..., None] * kka_t[..., None, :]
                 + v_t[..., None] * k_t[..., None, :])
            return S, jnp.einsum('bhij,bhj->bhi', S, r_t)

        S_fin, y = lax.scan(step, S0, xs)
        y = jnp.moveaxis(y, 0, 1)
        m = jnp.mean(y, axis=-1, keepdims=True)
        var = jnp.mean(jnp.square(y - m), axis=-1, keepdims=True)
        y = ((y - m) * lax.rsqrt(var + RW_GN_EPS)).reshape(B_, L, -1) * ln_w + ln_b
        bonus = (jnp.sum(rh * kh * r_k, axis=-1, keepdims=True) * vh).reshape(B_, L, -1)
        return y + bonus, S_fin

    S0 = jnp.zeros((s_ctx.shape[0], RW_HEADS, RW_HEAD, RW_HEAD), jnp.float32)
    oc, ol = bidirectional(direction, in_c, in_l, S0, (mu, w0, w1, w2, a0, a1, a2))
    return oc * g_c, ol * g_l, vf_c, vf_l


def retention_direction(inp, S0, p, d):
    q, k, v = inp
    (decay_raw,) = p
    B_, L, H, dk = q.shape
    dv = v.shape[-1]
    n = L // RT_CHUNK
    qc = q.reshape(B_, n, RT_CHUNK, H, dk)
    kc = k.reshape(B_, n, RT_CHUNK, H, dk)
    vc = v.reshape(B_, n, RT_CHUNK, H, dv)
    lg = -jax.nn.softplus(decay_raw.astype(jnp.float32))
    idx = jnp.arange(RT_CHUNK, dtype=jnp.float32)
    diff = idx[:, None] - idx[None, :]
    keep = diff > 0 if d == 1 else diff >= 0
    dmask = jnp.where(keep, jnp.exp(jnp.where(keep, diff, 0.0)[None] * lg[:, None, None]), 0.0)
    inner = jnp.einsum('bnihd,bnjhd->bnhij', qc, kc) * dmask[None, None]
    inner = jnp.einsum('bnhij,bnjhe->bnihe', inner, vc)
    k_dec = jnp.exp((RT_CHUNK - 1 - idx)[:, None] * lg[None, :])
    kv = jnp.einsum('bnjhd,bnjhe->nbhde', kc * k_dec[:, :, None], vc)
    chunk_dec = jnp.exp(RT_CHUNK * lg)[None, :, None, None]

    def step(S, kv_n):
        return chunk_dec * S + kv_n, S

    S_fin, S_prev = lax.scan(step, S0, kv)
    q_dec = jnp.exp((idx + 1.0)[:, None] * lg[None, :])
    cross = jnp.einsum('bnihd,nbhde->bnihe', qc * q_dec[:, :, None], S_prev)
    return (inner + cross).reshape(B_, L, H, dv), S_fin


def retention_mixer(s_ctx, s_lat, rows, cols, decay_raw, gn_w):
    def prep(s, rotate):
        q, k, v, g = jnp.split(s, 4, axis=-1)
        B_, L = s.shape[0], s.shape[1]
        q = q.reshape(B_, L, RT_HEADS, RT_DK)
        k = k.reshape(B_, L, RT_HEADS, RT_DK)
        v = v.reshape(B_, L, RT_HEADS, RT_DV)
        if rotate:
            q, k = rope_2d(q, rows, cols), rope_2d(k, rows, cols)
        return (q, k * RT_DK ** -0.5, v), g

    in_c, g_c = prep(s_ctx, False)
    in_l, g_l = prep(s_lat, True)
    S0 = jnp.zeros((s_ctx.shape[0], RT_HEADS, RT_DK, RT_DV), jnp.float32)
    oc, ol = bidirectional(retention_direction, in_c, in_l, S0, (decay_raw,))

    def finish(o, g):
        m = jnp.mean(o, axis=-1, keepdims=True)
        var = jnp.mean(jnp.square(o - m), axis=-1, keepdims=True)
        o = ((o - m) * lax.rsqrt(var + RT_GN_EPS)).reshape(o.shape[0], o.shape[1], -1)
        return o * gn_w * jax.nn.silu(g)

    return finish(oc, g_c), finish(ol, g_l)


def moe(h, router_w, router_b, w_gu, b_gu, w_dn, b_dn):
    f32 = jnp.float32
    t = h.reshape(-1, h.shape[-1])
    logits = t.astype(f32) @ router_w.astype(f32) + router_b.astype(f32)
    vals, idx = lax.top_k(logits, TOP_K)
    probs = jax.nn.softmax(vals, axis=-1)
    gate = jnp.sum(jax.nn.one_hot(idx, N_EXPERTS, dtype=f32) * probs[..., None], axis=1)
    out = jnp.zeros(t.shape, f32)
    for e in range(N_EXPERTS):
        gu = (t @ w_gu[e]).astype(f32) + b_gu[e]
        glu = jnp.minimum(gu[:, :D_EXPERT], SWIGLU_LIMIT)
        lin = jnp.clip(gu[:, D_EXPERT:], -SWIGLU_LIMIT, SWIGLU_LIMIT)
        act = glu * jax.nn.sigmoid(SWIGLU_ALPHA * glu) * (lin + 1.0)
        y = (act.astype(t.dtype) @ w_dn[e]).astype(f32) + b_dn[e]
        out = out + gate[:, e:e + 1] * y
    return out.reshape(h.shape).astype(h.dtype)


def setup_inputs(seed: int = 0) -> dict:
    key = jax.random.key(seed)
    keys = iter(jax.random.split(key, 64))
    f32 = jnp.float32
    G, D, E, F, NL = D_GROUP, D_MODEL, N_EXPERTS, D_EXPERT, DEPTH

    def normal(shape, scale):
        return jax.random.normal(next(keys), shape, f32) * scale

    def uniform(shape, lo, hi):
        return jax.random.uniform(next(keys), shape, f32, lo, hi)

    def near(shape, centre, noise=0.02):
        return centre + normal(shape, noise)

    lam_s = uniform((NL, 2, G), 0.9, 0.999) ** (1.0 / RG_C)
    rg_lambda = jnp.log(lam_s) - jnp.log1p(-lam_s)
    neg_log_gamma = -np.log1p(-np.power(2.0, -5.0 - np.arange(RT_HEADS)))
    rt_base = jnp.asarray(np.log(np.expm1(neg_log_gamma)), f32)

    return {
        'x': normal((BATCH, SEQ, D), 1.0),
        'c': normal((BATCH, D), 1.0),
        'ctx': normal((BATCH, CTX_LEN, D), 1.0),
        'c_ctx': normal((D,), 1.0),
        'ada_w': normal((NL, D, N_MOD * D), 0.5 * D ** -0.5),
        'ada_b': normal((NL, N_MOD * D), 0.02),
        'norm1_w': near((NL, D), 1.0),
        'norm2_w': near((NL, D), 1.0),
        'w_in': normal((NL, D, D_IN), D ** -0.5),
        'w_out': normal((NL, N_GROUPS * G, D), (N_GROUPS * G) ** -0.5),
        'hy_short_w': normal((NL, 3, 3 * G), 3 ** -0.5),
        'hy_short_b': normal((NL, 3 * G), 0.02),
        'hy_f_w1': normal((NL, HY_EMB, HY_FFN), HY_EMB ** -0.5),
        'hy_f_b1': normal((NL, HY_FFN), 0.1),
        'hy_f_w2': normal((NL, HY_FFN, HY_FFN), HY_FFN ** -0.5),
        'hy_f_b2': normal((NL, HY_FFN), 0.1),
        'hy_f_w3': normal((NL, HY_FFN, HY_FFN), HY_FFN ** -0.5),
        'hy_f_b3': normal((NL, HY_FFN), 0.1),
        'hy_f_w4': normal((NL, HY_FFN, 2 * HY_ORDER * G), HY_FFN ** -0.5),
        'hy_f_freq': near((NL, HY_FFN), 1.0, 0.05),
        'hy_bias': normal((NL, HY_ORDER, G), 1.0),
        'rg_conv_w': normal((NL, 2, RG_CONV, G), RG_CONV ** -0.5),
        'rg_conv_b': normal((NL, 2, G), 0.02),
        'rg_wa': normal((NL, 2, RG_HEADS, G // RG_HEADS, G // RG_HEADS), (G // RG_HEADS) ** -0.5),
        'rg_ba': normal((NL, 2, G), 0.02),
        'rg_wx': normal((NL, 2, RG_HEADS, G // RG_HEADS, G // RG_HEADS), (G // RG_HEADS) ** -0.5),
        'rg_bx': normal((NL, 2, G), 0.02),
        'rg_lambda': rg_lambda,
        'rw_mu': uniform((NL, 2, 4, G), 0.0, 1.0),
        'rw_w0': uniform((NL, 2, G), -6.5, -1.5),
        'rw_w1': normal((NL, 2, G, RW_DECAY_LORA), G ** -0.5),
        'rw_w2': normal((NL, 2, RW_DECAY_LORA, G), 0.1 * RW_DECAY_LORA ** -0.5),
        'rw_a0': normal((NL, 2, G), 0.1),
        'rw_a1': normal((NL, 2, G, RW_A_LORA), G ** -0.5),
        'rw_a2': normal((NL, 2, RW_A_LORA, G), 0.1 * RW_A_LORA ** -0.5),
        'rw_g1': normal((NL, G, RW_G_LORA), G ** -0.5),
        'rw_g2': normal((NL, RW_G_LORA, G), RW_G_LORA ** -0.5),
        'rw_k_k': near((NL, G), 0.85, 0.05),
        'rw_k_a': near((NL, G), 1.0, 0.05),
        'rw_r_k': normal((NL, RW_HEADS, RW_HEAD), 0.1),
        'rw_ln_w': near((NL, G), 1.0),
        'rw_ln_b': normal((NL, G), 0.02),
        'rw_v0': near((NL - 1, G), 1.0, 0.05),
        'rw_v1': normal((NL - 1, G, RW_V_LORA), G ** -0.5),
        'rw_v2': normal((NL - 1, RW_V_LORA, G), 0.1 * RW_V_LORA ** -0.5),
        'rt_decay': rt_base + normal((NL, 2, RT_HEADS), 0.05),
        'rt_gn_w': near((NL, G), 1.0),
        'moe_router_w': normal((NL, D, E), D ** -0.5),
        'moe_router_b': normal((NL, E), 0.01),
        'moe_w_gu': normal((NL, E, D, 2 * F), D ** -0.5),
        'moe_b_gu': normal((NL, E, 2 * F), 0.02),
        'moe_w_dn': normal((NL, E, F, D), F ** -0.5),
        'moe_b_dn': normal((NL, E, D), 0.02),
        'final_norm_w': near((D,), 1.0),
    }


def reference(x, c, ctx, c_ctx, ada_w, ada_b, norm1_w, norm2_w, w_in, w_out,
              hy_short_w, hy_short_b, hy_f_w1, hy_f_b1, hy_f_w2, hy_f_b2, hy_f_w3, hy_f_b3,
              hy_f_w4, hy_f_freq, hy_bias,
              rg_conv_w, rg_conv_b, rg_wa, rg_ba, rg_wx, rg_bx, rg_lambda,
              rw_mu, rw_w0, rw_w1, rw_w2, rw_a0, rw_a1, rw_a2, rw_g1, rw_g2, rw_k_k, rw_k_a,
              rw_r_k, rw_ln_w, rw_ln_b, rw_v0, rw_v1, rw_v2,
              rt_decay, rt_gn_w,
              moe_router_w, moe_router_b, moe_w_gu, moe_b_gu, moe_w_dn, moe_b_dn,
              final_norm_w):
    f32 = jnp.float32
    dt = x.dtype
    G = D_GROUP
    L = x.shape[1]
    n_ctx = ctx.shape[1]
    ROWS = L // GRID_W
    rows = jnp.repeat(jnp.arange(ROWS, dtype=f32), GRID_W)
    cols = jnp.tile(jnp.arange(GRID_W, dtype=f32), ROWS)
    cond_lat = jax.nn.silu(c.astype(f32))[:, None, :]
    cond_ctx = jax.nn.silu(c_ctx.astype(f32))[None, None, :]
    xl, xc = x, ctx.astype(dt)
    vf_c = vf_l = None

    for l in range(DEPTH):
        last = l == DEPTH - 1
        aw = ada_w[l].astype(f32)
        mod_l = jnp.split((cond_lat @ aw + ada_b[l]).astype(dt), N_MOD, axis=-1)
        mod_c = jnp.split((cond_ctx @ aw + ada_b[l]).astype(dt), N_MOD, axis=-1)

        hl = modulate(xl, norm1_w[l], mod_l[0], mod_l[1])
        hc = modulate(xc, norm1_w[l], mod_c[0], mod_c[1])
        ul = (hl @ w_in[l]).astype(f32)
        uc = (hc @ w_in[l]).astype(f32)
        hy_p = (hy_short_w[l], hy_short_b[l], hy_f_w1[l], hy_f_b1[l], hy_f_w2[l], hy_f_b2[l],
                hy_f_w3[l], hy_f_b3[l], hy_f_w4[l], hy_f_freq[l], hy_bias[l])
        hy_l = hyena_mixer(ul[..., : 3 * G], *hy_p)
        rg_c, rg_l = rglru_mixer(uc[..., 3 * G: 5 * G], ul[..., 3 * G: 5 * G],
                                 rg_conv_w[l], rg_conv_b[l], rg_wa[l], rg_ba[l],
                                 rg_wx[l], rg_bx[l], rg_lambda[l])
        vmix = None if l == 0 else (rw_v0[l - 1], rw_v1[l - 1], rw_v2[l - 1])
        rw_c, rw_l, vf_c, vf_l = rwkv_mixer(uc[..., 5 * G: 9 * G], ul[..., 5 * G: 9 * G],
                                            vf_c, vf_l, vmix, rw_mu[l], rw_w0[l], rw_w1[l],
                                            rw_w2[l], rw_a0[l], rw_a1[l], rw_a2[l], rw_g1[l],
                                            rw_g2[l], rw_k_k[l], rw_k_a[l], rw_r_k[l],
                                            rw_ln_w[l], rw_ln_b[l])
        rt_c, rt_l = retention_mixer(uc[..., 9 * G:], ul[..., 9 * G:], rows, cols,
                                     rt_decay[l], rt_gn_w[l])
        yl = jnp.concatenate([hy_l, rg_l, rw_l, rt_l], axis=-1).astype(dt) @ w_out[l]
        xl = xl + mod_l[2] * yl
        if not last:
            hy_c = hyena_mixer(uc[..., : 3 * G], *hy_p)
            yc = jnp.concatenate([hy_c, rg_c, rw_c, rt_c], axis=-1).astype(dt) @ w_out[l]
            xc = xc + mod_c[2] * yc

        moe_p = (moe_router_w[l], moe_router_b[l], moe_w_gu[l], moe_b_gu[l],
                 moe_w_dn[l], moe_b_dn[l])
        hl = modulate(xl, norm2_w[l], mod_l[3], mod_l[4])
        if last:
            xl = xl + mod_l[5] * moe(hl, *moe_p)
        else:
            hc = modulate(xc, norm2_w[l], mod_c[3], mod_c[4])
            f = moe(jnp.concatenate([hc, hl], axis=1), *moe_p)
            xc = xc + mod_c[5] * f[:, :n_ctx]
            xl = xl + mod_l[5] * f[:, n_ctx:]

    return rms_norm(xl, final_norm_w)
```

```python
import functools
import math

import jax
import jax.numpy as jnp
import numpy as np
from jax import lax
from jax.experimental import pallas as pl
from jax.experimental.pallas import tpu as pltpu

F32 = jnp.float32
BF16 = jnp.bfloat16
HIGHEST = lax.Precision.HIGHEST

GRID_W = 64
N_GROUPS = 4
N_MOD = 6
RMS_EPS = 1e-6

HY_ORDER = 2
HY_EMB = 33
HY_TARGET = 1e-2
HY_MIN_DECAY = math.log(HY_TARGET) / 1.5
HY_MAX_DECAY = math.log(HY_TARGET) / 0.3

RG_HEADS = 8
RG_CONV = 4
RG_C = 8.0

RW_HEAD = 64
RW_GN_EPS = 64e-5
RW_CHUNK = 64

RT_HEADS = 4
RT_CHUNK = 128
RT_GN_EPS = 1e-6
ROPE_BASE = 10000.0

N_EXPERTS = 32
TOP_K = 4
SWIGLU_LIMIT = 7.0
SWIGLU_ALPHA = 1.702

VMEM_LIMIT_BYTES = 52 * 1024 * 1024
LRU_CHUNK = 256
MOE_TILE = 256


def _params(*sem):
    return pltpu.CompilerParams(dimension_semantics=sem, vmem_limit_bytes=VMEM_LIMIT_BYTES)


def _modmm_kernel(x_ref, nw_ref, sh_ref, sc_ref, w_ref, o_ref, h_ref):
    @pl.when(pl.program_id(2) == 0)
    def _():
        x = x_ref[0]
        y = x * lax.rsqrt(jnp.mean(x * x, axis=-1, keepdims=True) + RMS_EPS)
        y = y * nw_ref[...] * (1.0 + sc_ref[0]) + sh_ref[0]
        h_ref[...] = y.astype(BF16)

    o_ref[0] = jnp.dot(h_ref[...], w_ref[...], preferred_element_type=F32)


def modulated_matmul(x, nw, shift, scale, w_bf16, tm, tn):
    B, L, D = x.shape
    N = w_bf16.shape[1]
    bm = shift.shape[0]
    mod_idx = (lambda b, i, j: (b, 0, 0)) if bm == B else (lambda b, i, j: (0, 0, 0))
    return pl.pallas_call(
        _modmm_kernel,
        grid=(B, L // tm, N // tn),
        in_specs=[
            pl.BlockSpec((1, tm, D), lambda b, i, j: (b, i, 0)),
            pl.BlockSpec((1, D), lambda b, i, j: (0, 0)),
            pl.BlockSpec((1, 1, D), mod_idx),
            pl.BlockSpec((1, 1, D), mod_idx),
            pl.BlockSpec((D, tn), lambda b, i, j: (0, j)),
        ],
        out_specs=pl.BlockSpec((1, tm, tn), lambda b, i, j: (b, i, j)),
        out_shape=jax.ShapeDtypeStruct((B, L, N), F32),
        scratch_shapes=[pltpu.VMEM((tm, D), BF16)],
        compiler_params=_params("parallel", "parallel", "arbitrary"),
        name="modulated_matmul",
    )(x, nw.reshape(1, D), shift.reshape(bm, 1, D), scale.reshape(bm, 1, D), w_bf16)


def _outproj_kernel(y_ref, w_ref, res_ref, g_ref, o_ref):
    acc = jnp.dot(y_ref[0].astype(BF16), w_ref[...], preferred_element_type=F32)
    o_ref[0] = res_ref[0] + g_ref[0] * acc


def gated_out_proj(y, w_bf16, res, gate, tm):
    B, L, K = y.shape
    D = w_bf16.shape[1]
    bm = gate.shape[0]
    g_idx = (lambda b, i: (b, 0, 0)) if bm == B else (lambda b, i: (0, 0, 0))
    return pl.pallas_call(
        _outproj_kernel,
        grid=(B, L // tm),
        in_specs=[
            pl.BlockSpec((1, tm, K), lambda b, i: (b, i, 0)),
            pl.BlockSpec((K, D), lambda b, i: (0, 0)),
            pl.BlockSpec((1, tm, D), lambda b, i: (b, i, 0)),
            pl.BlockSpec((1, 1, D), g_idx),
        ],
        out_specs=pl.BlockSpec((1, tm, D), lambda b, i: (b, i, 0)),
        out_shape=jax.ShapeDtypeStruct((B, L, D), F32),
        compiler_params=_params("parallel", "parallel"),
        name="gated_out_proj",
    )(y, w_bf16, res, gate.reshape(bm, 1, D))


def _rwkv_kernel(r_ref, lw_ref, k_ref, v_ref, kk_ref, kka_ref, y_ref, s_ref, *, heads):
    C = r_ref.shape[1]
    N = RW_HEAD

    @pl.when(pl.program_id(1) == 0)
    def _():
        s_ref[...] = jnp.zeros_like(s_ref)

    ii = lax.broadcasted_iota(jnp.int32, (C, C), 0)
    jj = lax.broadcasted_iota(jnp.int32, (C, C), 1)
    incl = ii >= jj
    strict = ii > jj
    eye = (ii == jj).astype(F32)

    lw = lw_ref[0]
    cum = jnp.dot(incl.astype(F32), lw, precision=HIGHEST, preferred_element_type=F32)
    cum_end = cum[C - 1:C, :]
    r = r_ref[0]
    k = k_ref[0]
    v = v_ref[0]
    kk = kk_ref[0]
    kka = kka_ref[0]
    e_neg = jnp.exp(-cum)
    e_end = jnp.exp(cum_end - cum)
    bt = kk * jnp.exp(cum - lw)
    at = -kka * e_neg
    kt = k * e_neg
    rt = r * jnp.exp(cum)
    at_end = -kka * e_end
    kt_end = k * e_end
    g_end = jnp.exp(cum_end)

    nt = (((1,), (1,)), ((), ()))
    for h in range(heads):
        sl = slice(h * N, (h + 1) * N)
        s0 = s_ref[h]
        br = jnp.concatenate([bt[:, sl], rt[:, sl]], axis=0)
        ak = jnp.concatenate([at[:, sl], kt[:, sl]], axis=0)
        a_all = lax.dot_general(br, ak, nt, precision=HIGHEST, preferred_element_type=F32)
        a_ab = jnp.where(strict, a_all[:C, :C], 0.0)
        a_bk = jnp.where(strict, a_all[:C, C:], 0.0)
        a_ra = jnp.where(incl, a_all[C:, :C], 0.0)
        a_rk = jnp.where(incl, a_all[C:, C:], 0.0)
        tinv = eye + a_ab
        p = a_ab
        for _ in range(int(math.log2(C)) - 1):
            p = jnp.dot(p, p, precision=HIGHEST, preferred_element_type=F32)
            tinv = tinv + jnp.dot(tinv, p, precision=HIGHEST, preferred_element_type=F32)
        br_s = lax.dot_general(br, s0, nt, precision=HIGHEST, preferred_element_type=F32)
        vh = v[:, sl]
        rhs = br_s[:C] + jnp.dot(a_bk, vh, precision=HIGHEST, preferred_element_type=F32)
        u = jnp.dot(tinv, rhs, precision=HIGHEST, preferred_element_type=F32)
        uv = jnp.concatenate([u, vh], axis=0)
        y = br_s[C:] + jnp.dot(jnp.concatenate([a_ra, a_rk], axis=1), uv,
                               precision=HIGHEST, preferred_element_type=F32)
        ak_end = jnp.concatenate([at_end[:, sl], kt_end[:, sl]], axis=0)
        s_ref[h] = s0 * g_end[:, sl] + jnp.dot(uv.T, ak_end, precision=HIGHEST,
                                               preferred_element_type=F32)
        y_ref[0, :, sl] = y


def rwkv_scan(r, lw, k, v, kk, kka):
    B, L, G = r.shape
    heads = G // RW_HEAD
    C = RW_CHUNK
    spec = pl.BlockSpec((1, C, G), lambda b, n: (b, n, 0))
    return pl.pallas_call(
        functools.partial(_rwkv_kernel, heads=heads),
        grid=(B, L // C),
        in_specs=[spec] * 6,
        out_specs=spec,
        out_shape=jax.ShapeDtypeStruct((B, L, G), F32),
        scratch_shapes=[pltpu.VMEM((heads, RW_HEAD, RW_HEAD), F32)],
        compiler_params=_params("parallel", "arbitrary"),
        name="rwkv_scan",
    )(r, lw, k, v, kk, kka)


def _retention_kernel(q_ref, k_ref, v_ref, dm_ref, kdec_ref, qdec_ref, cdec_ref, o_ref, s_ref, *, heads):
    C = q_ref.shape[1]
    dh = q_ref.shape[2] // heads

    @pl.when(pl.program_id(1) == 0)
    def _():
        s_ref[...] = jnp.zeros_like(s_ref)

    q = q_ref[0]
    k = k_ref[0]
    v = v_ref[0]
    qd = (q * qdec_ref[...]).astype(BF16)
    kd = (k * kdec_ref[...]).astype(BF16)
    qb = q.astype(BF16)
    kb = k.astype(BF16)
    vb = v.astype(BF16)
    cdec = cdec_ref[...]
    nt = (((1,), (1,)), ((), ()))
    for h in range(heads):
        sl = slice(h * dh, (h + 1) * dh)
        s0 = s_ref[h]
        inner = lax.dot_general(qb[:, sl], kb[:, sl], nt, preferred_element_type=F32) * dm_ref[h]
        out = jnp.dot(inner.astype(BF16), vb[:, sl], preferred_element_type=F32)
        out = out + jnp.dot(qd[:, sl], s0.astype(BF16), preferred_element_type=F32)
        kv = jnp.dot(kd[:, sl].T, vb[:, sl], preferred_element_type=F32)
        s_ref[h] = cdec[:, sl] * s0 + kv
        o_ref[0, :, sl] = out


def retention_scan(q, k, v, lg, backward):
    B, L, G = q.shape
    H = RT_HEADS
    dh = G // H
    C = RT_CHUNK
    idx = jnp.arange(C, dtype=F32)
    diff = idx[:, None] - idx[None, :]
    keep = diff > 0 if backward else diff >= 0
    dmask = jnp.where(keep, jnp.exp(jnp.where(keep, diff, 0.0)[None] * lg[:, None, None]), 0.0)
    kdec = jnp.repeat(jnp.exp((C - 1 - idx)[:, None] * lg[None, :]), dh, axis=1)
    qdec = jnp.repeat(jnp.exp((idx + 1.0)[:, None] * lg[None, :]), dh, axis=1)
    cdec = jnp.repeat(jnp.exp(C * lg), dh)[None, :]
    spec = pl.BlockSpec((1, C, G), lambda b, n: (b, n, 0))
    return pl.pallas_call(
        functools.partial(_retention_kernel, heads=H),
        grid=(B, L // C),
        in_specs=[spec, spec, spec,
                  pl.BlockSpec((H, C, C), lambda b, n: (0, 0, 0)),
                  pl.BlockSpec((C, G), lambda b, n: (0, 0)),
                  pl.BlockSpec((C, G), lambda b, n: (0, 0)),
                  pl.BlockSpec((1, G), lambda b, n: (0, 0))],
        out_specs=spec,
        out_shape=jax.ShapeDtypeStruct((B, L, G), F32),
        scratch_shapes=[pltpu.VMEM((H, dh, dh), F32)],
        compiler_params=_params("parallel", "arbitrary"),
        name="retention_scan",
    )(q, k, v, dmask, kdec, qdec, cdec)


def _lru_kernel(a_ref, b_ref, h_ref, carry_ref):
    T = a_ref.shape[1]

    @pl.when(pl.program_id(1) == 0)
    def _():
        carry_ref[...] = jnp.zeros_like(carry_ref)

    a = a_ref[0]
    b = b_ref[0]
    row = lax.broadcasted_iota(jnp.int32, a.shape, 0)
    s = 1
    while s < T:
        m = row >= s
        b = jnp.where(m, a * pltpu.roll(b, s, 0) + b, b)
        a = jnp.where(m, a * pltpu.roll(a, s, 0), a)
        s *= 2
    h = b + a * carry_ref[...]
    h_ref[0] = h
    carry_ref[...] = h[T - 1:T, :]


def lru_scan(a, b):
    B, L, G = a.shape
    T = LRU_CHUNK
    spec = pl.BlockSpec((1, T, G), lambda i, n: (i, n, 0))
    return pl.pallas_call(
        _lru_kernel,
        grid=(B, L // T),
        in_specs=[spec, spec],
        out_specs=spec,
        out_shape=jax.ShapeDtypeStruct((B, L, G), F32),
        scratch_shapes=[pltpu.VMEM((1, G), F32)],
        compiler_params=_params("parallel", "arbitrary"),
        name="lru_scan",
    )(a, b)


def _moe_kernel(te_ref, nt_ref, x_ref, wgu_ref, bgu_ref, wdn_ref, bdn_ref, g_ref, o_ref):
    i = pl.program_id(0)
    F = wdn_ref.shape[1]

    @pl.when(i < nt_ref[0])
    def _():
        gu = jnp.dot(x_ref[...], wgu_ref[0], preferred_element_type=F32) + bgu_ref[0]
        glu = jnp.minimum(gu[:, :F], SWIGLU_LIMIT)
        lin = jnp.clip(gu[:, F:], -SWIGLU_LIMIT, SWIGLU_LIMIT)
        act = glu * jax.nn.sigmoid(SWIGLU_ALPHA * glu) * (lin + 1.0)
        y = jnp.dot(act.astype(BF16), wdn_ref[0], preferred_element_type=F32) + bdn_ref[0]
        o_ref[...] = y * g_ref[...]

    @pl.when(i >= nt_ref[0])
    def _():
        o_ref[...] = jnp.zeros_like(o_ref)


def moe_grouped(xs, tile_expert, n_tiles_used, w_gu, b_gu, w_dn, b_dn, row_gate):
    P, D = xs.shape
    E, _, F2 = w_gu.shape
    F = F2 // 2
    tm = MOE_TILE
    grid_spec = pltpu.PrefetchScalarGridSpec(
        num_scalar_prefetch=2,
        grid=(P // tm,),
        in_specs=[
            pl.BlockSpec((tm, D), lambda i, te, nt: (i, 0)),
            pl.BlockSpec((1, D, F2), lambda i, te, nt: (te[i], 0, 0)),
            pl.BlockSpec((1, 1, F2), lambda i, te, nt: (te[i], 0, 0)),
            pl.BlockSpec((1, F, D), lambda i, te, nt: (te[i], 0, 0)),
            pl.BlockSpec((1, 1, D), lambda i, te, nt: (te[i], 0, 0)),
            pl.BlockSpec((tm, 1), lambda i, te, nt: (i, 0)),
        ],
        out_specs=pl.BlockSpec((tm, D), lambda i, te, nt: (i, 0)),
    )
    return pl.pallas_call(
        _moe_kernel,
        grid_spec=grid_spec,
        out_shape=jax.ShapeDtypeStruct((P, D), F32),
        compiler_params=_params("arbitrary"),
        name="moe_grouped",
    )(tile_expert, n_tiles_used, xs, w_gu, b_gu.reshape(E, 1, F2), w_dn, b_dn.reshape(E, 1, D),
      row_gate)


def moe(t, router_w, router_b, w_gu_bf16, b_gu, w_dn_bf16, b_dn):
    T, D = t.shape
    E = N_EXPERTS
    tm = MOE_TILE
    logits = jnp.dot(t, router_w, precision=HIGHEST) + router_b
    vals, idx = lax.top_k(logits, TOP_K)
    probs = jax.nn.softmax(vals, axis=-1)
    onehot = jnp.sum(jax.nn.one_hot(idx, E, dtype=jnp.int32), axis=1)
    rank = jnp.cumsum(onehot, axis=0) - onehot
    sizes = jnp.sum(onehot, axis=0)
    padded = ((sizes + tm - 1) // tm) * tm
    ends = jnp.cumsum(padded)
    starts = ends - padded
    dest = starts[idx] + jnp.take_along_axis(rank, idx, axis=1)
    n_rows = T * TOP_K + E * tm
    flat_dest = dest.reshape(-1)
    tok = jnp.repeat(jnp.arange(T, dtype=jnp.int32), TOP_K)
    row_token = jnp.zeros((n_rows,), jnp.int32).at[flat_dest].set(tok)
    row_gate = jnp.zeros((n_rows,), F32).at[flat_dest].set(probs.reshape(-1))
    xs = t.astype(BF16)[row_token]
    tile_start = jnp.arange(n_rows // tm, dtype=jnp.int32) * tm
    tile_expert = jnp.minimum(jnp.searchsorted(ends, tile_start, side='right'), E - 1).astype(jnp.int32)
    n_used = (ends[-1] // tm).astype(jnp.int32).reshape(1)
    ys = moe_grouped(xs, tile_expert, n_used, w_gu_bf16, b_gu, w_dn_bf16, b_dn,
                     row_gate.reshape(n_rows, 1))
    return jnp.sum(ys[dest], axis=1)


def shift_prev(x, n):
    if n == 0:
        return x
    return jnp.pad(x, ((0, 0), (n, 0), (0, 0)))[:, : x.shape[1]]


def shift_next(x, n):
    return jnp.pad(x, ((0, 0), (0, n), (0, 0)))[:, n:]


def rope_2d(x, rows, cols):
    half = x.shape[-1] // 2
    quarter = half // 2
    inv = jnp.power(ROPE_BASE, -jnp.arange(quarter, dtype=F32) / quarter)

    def rot(xp, pos):
        ang = pos[:, None] * inv[None, :]
        cos = jnp.cos(ang)[None, :, None, :]
        sin = jnp.sin(ang)[None, :, None, :]
        x1, x2 = xp[..., :quarter], xp[..., quarter:]
        return jnp.concatenate([x1 * cos - x2 * sin, x1 * sin + x2 * cos], axis=-1)

    return jnp.concatenate([rot(x[..., :half], rows), rot(x[..., half:], cols)], axis=-1)


def flip_t(t):
    return jnp.flip(t, axis=1)


def scan_order(ctx_t, lat_t, d):
    if d == 1:
        ctx_t, lat_t = flip_t(ctx_t), flip_t(lat_t)
    return jnp.concatenate([ctx_t, lat_t], axis=1)


def unscan(t, n_ctx, d):
    c, l = t[:, :n_ctx], t[:, n_ctx:]
    if d == 1:
        c, l = flip_t(c), flip_t(l)
    return c, l


def hyena_filter_spectrum(L, G, w1, b1, w2, b2, w3, b3, w4, freq):
    t = jnp.linspace(0.0, 1.0, L, dtype=F32)[:, None]
    bands = (HY_EMB - 1) // 2
    fr = jnp.linspace(1e-4, bands - 1, bands, dtype=F32)
    ang = (2.0 * math.pi / L) * jnp.arange(L, dtype=F32)[:, None] * fr[None, :]
    z = jnp.concatenate([t, jnp.cos(ang), -jnp.sin(ang)], axis=-1)
    h = jnp.sin(freq * (z @ w1 + b1))
    h = jnp.sin(freq * (h @ w2 + b2))
    h = jnp.sin(freq * (h @ w3 + b3))
    h = (h @ w4).reshape(L, 2, HY_ORDER, G)
    deltas = jnp.abs(jnp.linspace(HY_MIN_DECAY, HY_MAX_DECAY, G, dtype=F32))
    h = h * jnp.exp(-t * deltas)[:, None, None, :]
    taps = jnp.concatenate([h[:, 0], jnp.zeros((1, HY_ORDER, G), F32), h[:0:-1, 1]], axis=0)
    taps = taps / jnp.sum(jnp.abs(taps), axis=0, keepdims=True)
    return jnp.fft.rfft(taps, axis=0)


def hyena_mixer(u, short_w, short_b, fw1, fb1, fw2, fb2, fw3, fb3, fw4, ffreq, bias):
    L = u.shape[1]
    G = u.shape[2] // 3
    u = short_w[0] * shift_prev(u, 1) + short_w[1] * u + short_w[2] * shift_next(u, 1) + short_b
    v, x1, x2 = jnp.split(u, 3, axis=-1)
    spec = hyena_filter_spectrum(L, G, fw1, fb1, fw2, fb2, fw3, fb3, fw4, ffreq)
    z = v
    for o, gate in enumerate((x1, x2)):
        zf = jnp.fft.rfft(z, n=2 * L, axis=1)
        z = jnp.fft.irfft(zf * spec[None, :, o], n=2 * L, axis=1)[:, :L] + bias[o] * z
        z = gate * z
    return z


def rglru_mixer(s_ctx, s_lat, conv_w, conv_b, wa, ba, wx, bx, lam):
    n_ctx = s_ctx.shape[1]
    xc_, gc_ = jnp.split(s_ctx, 2, axis=-1)
    xl_, gl_ = jnp.split(s_lat, 2, axis=-1)
    out_c, out_l = 0.0, 0.0
    for d in range(2):
        a_parts, b_parts = [], []
        for xb in (xc_, xl_):
            if d == 1:
                xb = flip_t(xb)
            xc = conv_b[d] + sum(conv_w[d, j] * shift_prev(xb, RG_CONV - 1 - j) for j in range(RG_CONV))
            B_, L = xb.shape[0], xb.shape[1]
            xh = xc.reshape(B_, L, RG_HEADS, -1)
            r = jax.nn.sigmoid(jnp.einsum('blhi,hij->blhj', xh, wa[d]).reshape(B_, L, -1) + ba[d])
            i = jax.nn.sigmoid(jnp.einsum('blhi,hij->blhj', xh, wx[d]).reshape(B_, L, -1) + bx[d])
            log_a = -RG_C * r * jax.nn.softplus(-lam[d])
            a_parts.append(jnp.exp(log_a))
            b_parts.append(jnp.sqrt(-jnp.expm1(2.0 * log_a)) * (i * xc))
        h = lru_scan(jnp.concatenate(a_parts, axis=1), jnp.concatenate(b_parts, axis=1))
        oc, ol = unscan(h, n_ctx, d)
        out_c = out_c + oc
        out_l = out_l + ol
    return out_c * jax.nn.gelu(gc_), out_l * jax.nn.gelu(gl_)


def rwkv_mixer(s_ctx, s_lat, vf_ctx, vf_lat, vmix, mu, w0, w1, w2, a0, a1, a2,
               g1, g2, k_k, k_a, r_k, ln_w, ln_b):
    n_ctx = s_ctx.shape[1]
    G = s_ctx.shape[-1] // 4
    H = G // RW_HEAD

    def prep(s, vfirst):
        r, k, v, z = jnp.split(s, 4, axis=-1)
        if vmix is None:
            vfirst = v
        else:
            v0, v1, v2 = vmix
            v = v + (vfirst - v) * jax.nn.sigmoid(v0 + (z @ v1) @ v2)
        g = jax.nn.sigmoid(z @ g1) @ g2
        return (r, k, v, z), g, vfirst

    in_c, g_c, vf_c = prep(s_ctx, vf_ctx)
    in_l, g_l, vf_l = prep(s_lat, vf_lat)

    out_c, out_l = 0.0, 0.0
    for d in range(2):
        parts = []
        for inp in (in_c, in_l):
            if d == 1:
                inp = tuple(flip_t(t) for t in inp)
            r, k, v, z = [s + (shift_prev(s, 1) - s) * mu[d, j] for j, s in enumerate(inp)]
            w_log = -jax.nn.softplus(-(w0[d] + jnp.tanh(z @ w1[d]) @ w2[d])) - 0.5
            lw = -jnp.exp(w_log)
            a = jax.nn.sigmoid(a0[d] + (z @ a1[d]) @ a2[d])
            B_, L = r.shape[0], r.shape[1]
            heads = lambda t: t.reshape(B_, L, H, RW_HEAD)
            kk = heads(k * k_k)
            kk = kk / jnp.maximum(jnp.sqrt(jnp.sum(kk * kk, axis=-1, keepdims=True)), 1e-12)
            kk = kk.reshape(B_, L, G)
            k = k * (1 + (a - 1) * k_a)
            parts.append((r, lw, k, v, kk, kk * a))
        cat = [jnp.concatenate([pc, pl_], axis=1) for pc, pl_ in zip(*parts)]
        r, lw, k, v, kk, kka = cat
        y = rwkv_scan(r, lw, k, v, kk, kka)
        B_, L = y.shape[0], y.shape[1]
        yh = y.reshape(B_, L, H, RW_HEAD)
        m = jnp.mean(yh, axis=-1, keepdims=True)
        var = jnp.mean(jnp.square(yh - m), axis=-1, keepdims=True)
        yn = ((yh - m) * lax.rsqrt(var + RW_GN_EPS)).reshape(B_, L, -1) * ln_w + ln_b
        rh, kh, vh = (t.reshape(B_, L, H, RW_HEAD) for t in (r, k, v))
        bonus = (jnp.sum(rh * kh * r_k, axis=-1, keepdims=True) * vh).reshape(B_, L, -1)
        oc, ol = unscan(yn + bonus, n_ctx, d)
        out_c = out_c + oc
        out_l = out_l + ol
    return out_c * g_c, out_l * g_l, vf_c, vf_l


def retention_mixer(s_ctx, s_lat, rows, cols, decay_raw, gn_w):
    n_ctx = s_ctx.shape[1]
    G = s_ctx.shape[-1] // 4
    dh = G // RT_HEADS

    def prep(s, rotate):
        q, k, v, g = jnp.split(s, 4, axis=-1)
        B_, L = s.shape[0], s.shape[1]
        if rotate:
            q = rope_2d(q.reshape(B_, L, RT_HEADS, dh), rows, cols).reshape(B_, L, G)
            k = rope_2d(k.reshape(B_, L, RT_HEADS, dh), rows, cols).reshape(B_, L, G)
        return (q, k * dh ** -0.5, v), g

    in_c, g_c = prep(s_ctx, False)
    in_l, g_l = prep(s_lat, True)
    out_c, out_l = 0.0, 0.0
    for d in range(2):
        lg = -jax.nn.softplus(decay_raw[d].astype(F32))
        q, k, v = (scan_order(c_, l_, d) for c_, l_ in zip(in_c, in_l))
        o = retention_scan(q, k, v, lg, backward=(d == 1))
        oc, ol = unscan(o, n_ctx, d)
        out_c = out_c + oc
        out_l = out_l + ol

    def finish(o, g):
        B_, L = o.shape[0], o.shape[1]
        oh = o.reshape(B_, L, RT_HEADS, dh)
        m = jnp.mean(oh, axis=-1, keepdims=True)
        var = jnp.mean(jnp.square(oh - m), axis=-1, keepdims=True)
        on = ((oh - m) * lax.rsqrt(var + RT_GN_EPS)).reshape(B_, L, -1)
        return on * gn_w * jax.nn.silu(g)

    return finish(out_c, g_c), finish(out_l, g_l)


def rms_norm(x, w):
    y = x * lax.rsqrt(jnp.mean(x * x, axis=-1, keepdims=True) + RMS_EPS)
    return y * w


def modulate(x, w, shift, scale):
    return rms_norm(x, w) * (1 + scale[:, None, :]) + shift[:, None, :]


def kernel(x, c, ctx, c_ctx, ada_w, ada_b, norm1_w, norm2_w, w_in, w_out, hy_short_w, hy_short_b, hy_f_w1, hy_f_b1, hy_f_w2, hy_f_b2, hy_f_w3, hy_f_b3, hy_f_w4, hy_f_freq, hy_bias, rg_conv_w, rg_conv_b, rg_wa, rg_ba, rg_wx, rg_bx, rg_lambda, rw_mu, rw_w0, rw_w1, rw_w2, rw_a0, rw_a1, rw_a2, rw_g1, rw_g2, rw_k_k, rw_k_a, rw_r_k, rw_ln_w, rw_ln_b, rw_v0, rw_v1, rw_v2, rt_decay, rt_gn_w, moe_router_w, moe_router_b, moe_w_gu, moe_b_gu, moe_w_dn, moe_b_dn, final_norm_w):
    depth = ada_w.shape[0]
    B, L, D = x.shape
    n_ctx = ctx.shape[1]
    G = D // N_GROUPS
    n_rows = L // GRID_W
    rows = jnp.repeat(jnp.arange(n_rows, dtype=F32), GRID_W)
    cols = jnp.tile(jnp.arange(GRID_W, dtype=F32), n_rows)
    cond_lat = jax.nn.silu(c.astype(F32))
    cond_ctx = jax.nn.silu(c_ctx.astype(F32))[None, :]
    xl, xc = x, ctx.astype(x.dtype)
    vf_c = vf_l = None

    for l in range(depth):
        last = l == depth - 1
        aw = ada_w[l].astype(F32)
        mod_l = jnp.split(cond_lat @ aw + ada_b[l], N_MOD, axis=-1)
        mod_c = jnp.split(cond_ctx @ aw + ada_b[l], N_MOD, axis=-1)

        w_in_b = w_in[l].astype(BF16)
        w_out_b = w_out[l].astype(BF16)
        ul = modulated_matmul(xl, norm1_w[l], mod_l[0], mod_l[1], w_in_b, tm=512, tn=512)
        uc = modulated_matmul(xc, norm1_w[l], mod_c[0], mod_c[1], w_in_b, tm=n_ctx, tn=512)

        hy_p = (hy_short_w[l], hy_short_b[l], hy_f_w1[l], hy_f_b1[l], hy_f_w2[l], hy_f_b2[l],
                hy_f_w3[l], hy_f_b3[l], hy_f_w4[l], hy_f_freq[l], hy_bias[l])
        hy_l = hyena_mixer(ul[..., : 3 * G], *hy_p)
        rg_c, rg_l = rglru_mixer(uc[..., 3 * G: 5 * G], ul[..., 3 * G: 5 * G],
                                 rg_conv_w[l], rg_conv_b[l], rg_wa[l], rg_ba[l],
                                 rg_wx[l], rg_bx[l], rg_lambda[l])
        vmix = None if l == 0 else (rw_v0[l - 1], rw_v1[l - 1], rw_v2[l - 1])
        rw_c, rw_l, vf_c, vf_l = rwkv_mixer(uc[..., 5 * G: 9 * G], ul[..., 5 * G: 9 * G],
                                            vf_c, vf_l, vmix, rw_mu[l], rw_w0[l], rw_w1[l],
                                            rw_w2[l], rw_a0[l], rw_a1[l], rw_a2[l], rw_g1[l],
                                            rw_g2[l], rw_k_k[l], rw_k_a[l], rw_r_k[l],
                                            rw_ln_w[l], rw_ln_b[l])
        rt_c, rt_l = retention_mixer(uc[..., 9 * G:], ul[..., 9 * G:], rows, cols,
                                     rt_decay[l], rt_gn_w[l])
        yl = jnp.concatenate([hy_l, rg_l, rw_l, rt_l], axis=-1)
        xl = gated_out_proj(yl, w_out_b, xl, mod_l[2], tm=256)
        if not last:
            hy_c = hyena_mixer(uc[..., : 3 * G], *hy_p)
            yc = jnp.concatenate([hy_c, rg_c, rw_c, rt_c], axis=-1)
            xc = gated_out_proj(yc, w_out_b, xc, mod_c[2], tm=n_ctx)

        w_gu_b = moe_w_gu[l].astype(BF16)
        w_dn_b = moe_w_dn[l].astype(BF16)
        moe_p = (moe_router_w[l], moe_router_b[l], w_gu_b, moe_b_gu[l], w_dn_b, moe_b_dn[l])
        hl = modulate(xl, norm2_w[l], mod_l[3], mod_l[4])
        if last:
            f = moe(hl.reshape(-1, D), *moe_p).reshape(B, L, D)
            xl = xl + mod_l[5][:, None, :] * f
        else:
            hc = modulate(xc, norm2_w[l], mod_c[3], mod_c[4])
            f = moe(jnp.concatenate([hc, hl], axis=1).reshape(-1, D), *moe_p).reshape(B, n_ctx + L, D)
            xc = xc + mod_c[5][:, None, :] * f[:, :n_ctx]
            xl = xl + mod_l[5][:, None, :] * f[:, n_ctx:]

    return rms_norm(xl, final_norm_w)
```

```python
import functools
import math

import jax
import jax.numpy as jnp
from jax import lax
from jax.experimental import pallas as pl
from jax.experimental.pallas import tpu as pltpu

F32 = jnp.float32
BF16 = jnp.bfloat16
HIGHEST = lax.Precision.HIGHEST

GRID_W = 64
N_GROUPS = 4
N_MOD = 6
RMS_EPS = 1e-6

HY_ORDER = 2
HY_EMB = 33
HY_TARGET = 1e-2
HY_MIN_DECAY = math.log(HY_TARGET) / 1.5
HY_MAX_DECAY = math.log(HY_TARGET) / 0.3

RG_HEADS = 8
RG_CONV = 4
RG_C = 8.0

RW_HEAD = 64
RW_GN_EPS = 64e-5
RW_CHUNK = 64
RW_SUB = 16

RT_HEADS = 4
RT_CHUNK = 128
RT_GN_EPS = 1e-6
ROPE_BASE = 10000.0

N_EXPERTS = 32
TOP_K = 4
SWIGLU_LIMIT = 7.0
SWIGLU_ALPHA = 1.702

VMEM_LIMIT_BYTES = 52 * 1024 * 1024
LRU_CHUNK = 256
MOE_TILE = 256
ROW_TILE = 256

NT_DIMS = (((1,), (1,)), ((), ()))


def _params(*sem):
    return pltpu.CompilerParams(dimension_semantics=sem, vmem_limit_bytes=VMEM_LIMIT_BYTES)


def _scan_block(n, n_ctx_blocks, n_blocks, reverse):
    if not reverse:
        return n
    return jnp.where(n < n_ctx_blocks, n_ctx_blocks - 1 - n, n_blocks - 1 + n_ctx_blocks - n)


def _shift_rows(cur, prv, j, reverse):
    T = cur.shape[0]
    row = lax.broadcasted_iota(jnp.int32, cur.shape, 0)
    if not reverse:
        return jnp.where(row < j, pltpu.roll(prv, j, 0), pltpu.roll(cur, j, 0))
    return jnp.where(row >= T - j, pltpu.roll(prv, T - j, 0), pltpu.roll(cur, T - j, 0))


def _mm(a, b):
    return jnp.dot(a.astype(BF16), b.astype(BF16), preferred_element_type=F32)


def _mm_nt(a, b):
    return lax.dot_general(a.astype(BF16), b.astype(BF16), NT_DIMS, preferred_element_type=F32)


def _split_dot(x, m_bf16, parts):
    acc = None
    for _ in range(parts):
        hi = x.astype(BF16)
        d = jnp.dot(hi, m_bf16, preferred_element_type=F32)
        acc = d if acc is None else acc + d
        x = x - hi.astype(F32)
    return acc


def _expm1(x):
    small = x * (1.0 + x * (0.5 + x * (1.0 / 6.0 + x * (1.0 / 24.0 + x * (1.0 / 120.0)))))
    return jnp.where(jnp.abs(x) < 0.1, small, jnp.exp(x) - 1.0)


def _softplus(x):
    return jnp.maximum(x, 0.0) + jnp.log(1.0 + jnp.exp(-jnp.abs(x)))


def _sigmoid(x):
    return 1.0 / (1.0 + jnp.exp(-x))


def _modmm_kernel(x_ref, nw_ref, sh_ref, sc_ref, w_ref, o_ref, h_ref):
    @pl.when(pl.program_id(2) == 0)
    def _():
        x = x_ref[0]
        y = x * lax.rsqrt(jnp.mean(x * x, axis=-1, keepdims=True) + RMS_EPS)
        y = y * nw_ref[...] * (1.0 + sc_ref[0]) + sh_ref[0]
        h_ref[...] = y.astype(BF16)

    o_ref[0] = jnp.dot(h_ref[...], w_ref[...], preferred_element_type=F32)


def modulated_matmul(x, nw, shift, scale, w_bf16, n_ctx, tn):
    B, L, D = x.shape
    N = w_bf16.shape[1]
    tm = ROW_TILE
    ncb = n_ctx // tm
    mod_idx = lambda b, i, j: (2 * b + (i >= ncb).astype(jnp.int32), 0, 0)
    return pl.pallas_call(
        _modmm_kernel,
        grid=(B, L // tm, N // tn),
        in_specs=[
            pl.BlockSpec((1, tm, D), lambda b, i, j: (b, i, 0)),
            pl.BlockSpec((1, D), lambda b, i, j: (0, 0)),
            pl.BlockSpec((1, 1, D), mod_idx),
            pl.BlockSpec((1, 1, D), mod_idx),
            pl.BlockSpec((D, tn), lambda b, i, j: (0, j)),
        ],
        out_specs=pl.BlockSpec((1, tm, tn), lambda b, i, j: (b, i, j)),
        out_shape=jax.ShapeDtypeStruct((B, L, N), F32),
        scratch_shapes=[pltpu.VMEM((tm, D), BF16)],
        compiler_params=_params("parallel", "parallel", "arbitrary"),
        name="modulated_matmul",
    )(x, nw.reshape(1, D), shift.reshape(2 * B, 1, D), scale.reshape(2 * B, 1, D), w_bf16)


def _outproj_kernel(hy_ref, rg0_ref, rg1_ref, rw0_ref, rw1_ref, rt_ref, w_ref, res_ref, g_ref, o_ref):
    G = hy_ref.shape[2]
    slabs = (hy_ref[0], rg0_ref[0] + rg1_ref[0], rw0_ref[0] + rw1_ref[0], rt_ref[0])
    acc = None
    for i, s in enumerate(slabs):
        d = jnp.dot(s.astype(BF16), w_ref[i * G:(i + 1) * G, :], preferred_element_type=F32)
        acc = d if acc is None else acc + d
    o_ref[0] = res_ref[0] + g_ref[0] * acc


def gated_out_proj(slabs, w_bf16, res, gate, n_ctx):
    B, L, D = res.shape
    G = slabs[0].shape[2]
    tm = ROW_TILE
    ncb = n_ctx // tm
    g_idx = lambda b, i: (2 * b + (i >= ncb).astype(jnp.int32), 0, 0)
    slab_spec = pl.BlockSpec((1, tm, G), lambda b, i: (b, i, 0))
    return pl.pallas_call(
        _outproj_kernel,
        grid=(B, L // tm),
        in_specs=[slab_spec] * 6 + [
            pl.BlockSpec((N_GROUPS * G, D), lambda b, i: (0, 0)),
            pl.BlockSpec((1, tm, D), lambda b, i: (b, i, 0)),
            pl.BlockSpec((1, 1, D), g_idx),
        ],
        out_specs=pl.BlockSpec((1, tm, D), lambda b, i: (b, i, 0)),
        out_shape=jax.ShapeDtypeStruct((B, L, D), F32),
        compiler_params=_params("parallel", "parallel"),
        name="gated_out_proj",
    )(*slabs, w_bf16, res, gate.reshape(2 * B, 1, D))


def _rglru_kernel(x_ref, gate_ref, cw_ref, cb_ref, wa_ref, ba_ref, wx_ref, bx_ref, lam_ref,
                  o_ref, prev_ref, carry_ref, *, n_ctx_blocks, reverse):
    T = x_ref.shape[1]
    n = pl.program_id(1)

    @pl.when(n == 0)
    def _():
        carry_ref[...] = jnp.zeros_like(carry_ref)

    @pl.when((n == 0) | (n == n_ctx_blocks))
    def _():
        prev_ref[...] = jnp.zeros_like(prev_ref)

    x = x_ref[0]
    prv = prev_ref[...]
    xc = cb_ref[...] + cw_ref[RG_CONV - 1:RG_CONV, :] * x
    for j in range(1, RG_CONV):
        xc = xc + cw_ref[RG_CONV - 1 - j:RG_CONV - j, :] * _shift_rows(x, prv, j, reverse)
    prev_ref[...] = x

    xb = xc.astype(BF16)
    r = _sigmoid(jnp.dot(xb, wa_ref[...], preferred_element_type=F32) + ba_ref[...])
    i = _sigmoid(jnp.dot(xb, wx_ref[...], preferred_element_type=F32) + bx_ref[...])
    log_a = -RG_C * r * _softplus(-lam_ref[...])
    a = jnp.exp(log_a)
    b = jnp.sqrt(-_expm1(2.0 * log_a)) * (i * xc)

    row = lax.broadcasted_iota(jnp.int32, a.shape, 0)
    s = 1
    while s < T:
        if not reverse:
            m = row >= s
            b = jnp.where(m, a * pltpu.roll(b, s, 0) + b, b)
            a = jnp.where(m, a * pltpu.roll(a, s, 0), a)
        else:
            m = row < T - s
            b = jnp.where(m, a * pltpu.roll(b, T - s, 0) + b, b)
            a = jnp.where(m, a * pltpu.roll(a, T - s, 0), a)
        s *= 2
    h = b + a * carry_ref[...]
    carry_ref[...] = h[0:1, :] if reverse else h[T - 1:T, :]
    o_ref[0] = h * jax.nn.gelu(gate_ref[0])


def _block_diag(w):
    H, di, dj = w.shape
    eye = jnp.eye(H, dtype=w.dtype)
    return (eye[:, None, :, None] * w[:, :, None, :]).reshape(H * di, H * dj)


def rglru_direction(u, col0, n_ctx, conv_w, conv_b, wa, ba, wx, bx, lam, reverse):
    B, L, _ = u.shape
    G = conv_w.shape[-1]
    T = LRU_CHUNK
    nb, ncb = L // T, n_ctx // T
    blk = lambda c: pl.BlockSpec((1, T, G), lambda b, n: (b, _scan_block(n, ncb, nb, reverse), c))
    row = pl.BlockSpec((1, G), lambda b, n: (0, 0))
    mat = pl.BlockSpec((G, G), lambda b, n: (0, 0))
    return pl.pallas_call(
        functools.partial(_rglru_kernel, n_ctx_blocks=ncb, reverse=reverse),
        grid=(B, nb),
        in_specs=[blk(col0), blk(col0 + 1), pl.BlockSpec((RG_CONV, G), lambda b, n: (0, 0)),
                  row, mat, row, mat, row, row],
        out_specs=pl.BlockSpec((1, T, G), lambda b, n: (b, _scan_block(n, ncb, nb, reverse), 0)),
        out_shape=jax.ShapeDtypeStruct((B, L, G), F32),
        scratch_shapes=[pltpu.VMEM((T, G), F32), pltpu.VMEM((1, G), F32)],
        compiler_params=_params("parallel", "arbitrary"),
        name="rglru_mixer",
    )(u, u, conv_w, conv_b.reshape(1, G), _block_diag(wa).astype(BF16), ba.reshape(1, G),
      _block_diag(wx).astype(BF16), bx.reshape(1, G), lam.reshape(1, G))


def _rwkv_kernel(*refs, heads, n_ctx_blocks, reverse, has_vmix):
    it = iter(refs)
    r_ref, k_ref, v_ref, z_ref = next(it), next(it), next(it), next(it)
    vf_ref = next(it) if has_vmix else None
    mu_ref, w0_ref, w1_ref, w2_ref, a0_ref, a1_ref, a2_ref = (next(it) for _ in range(7))
    g1_ref, g2_ref, kk_ref, ka_ref, rk_ref, lnw_ref, lnb_ref = (next(it) for _ in range(7))
    if has_vmix:
        v0_ref, v1_ref, v2_ref = next(it), next(it), next(it)
    ones_ref = next(it)
    y_ref, s_ref, prev_ref = next(it), next(it), next(it)

    C = r_ref.shape[1]
    N = RW_HEAD
    n = pl.program_id(1)

    @pl.when(n == 0)
    def _():
        s_ref[...] = jnp.zeros_like(s_ref)

    @pl.when((n == 0) | (n == n_ctx_blocks))
    def _():
        prev_ref[...] = jnp.zeros_like(prev_ref)

    raw = [r_ref[0], k_ref[0], v_ref[0], z_ref[0]]
    z_raw = raw[3]
    if has_vmix:
        vm = _sigmoid(v0_ref[...] + _mm(_mm(z_raw, v1_ref[...]), v2_ref[...]))
        raw[2] = raw[2] + (vf_ref[0] - raw[2]) * vm
    gate = _mm(_sigmoid(_mm(z_raw, g1_ref[...])), g2_ref[...])
    mixed = []
    for j in range(4):
        prv = prev_ref[j]
        mixed.append(raw[j] + (_shift_rows(raw[j], prv, 1, reverse) - raw[j]) * mu_ref[j:j + 1, :])
        prev_ref[j] = raw[j]
    r, k, v, z = mixed
    w_log = -_softplus(-(w0_ref[...] + _mm(jnp.tanh(_mm(z, w1_ref[...])), w2_ref[...]))) - 0.5
    lw = -jnp.exp(w_log)
    a = _sigmoid(a0_ref[...] + _mm(_mm(z, a1_ref[...]), a2_ref[...]))
    ones_bd = ones_ref[...]
    kk = k * kk_ref[...]
    kk = kk / jnp.maximum(jnp.sqrt(_split_dot(kk * kk, ones_bd, 2)), 1e-12)
    k = k * (1.0 + (a - 1.0) * ka_ref[...])
    kka = kk * a

    ii = lax.broadcasted_iota(jnp.int32, (C, C), 0)
    jj = lax.broadcasted_iota(jnp.int32, (C, C), 1)
    incl = (ii <= jj) if reverse else (ii >= jj)
    strict = (ii < jj) if reverse else (ii > jj)
    same_sub = (ii & -RW_SUB) == (jj & -RW_SUB)
    eye = (ii == jj).astype(F32)

    cum = None
    lw_part = lw
    for _ in range(3):
        hi = lw_part.astype(BF16)
        d = jnp.dot(incl.astype(BF16), hi, preferred_element_type=F32)
        cum = d if cum is None else cum + d
        lw_part = lw_part - hi.astype(F32)
    cum_end = cum[0:1, :] if reverse else cum[C - 1:C, :]
    e_neg = jnp.exp(-cum)
    e_end = jnp.exp(cum_end - cum)
    bt = kk * jnp.exp(cum - lw)
    at = -kka * e_neg
    kt = k * e_neg
    rt = r * jnp.exp(cum)
    at_end = -kka * e_end
    kt_end = k * e_end
    g_end = jnp.exp(cum_end)

    hs = range(heads)
    sl = [slice(h * N, (h + 1) * N) for h in hs]
    s0 = [s_ref[h] for h in hs]
    vh = [v[:, sl[h]] for h in hs]
    br = [jnp.concatenate([bt[:, sl[h]], rt[:, sl[h]]], axis=0) for h in hs]
    ak = [jnp.concatenate([at[:, sl[h]], kt[:, sl[h]]], axis=0) for h in hs]
    a_all = [_mm_nt(br[h], ak[h]) for h in hs]
    a_ab = [jnp.where(strict, a_all[h][:C, :C], 0.0) for h in hs]
    a_bk = [jnp.where(strict, a_all[h][:C, C:], 0.0) for h in hs]
    a_r = [jnp.concatenate([jnp.where(incl, a_all[h][C:, :C], 0.0),
                            jnp.where(incl, a_all[h][C:, C:], 0.0)], axis=1) for h in hs]
    a_d = [jnp.where(same_sub, a_ab[h], 0.0) for h in hs]
    a_o = [jnp.where(same_sub, 0.0, a_ab[h]) for h in hs]
    tinv = [eye + a_d[h] for h in hs]
    p = a_d
    for _ in range(int(math.log2(RW_SUB)) - 1):
        p = [_mm(p[h], p[h]) for h in hs]
        tinv = [tinv[h] + _mm(tinv[h], p[h]) for h in hs]
    nn = [_mm(tinv[h], a_o[h]) for h in hs]
    levels = int(math.log2(C // RW_SUB))
    for lvl in range(levels):
        tinv = [tinv[h] + _mm(nn[h], tinv[h]) for h in hs]
        if lvl + 1 < levels:
            nn = [_mm(nn[h], nn[h]) for h in hs]

    br_s = [_mm_nt(br[h], s0[h]) for h in hs]
    rhs = [br_s[h][:C] + _mm(a_bk[h], vh[h]) for h in hs]
    u = [_mm(tinv[h], rhs[h]) for h in hs]
    uv = [jnp.concatenate([u[h], vh[h]], axis=0) for h in hs]
    y = [br_s[h][C:] + _mm(a_r[h], uv[h]) for h in hs]
    for h in hs:
        ak_end = jnp.concatenate([at_end[:, sl[h]], kt_end[:, sl[h]]], axis=0)
        s_ref[h] = s0[h] * g_end[:, sl[h]] + _mm(uv[h].T, ak_end)
    y = jnp.concatenate(y, axis=1)

    inv_n = 1.0 / N
    mean = _split_dot(y, ones_bd, 2) * inv_n
    yc = y - mean
    var = _split_dot(yc * yc, ones_bd, 2) * inv_n
    yn = yc * lax.rsqrt(var + RW_GN_EPS) * lnw_ref[...] + lnb_ref[...]
    bonus = _split_dot(r * k * rk_ref[...], ones_bd, 2) * v
    y_ref[0] = (yn + bonus) * gate


def rwkv_direction(u, col0, n_ctx, vf_u, mu, w0, w1, w2, a0, a1, a2, g1, g2, k_k, k_a, r_k,
                   ln_w, ln_b, vmix, reverse):
    B, L, _ = u.shape
    G = w0.shape[-1]
    heads = G // RW_HEAD
    C = RW_CHUNK
    nb, ncb = L // C, n_ctx // C
    has_vmix = vmix is not None
    blk = lambda c: pl.BlockSpec((1, C, G), lambda b, n: (b, _scan_block(n, ncb, nb, reverse), c))
    full = lambda arr: pl.BlockSpec(arr.shape, lambda b, n: (0,) * arr.ndim)
    row = lambda t: t.reshape(1, G)
    hid = jnp.arange(G) // RW_HEAD
    ones_bd = (hid[:, None] == hid[None, :]).astype(BF16)
    ins = [u, u, u, u]
    specs = [blk(col0), blk(col0 + 1), blk(col0 + 2), blk(col0 + 3)]
    if has_vmix:
        ins.append(vf_u)
        specs.append(blk(col0 + 2))
    params = [mu, row(w0), w1.astype(BF16), w2.astype(BF16), row(a0), a1.astype(BF16), a2.astype(BF16),
              g1.astype(BF16), g2.astype(BF16), row(k_k), row(k_a), row(r_k), row(ln_w), row(ln_b)]
    if has_vmix:
        v0, v1, v2 = vmix
        params += [row(v0), v1.astype(BF16), v2.astype(BF16)]
    params.append(ones_bd)
    ins += params
    specs += [full(p_) for p_ in params]
    return pl.pallas_call(
        functools.partial(_rwkv_kernel, heads=heads, n_ctx_blocks=ncb, reverse=reverse,
                          has_vmix=has_vmix),
        grid=(B, nb),
        in_specs=specs,
        out_specs=pl.BlockSpec((1, C, G), lambda b, n: (b, _scan_block(n, ncb, nb, reverse), 0)),
        out_shape=jax.ShapeDtypeStruct((B, L, G), F32),
        scratch_shapes=[pltpu.VMEM((heads, RW_HEAD, RW_HEAD), F32), pltpu.VMEM((4, C, G), F32)],
        compiler_params=_params("parallel", "arbitrary"),
        name="rwkv_mixer",
    )(*ins)


def _retention_kernel(*refs, heads, reverse, finish):
    it = iter(refs)
    q_ref, k_ref, v_ref = next(it), next(it), next(it)
    cos_ref, sin_ref, dm_ref, kdec_ref, qdec_ref, cdec_ref = (next(it) for _ in range(6))
    if finish:
        g_ref, o0_ref, gnw_ref = next(it), next(it), next(it)
    o_ref, s_ref = next(it), next(it)

    C, G = q_ref.shape[1], q_ref.shape[2]
    dh = G // heads
    quarter = dh // 4

    @pl.when(pl.program_id(1) == 0)
    def _():
        s_ref[...] = jnp.zeros_like(s_ref)

    lane = lax.broadcasted_iota(jnp.int32, (C, G), 1)
    first = (lane & quarter) == 0
    cos = cos_ref[...]
    sin = sin_ref[...]

    def rope(x):
        swapped = jnp.where(first, pltpu.roll(x, G - quarter, 1), pltpu.roll(x, quarter, 1))
        return x * cos + swapped * sin

    q = rope(q_ref[0])
    k = rope(k_ref[0]) * (dh ** -0.5)
    vb = v_ref[0].astype(BF16)
    qd = (q * qdec_ref[...]).astype(BF16)
    kd = (k * kdec_ref[...]).astype(BF16)
    qb = q.astype(BF16)
    kb = k.astype(BF16)
    cdec = cdec_ref[...]
    outs = []
    for h in range(heads):
        sl = slice(h * dh, (h + 1) * dh)
        s0 = s_ref[h]
        inner = lax.dot_general(qb[:, sl], kb[:, sl], NT_DIMS, preferred_element_type=F32) * dm_ref[h]
        out = jnp.dot(inner.astype(BF16), vb[:, sl], preferred_element_type=F32)
        out = out + jnp.dot(qd[:, sl], s0.astype(BF16), preferred_element_type=F32)
        kv = jnp.dot(kd[:, sl].T, vb[:, sl], preferred_element_type=F32)
        s_ref[h] = cdec[:, sl] * s0 + kv
        if finish:
            out = out + o0_ref[0, :, sl]
            m = jnp.mean(out, axis=-1, keepdims=True)
            var = jnp.mean(jnp.square(out - m), axis=-1, keepdims=True)
            out = (out - m) * lax.rsqrt(var + RT_GN_EPS)
        outs.append(out)
    o = jnp.concatenate(outs, axis=1)
    if finish:
        g = g_ref[0]
        o = o * gnw_ref[...] * (g * _sigmoid(g))
    o_ref[0] = o


def retention_direction(u, col0, n_ctx, cos_tab, sin_tab, lg, gn_w, prev_out, reverse):
    B, L, _ = u.shape
    G = gn_w.shape[-1]
    H = RT_HEADS
    dh = G // H
    C = RT_CHUNK
    nb, ncb = L // C, n_ctx // C
    finish = prev_out is not None
    idx = jnp.arange(C, dtype=F32)
    pos = (C - 1 - idx) if reverse else idx
    diff = pos[:, None] - pos[None, :]
    keep = diff > 0 if reverse else diff >= 0
    dmask = jnp.where(keep, jnp.exp(jnp.where(keep, diff, 0.0)[None] * lg[:, None, None]), 0.0)
    kdec = jnp.repeat(jnp.exp((C - 1 - pos)[:, None] * lg[None, :]), dh, axis=1)
    qdec = jnp.repeat(jnp.exp((pos + 1.0)[:, None] * lg[None, :]), dh, axis=1)
    cdec = jnp.repeat(jnp.exp(C * lg), dh)[None, :]
    tblk = lambda b, n: _scan_block(n, ncb, nb, reverse)
    blk = lambda c: pl.BlockSpec((1, C, G), lambda b, n: (b, tblk(b, n), c))
    tab = pl.BlockSpec((C, G), lambda b, n: (tblk(b, n), 0))
    ins = [u, u, u, cos_tab, sin_tab, dmask, kdec, qdec, cdec]
    specs = [blk(col0), blk(col0 + 1), blk(col0 + 2), tab, tab,
             pl.BlockSpec((H, C, C), lambda b, n: (0, 0, 0)),
             pl.BlockSpec((C, G), lambda b, n: (0, 0)),
             pl.BlockSpec((C, G), lambda b, n: (0, 0)),
             pl.BlockSpec((1, G), lambda b, n: (0, 0))]
    if finish:
        ins += [u, prev_out, gn_w.reshape(1, G)]
        specs += [blk(col0 + 3), blk(0), pl.BlockSpec((1, G), lambda b, n: (0, 0))]
    return pl.pallas_call(
        functools.partial(_retention_kernel, heads=H, reverse=reverse, finish=finish),
        grid=(B, nb),
        in_specs=specs,
        out_specs=blk(0),
        out_shape=jax.ShapeDtypeStruct((B, L, G), F32),
        scratch_shapes=[pltpu.VMEM((H, dh, dh), F32)],
        compiler_params=_params("parallel", "arbitrary"),
        name="retention_mixer",
    )(*ins)


def rope_tables(n_ctx, L, G):
    dh = G // RT_HEADS
    quarter = dh // 4
    n_rows = L // GRID_W
    rows = jnp.repeat(jnp.arange(n_rows, dtype=F32), GRID_W)
    cols = jnp.tile(jnp.arange(GRID_W, dtype=F32), n_rows)
    inv = jnp.power(ROPE_BASE, -jnp.arange(quarter, dtype=F32) / quarter)
    lane = jnp.arange(G)
    use_cols = (lane % dh) >= (dh // 2)
    pos = jnp.where(use_cols[None, :], cols[:, None], rows[:, None])
    ang = pos * inv[lane % quarter][None, :]
    sign = jnp.where((lane % (2 * quarter)) < quarter, -1.0, 1.0)[None, :]
    cos = jnp.concatenate([jnp.ones((n_ctx, G), F32), jnp.cos(ang)], axis=0)
    sin = jnp.concatenate([jnp.zeros((n_ctx, G), F32), jnp.sin(ang) * sign], axis=0)
    return cos, sin


def _moe_kernel(te_ref, nt_ref, x_ref, wgu_ref, bgu_ref, wdn_ref, bdn_ref, g_ref, o_ref):
    i = pl.program_id(0)
    F = wdn_ref.shape[1]

    @pl.when(i < nt_ref[0])
    def _():
        gu = jnp.dot(x_ref[...], wgu_ref[0], preferred_element_type=F32) + bgu_ref[0]
        glu = jnp.minimum(gu[:, :F], SWIGLU_LIMIT)
        lin = jnp.clip(gu[:, F:], -SWIGLU_LIMIT, SWIGLU_LIMIT)
        act = glu * jax.nn.sigmoid(SWIGLU_ALPHA * glu) * (lin + 1.0)
        y = jnp.dot(act.astype(BF16), wdn_ref[0], preferred_element_type=F32) + bdn_ref[0]
        o_ref[...] = y * g_ref[...]

    @pl.when(i >= nt_ref[0])
    def _():
        o_ref[...] = jnp.zeros_like(o_ref)


def moe_grouped(xs, tile_expert, n_tiles_used, w_gu, b_gu, w_dn, b_dn, row_gate):
    P, D = xs.shape
    E, _, F2 = w_gu.shape
    F = F2 // 2
    tm = MOE_TILE
    grid_spec = pltpu.PrefetchScalarGridSpec(
        num_scalar_prefetch=2,
        grid=(P // tm,),
        in_specs=[
            pl.BlockSpec((tm, D), lambda i, te, nt: (i, 0)),
            pl.BlockSpec((1, D, F2), lambda i, te, nt: (te[i], 0, 0)),
            pl.BlockSpec((1, 1, F2), lambda i, te, nt: (te[i], 0, 0)),
            pl.BlockSpec((1, F, D), lambda i, te, nt: (te[i], 0, 0)),
            pl.BlockSpec((1, 1, D), lambda i, te, nt: (te[i], 0, 0)),
            pl.BlockSpec((tm, 1), lambda i, te, nt: (i, 0)),
        ],
        out_specs=pl.BlockSpec((tm, D), lambda i, te, nt: (i, 0)),
    )
    return pl.pallas_call(
        _moe_kernel,
        grid_spec=grid_spec,
        out_shape=jax.ShapeDtypeStruct((P, D), F32),
        compiler_params=_params("arbitrary"),
        name="moe_grouped",
    )(tile_expert, n_tiles_used, xs, w_gu, b_gu.reshape(E, 1, F2), w_dn, b_dn.reshape(E, 1, D),
      row_gate)


def moe(t, router_w, router_b, w_gu_bf16, b_gu, w_dn_bf16, b_dn):
    T, D = t.shape
    E = N_EXPERTS
    tm = MOE_TILE
    logits = jnp.dot(t, router_w, precision=HIGHEST) + router_b
    vals, idx = lax.top_k(logits, TOP_K)
    probs = jax.nn.softmax(vals, axis=-1)
    onehot = jnp.sum(jax.nn.one_hot(idx, E, dtype=jnp.int32), axis=1)
    rank = jnp.cumsum(onehot, axis=0) - onehot
    sizes = jnp.sum(onehot, axis=0)
    padded = ((sizes + tm - 1) // tm) * tm
    ends = jnp.cumsum(padded)
    starts = ends - padded
    dest = starts[idx] + jnp.take_along_axis(rank, idx, axis=1)
    n_rows = T * TOP_K + E * tm
    flat_dest = dest.reshape(-1)
    tok = jnp.repeat(jnp.arange(T, dtype=jnp.int32), TOP_K)
    row_token = jnp.zeros((n_rows,), jnp.int32).at[flat_dest].set(tok)
    row_gate = jnp.zeros((n_rows,), F32).at[flat_dest].set(probs.reshape(-1))
    xs = t.astype(BF16)[row_token]
    tile_start = jnp.arange(n_rows // tm, dtype=jnp.int32) * tm
    tile_expert = jnp.minimum(jnp.searchsorted(ends, tile_start, side='right'), E - 1).astype(jnp.int32)
    n_used = (ends[-1] // tm).astype(jnp.int32).reshape(1)
    ys = moe_grouped(xs, tile_expert, n_used, w_gu_bf16, b_gu, w_dn_bf16, b_dn,
                     row_gate.reshape(n_rows, 1))
    return jnp.sum(ys[dest], axis=1)


def shift_prev(x, n):
    return jnp.pad(x, ((0, 0), (n, 0), (0, 0)))[:, : x.shape[1]]


def shift_next(x, n):
    return jnp.pad(x, ((0, 0), (0, n), (0, 0)))[:, n:]


def hyena_filter_spectrum(L, G, w1, b1, w2, b2, w3, b3, w4, freq):
    t = jnp.linspace(0.0, 1.0, L, dtype=F32)[:, None]
    bands = (HY_EMB - 1) // 2
    fr = jnp.linspace(1e-4, bands - 1, bands, dtype=F32)
    ang = (2.0 * math.pi / L) * jnp.arange(L, dtype=F32)[:, None] * fr[None, :]
    z = jnp.concatenate([t, jnp.cos(ang), -jnp.sin(ang)], axis=-1)
    h = jnp.sin(freq * (z @ w1 + b1))
    h = jnp.sin(freq * (h @ w2 + b2))
    h = jnp.sin(freq * (h @ w3 + b3))
    h = (h @ w4).reshape(L, 2, HY_ORDER, G)
    deltas = jnp.abs(jnp.linspace(HY_MIN_DECAY, HY_MAX_DECAY, G, dtype=F32))
    h = h * jnp.exp(-t * deltas)[:, None, None, :]
    taps = jnp.concatenate([h[:, 0], jnp.zeros((1, HY_ORDER, G), F32), h[:0:-1, 1]], axis=0)
    taps = taps / jnp.sum(jnp.abs(taps), axis=0, keepdims=True)
    return jnp.fft.rfft(taps, axis=0)


def hyena_mixer(u, short_w, short_b, fw1, fb1, fw2, fb2, fw3, fb3, fw4, ffreq, bias):
    L = u.shape[1]
    G = u.shape[2] // 3
    u = short_w[0] * shift_prev(u, 1) + short_w[1] * u + short_w[2] * shift_next(u, 1) + short_b
    v, x1, x2 = jnp.split(u, 3, axis=-1)
    spec = hyena_filter_spectrum(L, G, fw1, fb1, fw2, fb2, fw3, fb3, fw4, ffreq)
    z = v
    for o, gate in enumerate((x1, x2)):
        zf = jnp.fft.rfft(z, n=2 * L, axis=1)
        z = jnp.fft.irfft(zf * spec[None, :, o], n=2 * L, axis=1)[:, :L] + bias[o] * z
        z = gate * z
    return z


def rms_norm(x, w):
    y = x * lax.rsqrt(jnp.mean(x * x, axis=-1, keepdims=True) + RMS_EPS)
    return y * w


def kernel(x, c, ctx, c_ctx, ada_w, ada_b, norm1_w, norm2_w, w_in, w_out, hy_short_w, hy_short_b, hy_f_w1, hy_f_b1, hy_f_w2, hy_f_b2, hy_f_w3, hy_f_b3, hy_f_w4, hy_f_freq, hy_bias, rg_conv_w, rg_conv_b, rg_wa, rg_ba, rg_wx, rg_bx, rg_lambda, rw_mu, rw_w0, rw_w1, rw_w2, rw_a0, rw_a1, rw_a2, rw_g1, rw_g2, rw_k_k, rw_k_a, rw_r_k, rw_ln_w, rw_ln_b, rw_v0, rw_v1, rw_v2, rt_decay, rt_gn_w, moe_router_w, moe_router_b, moe_w_gu, moe_b_gu, moe_w_dn, moe_b_dn, final_norm_w):
    depth = ada_w.shape[0]
    B, L, D = x.shape
    n_ctx = ctx.shape[1]
    G = D // N_GROUPS
    cond = jnp.concatenate([jnp.broadcast_to(jax.nn.silu(c_ctx.astype(F32))[None, :], (B, D)),
                            jax.nn.silu(c.astype(F32))], axis=0)
    xs = jnp.concatenate([ctx.astype(x.dtype), x], axis=1)
    cos_tab, sin_tab = rope_tables(n_ctx, L, G)
    u_first = None

    for l in range(depth):
        last = l == depth - 1
        mod = cond @ ada_w[l].astype(F32) + ada_b[l]
        mod = jnp.stack([mod[:B], mod[B:]], axis=1)
        mods = jnp.split(mod, N_MOD, axis=-1)

        u = modulated_matmul(xs, norm1_w[l], mods[0], mods[1], w_in[l].astype(BF16), n_ctx, tn=512)
        if l == 0:
            u_first = u

        hy_p = (hy_short_w[l], hy_short_b[l], hy_f_w1[l], hy_f_b1[l], hy_f_w2[l], hy_f_b2[l],
                hy_f_w3[l], hy_f_b3[l], hy_f_w4[l], hy_f_freq[l], hy_bias[l])
        hy_l = hyena_mixer(u[:, n_ctx:, : 3 * G], *hy_p)
        if last:
            hy_c = jnp.zeros((B, n_ctx, G), F32)
        else:
            hy_c = hyena_mixer(u[:, :n_ctx, : 3 * G], *hy_p)
        hy = jnp.concatenate([hy_c, hy_l], axis=1)

        rg = [rglru_direction(u, 3, n_ctx, rg_conv_w[l, d], rg_conv_b[l, d], rg_wa[l, d], rg_ba[l, d],
                              rg_wx[l, d], rg_bx[l, d], rg_lambda[l, d], reverse=(d == 1))
              for d in range(2)]
        vmix = None if l == 0 else (rw_v0[l - 1], rw_v1[l - 1], rw_v2[l - 1])
        rw = [rwkv_direction(u, 5, n_ctx, u_first, rw_mu[l, d], rw_w0[l, d], rw_w1[l, d], rw_w2[l, d],
                             rw_a0[l, d], rw_a1[l, d], rw_a2[l, d], rw_g1[l], rw_g2[l], rw_k_k[l],
                             rw_k_a[l], rw_r_k[l], rw_ln_w[l], rw_ln_b[l], vmix, reverse=(d == 1))
              for d in range(2)]
        lg = -jax.nn.softplus(rt_decay[l].astype(F32))
        rt0 = retention_direction(u, 9, n_ctx, cos_tab, sin_tab, lg[0], rt_gn_w[l], None, reverse=False)
        rt = retention_direction(u, 9, n_ctx, cos_tab, sin_tab, lg[1], rt_gn_w[l], rt0, reverse=True)

        xs = gated_out_proj((hy, rg[0], rg[1], rw[0], rw[1], rt), w_out[l].astype(BF16), xs, mods[2], n_ctx)

        moe_p = (moe_router_w[l], moe_router_b[l], moe_w_gu[l].astype(BF16), moe_b_gu[l],
                 moe_w_dn[l].astype(BF16), moe_b_dn[l])
        seg = jnp.arange(xs.shape[1]) >= n_ctx
        pick = lambda m: jnp.where(seg[None, :, None], m[:, 1:2, :], m[:, 0:1, :])
        if last:
            xl = xs[:, n_ctx:]
            hl = rms_norm(xl, norm2_w[l]) * (1 + mods[4][:, 1:2, :]) + mods[3][:, 1:2, :]
            f = moe(hl.reshape(-1, D), *moe_p).reshape(B, L, D)
            return rms_norm(xl + mods[5][:, 1:2, :] * f, final_norm_w)
        h = rms_norm(xs, norm2_w[l]) * (1 + pick(mods[4])) + pick(mods[3])
        f = moe(h.reshape(-1, D), *moe_p).reshape(xs.shape)
        xs = xs + pick(mods[5]) * f
```

```python
import functools
import math

import jax
import jax.numpy as jnp
from jax import lax
from jax.experimental import pallas as pl
from jax.experimental.pallas import tpu as pltpu

F32 = jnp.float32
BF16 = jnp.bfloat16
HIGHEST = lax.Precision.HIGHEST

GRID_W = 64
N_GROUPS = 4
N_MOD = 6
RMS_EPS = 1e-6

HY_ORDER = 2
HY_EMB = 33
HY_TARGET = 1e-2
HY_MIN_DECAY = math.log(HY_TARGET) / 1.5
HY_MAX_DECAY = math.log(HY_TARGET) / 0.3

RG_HEADS = 8
RG_CONV = 4
RG_C = 8.0

RW_HEAD = 64
RW_GN_EPS = 64e-5
RW_CHUNK = 64
RW_SUB = 16

RT_HEADS = 4
RT_CHUNK = 128
RT_GN_EPS = 1e-6
ROPE_BASE = 10000.0

N_EXPERTS = 32
TOP_K = 4
SWIGLU_LIMIT = 7.0
SWIGLU_ALPHA = 1.702

VMEM_LIMIT_BYTES = 52 * 1024 * 1024
LRU_CHUNK = 256
MOE_TILE = 256
ROW_TILE = 256

NT_DIMS = (((1,), (1,)), ((), ()))


def _params(*sem):
    return pltpu.CompilerParams(dimension_semantics=sem, vmem_limit_bytes=VMEM_LIMIT_BYTES)


def _scan_block(n, n_ctx_blocks, n_blocks, reverse):
    if not reverse:
        return n
    return jnp.where(n < n_ctx_blocks, n_ctx_blocks - 1 - n, n_blocks - 1 + n_ctx_blocks - n)


def _shift_rows(cur, prv, j, reverse):
    T = cur.shape[0]
    row = lax.broadcasted_iota(jnp.int32, cur.shape, 0)
    if not reverse:
        return jnp.where(row < j, pltpu.roll(prv, j, 0), pltpu.roll(cur, j, 0))
    return jnp.where(row >= T - j, pltpu.roll(prv, T - j, 0), pltpu.roll(cur, T - j, 0))


def _mm(a, b):
    return jnp.dot(a.astype(BF16), b.astype(BF16), preferred_element_type=F32)


def _mm_nt(a, b):
    return lax.dot_general(a.astype(BF16), b.astype(BF16), NT_DIMS, preferred_element_type=F32)


def _split_dot(x, m_bf16, parts):
    acc = None
    for _ in range(parts):
        hi = x.astype(BF16)
        d = jnp.dot(hi, m_bf16, preferred_element_type=F32)
        acc = d if acc is None else acc + d
        x = x - hi.astype(F32)
    return acc


def _expm1(x):
    small = x * (1.0 + x * (0.5 + x * (1.0 / 6.0 + x * (1.0 / 24.0 + x * (1.0 / 120.0)))))
    return jnp.where(jnp.abs(x) < 0.1, small, jnp.exp(x) - 1.0)


def _softplus(x):
    return jnp.maximum(x, 0.0) + jnp.log(1.0 + jnp.exp(-jnp.abs(x)))


def _sigmoid(x):
    return 1.0 / (1.0 + jnp.exp(-x))


def _modmm_kernel(x_ref, nw_ref, sh_ref, sc_ref, w_ref, o_ref):
    x = x_ref[0]
    y = x * lax.rsqrt(jnp.mean(x * x, axis=-1, keepdims=True) + RMS_EPS)
    y = y * nw_ref[...] * (1.0 + sc_ref[0]) + sh_ref[0]
    o_ref[0] = jnp.dot(y.astype(BF16), w_ref[...], preferred_element_type=F32)


def modulated_matmul(x, nw, shift, scale, w_bf16, n_ctx, tn):
    B, L, D = x.shape
    N = w_bf16.shape[1]
    tm = ROW_TILE
    ncb = n_ctx // tm
    mod_idx = lambda j, b, i: (2 * b + (i >= ncb).astype(jnp.int32), 0, 0)
    return pl.pallas_call(
        _modmm_kernel,
        grid=(N // tn, B, L // tm),
        in_specs=[
            pl.BlockSpec((1, tm, D), lambda j, b, i: (b, i, 0)),
            pl.BlockSpec((1, D), lambda j, b, i: (0, 0)),
            pl.BlockSpec((1, 1, D), mod_idx),
            pl.BlockSpec((1, 1, D), mod_idx),
            pl.BlockSpec((D, tn), lambda j, b, i: (0, j)),
        ],
        out_specs=pl.BlockSpec((1, tm, tn), lambda j, b, i: (b, i, j)),
        out_shape=jax.ShapeDtypeStruct((B, L, N), F32),
        compiler_params=_params("parallel", "parallel", "parallel"),
        name="modulated_matmul",
    )(x, nw.reshape(1, D), shift.reshape(2 * B, 1, D), scale.reshape(2 * B, 1, D), w_bf16)


def _outproj_kernel(hy_ref, rg0_ref, rg1_ref, rw0_ref, rw1_ref, rt_ref, w_ref, res_ref, g_ref, o_ref):
    G = hy_ref.shape[2]
    slabs = (hy_ref[0], rg0_ref[0] + rg1_ref[0], rw0_ref[0] + rw1_ref[0], rt_ref[0])
    acc = None
    for i, s in enumerate(slabs):
        d = jnp.dot(s.astype(BF16), w_ref[i * G:(i + 1) * G, :], preferred_element_type=F32)
        acc = d if acc is None else acc + d
    o_ref[0] = res_ref[0] + g_ref[0] * acc


def gated_out_proj(slabs, w_bf16, res, gate, n_ctx):
    B, L, D = res.shape
    G = slabs[0].shape[2]
    tm = ROW_TILE
    ncb = n_ctx // tm
    g_idx = lambda b, i: (2 * b + (i >= ncb).astype(jnp.int32), 0, 0)
    slab_spec = pl.BlockSpec((1, tm, G), lambda b, i: (b, i, 0))
    return pl.pallas_call(
        _outproj_kernel,
        grid=(B, L // tm),
        in_specs=[slab_spec] * 6 + [
            pl.BlockSpec((N_GROUPS * G, D), lambda b, i: (0, 0)),
            pl.BlockSpec((1, tm, D), lambda b, i: (b, i, 0)),
            pl.BlockSpec((1, 1, D), g_idx),
        ],
        out_specs=pl.BlockSpec((1, tm, D), lambda b, i: (b, i, 0)),
        out_shape=jax.ShapeDtypeStruct((B, L, D), F32),
        compiler_params=_params("parallel", "parallel"),
        name="gated_out_proj",
    )(*slabs, w_bf16, res, gate.reshape(2 * B, 1, D))


def _rglru_kernel(x_ref, gate_ref, cw_ref, cb_ref, wa_ref, ba_ref, wx_ref, bx_ref, lam_ref,
                  o_ref, prev_ref, carry_ref, *, n_ctx_blocks, reverse):
    T = x_ref.shape[1]
    n = pl.program_id(1)

    @pl.when(n == 0)
    def _():
        carry_ref[...] = jnp.zeros_like(carry_ref)

    @pl.when((n == 0) | (n == n_ctx_blocks))
    def _():
        prev_ref[...] = jnp.zeros_like(prev_ref)

    x = x_ref[0]
    prv = prev_ref[...]
    xc = cb_ref[...] + cw_ref[RG_CONV - 1:RG_CONV, :] * x
    for j in range(1, RG_CONV):
        xc = xc + cw_ref[RG_CONV - 1 - j:RG_CONV - j, :] * _shift_rows(x, prv, j, reverse)
    prev_ref[...] = x

    xb = xc.astype(BF16)
    r = _sigmoid(jnp.dot(xb, wa_ref[...], preferred_element_type=F32) + ba_ref[...])
    i = _sigmoid(jnp.dot(xb, wx_ref[...], preferred_element_type=F32) + bx_ref[...])
    log_a = -RG_C * r * _softplus(-lam_ref[...])
    a = jnp.exp(log_a)
    b = jnp.sqrt(-_expm1(2.0 * log_a)) * (i * xc)

    row = lax.broadcasted_iota(jnp.int32, a.shape, 0)
    s = 1
    while s < T:
        if not reverse:
            m = row >= s
            b = jnp.where(m, a * pltpu.roll(b, s, 0) + b, b)
            a = jnp.where(m, a * pltpu.roll(a, s, 0), a)
        else:
            m = row < T - s
            b = jnp.where(m, a * pltpu.roll(b, T - s, 0) + b, b)
            a = jnp.where(m, a * pltpu.roll(a, T - s, 0), a)
        s *= 2
    h = b + a * carry_ref[...]
    carry_ref[...] = h[0:1, :] if reverse else h[T - 1:T, :]
    o_ref[0] = h * jax.nn.gelu(gate_ref[0])


def _block_diag(w):
    H, di, dj = w.shape
    eye = jnp.eye(H, dtype=w.dtype)
    return (eye[:, None, :, None] * w[:, :, None, :]).reshape(H * di, H * dj)


def rglru_direction(u, col0, n_ctx, conv_w, conv_b, wa, ba, wx, bx, lam, reverse):
    B, L, _ = u.shape
    G = conv_w.shape[-1]
    T = LRU_CHUNK
    nb, ncb = L // T, n_ctx // T
    blk = lambda c: pl.BlockSpec((1, T, G), lambda b, n: (b, _scan_block(n, ncb, nb, reverse), c))
    row = pl.BlockSpec((1, G), lambda b, n: (0, 0))
    mat = pl.BlockSpec((G, G), lambda b, n: (0, 0))
    return pl.pallas_call(
        functools.partial(_rglru_kernel, n_ctx_blocks=ncb, reverse=reverse),
        grid=(B, nb),
        in_specs=[blk(col0), blk(col0 + 1), pl.BlockSpec((RG_CONV, G), lambda b, n: (0, 0)),
                  row, mat, row, mat, row, row],
        out_specs=pl.BlockSpec((1, T, G), lambda b, n: (b, _scan_block(n, ncb, nb, reverse), 0)),
        out_shape=jax.ShapeDtypeStruct((B, L, G), F32),
        scratch_shapes=[pltpu.VMEM((T, G), F32), pltpu.VMEM((1, G), F32)],
        compiler_params=_params("parallel", "arbitrary"),
        name="rglru_mixer",
    )(u, u, conv_w, conv_b.reshape(1, G), _block_diag(wa).astype(BF16), ba.reshape(1, G),
      _block_diag(wx).astype(BF16), bx.reshape(1, G), lam.reshape(1, G))


def _rwkv_kernel(*refs, heads, n_ctx_blocks, reverse, has_vmix):
    it = iter(refs)
    r_ref, k_ref, v_ref, z_ref = next(it), next(it), next(it), next(it)
    vf_ref = next(it) if has_vmix else None
    mu_ref, w0_ref, w1_ref, w2_ref, a0_ref, a1_ref, a2_ref = (next(it) for _ in range(7))
    g1_ref, g2_ref, kk_ref, ka_ref, rk_ref, lnw_ref, lnb_ref = (next(it) for _ in range(7))
    if has_vmix:
        v0_ref, v1_ref, v2_ref = next(it), next(it), next(it)
    ones_ref = next(it)
    y_ref, s_ref, prev_ref = next(it), next(it), next(it)

    C = r_ref.shape[1]
    N = RW_HEAD
    n = pl.program_id(1)

    @pl.when(n == 0)
    def _():
        s_ref[...] = jnp.zeros_like(s_ref)

    @pl.when((n == 0) | (n == n_ctx_blocks))
    def _():
        prev_ref[...] = jnp.zeros_like(prev_ref)

    raw = [r_ref[0], k_ref[0], v_ref[0], z_ref[0]]
    z_raw = raw[3]
    if has_vmix:
        vm = _sigmoid(v0_ref[...] + _mm(_mm(z_raw, v1_ref[...]), v2_ref[...]))
        raw[2] = raw[2] + (vf_ref[0] - raw[2]) * vm
    gate = _mm(_sigmoid(_mm(z_raw, g1_ref[...])), g2_ref[...])
    mixed = []
    for j in range(4):
        prv = prev_ref[j]
        mixed.append(raw[j] + (_shift_rows(raw[j], prv, 1, reverse) - raw[j]) * mu_ref[j:j + 1, :])
        prev_ref[j] = raw[j]
    r, k, v, z = mixed
    w_log = -_softplus(-(w0_ref[...] + _mm(jnp.tanh(_mm(z, w1_ref[...])), w2_ref[...]))) - 0.5
    lw = -jnp.exp(w_log)
    a = _sigmoid(a0_ref[...] + _mm(_mm(z, a1_ref[...]), a2_ref[...]))
    ones_bd = ones_ref[...]
    kk = k * kk_ref[...]
    kk = kk / jnp.maximum(jnp.sqrt(_split_dot(kk * kk, ones_bd, 2)), 1e-12)
    k = k * (1.0 + (a - 1.0) * ka_ref[...])
    kka = kk * a

    ii = lax.broadcasted_iota(jnp.int32, (C, C), 0)
    jj = lax.broadcasted_iota(jnp.int32, (C, C), 1)
    incl = (ii <= jj) if reverse else (ii >= jj)
    strict = (ii < jj) if reverse else (ii > jj)
    same_sub = (ii & -RW_SUB) == (jj & -RW_SUB)
    eye = (ii == jj).astype(F32)

    cum = None
    lw_part = lw
    for _ in range(3):
        hi = lw_part.astype(BF16)
        d = jnp.dot(incl.astype(BF16), hi, preferred_element_type=F32)
        cum = d if cum is None else cum + d
        lw_part = lw_part - hi.astype(F32)
    cum_end = cum[0:1, :] if reverse else cum[C - 1:C, :]
    e_neg = jnp.exp(-cum)
    e_end = jnp.exp(cum_end - cum)
    bt = kk * jnp.exp(cum - lw)
    at = -kka * e_neg
    kt = k * e_neg
    rt = r * jnp.exp(cum)
    at_end = -kka * e_end
    kt_end = k * e_end
    g_end = jnp.exp(cum_end)

    hs = range(heads)
    sl = [slice(h * N, (h + 1) * N) for h in hs]
    s0 = [s_ref[h] for h in hs]
    vh = [v[:, sl[h]] for h in hs]
    br = [jnp.concatenate([bt[:, sl[h]], rt[:, sl[h]]], axis=0) for h in hs]
    ak = [jnp.concatenate([at[:, sl[h]], kt[:, sl[h]]], axis=0) for h in hs]
    a_all = [_mm_nt(br[h], ak[h]) for h in hs]
    a_ab = [jnp.where(strict, a_all[h][:C, :C], 0.0) for h in hs]
    a_bk = [jnp.where(strict, a_all[h][:C, C:], 0.0) for h in hs]
    a_r = [jnp.concatenate([jnp.where(incl, a_all[h][C:, :C], 0.0),
                            jnp.where(incl, a_all[h][C:, C:], 0.0)], axis=1) for h in hs]
    a_d = [jnp.where(same_sub, a_ab[h], 0.0) for h in hs]
    a_o = [jnp.where(same_sub, 0.0, a_ab[h]) for h in hs]
    tinv = [eye + a_d[h] for h in hs]
    p = a_d
    for _ in range(int(math.log2(RW_SUB)) - 1):
        p = [_mm(p[h], p[h]) for h in hs]
        tinv = [tinv[h] + _mm(tinv[h], p[h]) for h in hs]
    nn = [_mm(tinv[h], a_o[h]) for h in hs]
    levels = int(math.log2(C // RW_SUB))
    for lvl in range(levels):
        tinv = [tinv[h] + _mm(nn[h], tinv[h]) for h in hs]
        if lvl + 1 < levels:
            nn = [_mm(nn[h], nn[h]) for h in hs]

    br_s = [_mm_nt(br[h], s0[h]) for h in hs]
    rhs = [br_s[h][:C] + _mm(a_bk[h], vh[h]) for h in hs]
    u = [_mm(tinv[h], rhs[h]) for h in hs]
    uv = [jnp.concatenate([u[h], vh[h]], axis=0) for h in hs]
    y = [br_s[h][C:] + _mm(a_r[h], uv[h]) for h in hs]
    for h in hs:
        ak_end = jnp.concatenate([at_end[:, sl[h]], kt_end[:, sl[h]]], axis=0)
        s_ref[h] = s0[h] * g_end[:, sl[h]] + _mm(uv[h].T, ak_end)
    y = jnp.concatenate(y, axis=1)

    inv_n = 1.0 / N
    mean = _split_dot(y, ones_bd, 2) * inv_n
    yc = y - mean
    var = _split_dot(yc * yc, ones_bd, 2) * inv_n
    yn = yc * lax.rsqrt(var + RW_GN_EPS) * lnw_ref[...] + lnb_ref[...]
    bonus = _split_dot(r * k * rk_ref[...], ones_bd, 2) * v
    y_ref[0] = (yn + bonus) * gate


def rwkv_direction(u, col0, n_ctx, vf_u, mu, w0, w1, w2, a0, a1, a2, g1, g2, k_k, k_a, r_k,
                   ln_w, ln_b, vmix, reverse):
    B, L, _ = u.shape
    G = w0.shape[-1]
    heads = G // RW_HEAD
    C = RW_CHUNK
    nb, ncb = L // C, n_ctx // C
    has_vmix = vmix is not None
    blk = lambda c: pl.BlockSpec((1, C, G), lambda b, n: (b, _scan_block(n, ncb, nb, reverse), c))
    full = lambda arr: pl.BlockSpec(arr.shape, lambda b, n: (0,) * arr.ndim)
    row = lambda t: t.reshape(1, G)
    hid = jnp.arange(G) // RW_HEAD
    ones_bd = (hid[:, None] == hid[None, :]).astype(BF16)
    ins = [u, u, u, u]
    specs = [blk(col0), blk(col0 + 1), blk(col0 + 2), blk(col0 + 3)]
    if has_vmix:
        ins.append(vf_u)
        specs.append(blk(col0 + 2))
    params = [mu, row(w0), w1.astype(BF16), w2.astype(BF16), row(a0), a1.astype(BF16), a2.astype(BF16),
              g1.astype(BF16), g2.astype(BF16), row(k_k), row(k_a), row(r_k), row(ln_w), row(ln_b)]
    if has_vmix:
        v0, v1, v2 = vmix
        params += [row(v0), v1.astype(BF16), v2.astype(BF16)]
    params.append(ones_bd)
    ins += params
    specs += [full(p_) for p_ in params]
    return pl.pallas_call(
        functools.partial(_rwkv_kernel, heads=heads, n_ctx_blocks=ncb, reverse=reverse,
                          has_vmix=has_vmix),
        grid=(B, nb),
        in_specs=specs,
        out_specs=pl.BlockSpec((1, C, G), lambda b, n: (b, _scan_block(n, ncb, nb, reverse), 0)),
        out_shape=jax.ShapeDtypeStruct((B, L, G), F32),
        scratch_shapes=[pltpu.VMEM((heads, RW_HEAD, RW_HEAD), F32), pltpu.VMEM((4, C, G), F32)],
        compiler_params=_params("parallel", "arbitrary"),
        name="rwkv_mixer",
    )(*ins)


def _retention_kernel(*refs, heads, reverse, finish):
    it = iter(refs)
    q_ref, k_ref, v_ref = next(it), next(it), next(it)
    cos_ref, sin_ref, dm_ref, kdec_ref, qdec_ref, cdec_ref = (next(it) for _ in range(6))
    if finish:
        g_ref, o0_ref, gnw_ref = next(it), next(it), next(it)
    o_ref, s_ref = next(it), next(it)

    C, G = q_ref.shape[1], q_ref.shape[2]
    dh = G // heads
    quarter = dh // 4

    @pl.when(pl.program_id(1) == 0)
    def _():
        s_ref[...] = jnp.zeros_like(s_ref)

    lane = lax.broadcasted_iota(jnp.int32, (C, G), 1)
    first = (lane & quarter) == 0
    cos = cos_ref[...]
    sin = sin_ref[...]

    def rope(x):
        swapped = jnp.where(first, pltpu.roll(x, G - quarter, 1), pltpu.roll(x, quarter, 1))
        return x * cos + swapped * sin

    q = rope(q_ref[0])
    k = rope(k_ref[0]) * (dh ** -0.5)
    vb = v_ref[0].astype(BF16)
    qd = (q * qdec_ref[...]).astype(BF16)
    kd = (k * kdec_ref[...]).astype(BF16)
    qb = q.astype(BF16)
    kb = k.astype(BF16)
    cdec = cdec_ref[...]
    outs = []
    for h in range(heads):
        sl = slice(h * dh, (h + 1) * dh)
        s0 = s_ref[h]
        inner = lax.dot_general(qb[:, sl], kb[:, sl], NT_DIMS, preferred_element_type=F32) * dm_ref[h]
        out = jnp.dot(inner.astype(BF16), vb[:, sl], preferred_element_type=F32)
        out = out + jnp.dot(qd[:, sl], s0.astype(BF16), preferred_element_type=F32)
        kv = jnp.dot(kd[:, sl].T, vb[:, sl], preferred_element_type=F32)
        s_ref[h] = cdec[:, sl] * s0 + kv
        if finish:
            out = out + o0_ref[0, :, sl]
            m = jnp.mean(out, axis=-1, keepdims=True)
            var = jnp.mean(jnp.square(out - m), axis=-1, keepdims=True)
            out = (out - m) * lax.rsqrt(var + RT_GN_EPS)
        outs.append(out)
    o = jnp.concatenate(outs, axis=1)
    if finish:
        g = g_ref[0]
        o = o * gnw_ref[...] * (g * _sigmoid(g))
    o_ref[0] = o


def retention_direction(u, col0, n_ctx, cos_tab, sin_tab, lg, gn_w, prev_out, reverse):
    B, L, _ = u.shape
    G = gn_w.shape[-1]
    H = RT_HEADS
    dh = G // H
    C = RT_CHUNK
    nb, ncb = L // C, n_ctx // C
    finish = prev_out is not None
    idx = jnp.arange(C, dtype=F32)
    pos = (C - 1 - idx) if reverse else idx
    diff = pos[:, None] - pos[None, :]
    keep = diff > 0 if reverse else diff >= 0
    dmask = jnp.where(keep, jnp.exp(jnp.where(keep, diff, 0.0)[None] * lg[:, None, None]), 0.0)
    kdec = jnp.repeat(jnp.exp((C - 1 - pos)[:, None] * lg[None, :]), dh, axis=1)
    qdec = jnp.repeat(jnp.exp((pos + 1.0)[:, None] * lg[None, :]), dh, axis=1)
    cdec = jnp.repeat(jnp.exp(C * lg), dh)[None, :]
    tblk = lambda b, n: _scan_block(n, ncb, nb, reverse)
    blk = lambda c: pl.BlockSpec((1, C, G), lambda b, n: (b, tblk(b, n), c))
    tab = pl.BlockSpec((C, G), lambda b, n: (tblk(b, n), 0))
    ins = [u, u, u, cos_tab, sin_tab, dmask, kdec, qdec, cdec]
    specs = [blk(col0), blk(col0 + 1), blk(col0 + 2), tab, tab,
             pl.BlockSpec((H, C, C), lambda b, n: (0, 0, 0)),
             pl.BlockSpec((C, G), lambda b, n: (0, 0)),
             pl.BlockSpec((C, G), lambda b, n: (0, 0)),
             pl.BlockSpec((1, G), lambda b, n: (0, 0))]
    if finish:
        ins += [u, prev_out, gn_w.reshape(1, G)]
        specs += [blk(col0 + 3), blk(0), pl.BlockSpec((1, G), lambda b, n: (0, 0))]
    return pl.pallas_call(
        functools.partial(_retention_kernel, heads=H, reverse=reverse, finish=finish),
        grid=(B, nb),
        in_specs=specs,
        out_specs=blk(0),
        out_shape=jax.ShapeDtypeStruct((B, L, G), F32),
        scratch_shapes=[pltpu.VMEM((H, dh, dh), F32)],
        compiler_params=_params("parallel", "arbitrary"),
        name="retention_mixer",
    )(*ins)


def rope_tables(n_ctx, L, G):
    dh = G // RT_HEADS
    quarter = dh // 4
    n_rows = L // GRID_W
    rows = jnp.repeat(jnp.arange(n_rows, dtype=F32), GRID_W)
    cols = jnp.tile(jnp.arange(GRID_W, dtype=F32), n_rows)
    inv = jnp.power(ROPE_BASE, -jnp.arange(quarter, dtype=F32) / quarter)
    lane = jnp.arange(G)
    use_cols = (lane % dh) >= (dh // 2)
    pos = jnp.where(use_cols[None, :], cols[:, None], rows[:, None])
    ang = pos * inv[lane % quarter][None, :]
    sign = jnp.where((lane % (2 * quarter)) < quarter, -1.0, 1.0)[None, :]
    cos = jnp.concatenate([jnp.ones((n_ctx, G), F32), jnp.cos(ang)], axis=0)
    sin = jnp.concatenate([jnp.zeros((n_ctx, G), F32), jnp.sin(ang) * sign], axis=0)
    return cos, sin


def _moe_kernel(te_ref, nt_ref, x_ref, wgu_ref, bgu_ref, wdn_ref, bdn_ref, g_ref, o_ref):
    i = pl.program_id(0)
    F = wdn_ref.shape[1]

    @pl.when(i < nt_ref[0])
    def _():
        gu = jnp.dot(x_ref[...], wgu_ref[0], preferred_element_type=F32) + bgu_ref[0]
        glu = jnp.minimum(gu[:, :F], SWIGLU_LIMIT)
        lin = jnp.clip(gu[:, F:], -SWIGLU_LIMIT, SWIGLU_LIMIT)
        act = glu * jax.nn.sigmoid(SWIGLU_ALPHA * glu) * (lin + 1.0)
        y = jnp.dot(act.astype(BF16), wdn_ref[0], preferred_element_type=F32) + bdn_ref[0]
        o_ref[...] = y * g_ref[...]

    @pl.when(i >= nt_ref[0])
    def _():
        o_ref[...] = jnp.zeros_like(o_ref)


def moe_grouped(xs, tile_expert, n_tiles_used, w_gu, b_gu, w_dn, b_dn, row_gate):
    P, D = xs.shape
    E, _, F2 = w_gu.shape
    F = F2 // 2
    tm = MOE_TILE
    grid_spec = pltpu.PrefetchScalarGridSpec(
        num_scalar_prefetch=2,
        grid=(P // tm,),
        in_specs=[
            pl.BlockSpec((tm, D), lambda i, te, nt: (i, 0)),
            pl.BlockSpec((1, D, F2), lambda i, te, nt: (te[i], 0, 0)),
            pl.BlockSpec((1, 1, F2), lambda i, te, nt: (te[i], 0, 0)),
            pl.BlockSpec((1, F, D), lambda i, te, nt: (te[i], 0, 0)),
            pl.BlockSpec((1, 1, D), lambda i, te, nt: (te[i], 0, 0)),
            pl.BlockSpec((tm, 1), lambda i, te, nt: (i, 0)),
        ],
        out_specs=pl.BlockSpec((tm, D), lambda i, te, nt: (i, 0)),
    )
    return pl.pallas_call(
        _moe_kernel,
        grid_spec=grid_spec,
        out_shape=jax.ShapeDtypeStruct((P, D), F32),
        compiler_params=_params("arbitrary"),
        name="moe_grouped",
    )(tile_expert, n_tiles_used, xs, w_gu, b_gu.reshape(E, 1, F2), w_dn, b_dn.reshape(E, 1, D),
      row_gate)


def moe(t, router_w, router_b, w_gu_bf16, b_gu, w_dn_bf16, b_dn):
    T, D = t.shape
    E = N_EXPERTS
    tm = MOE_TILE
    logits = jnp.dot(t, router_w, precision=HIGHEST) + router_b
    vals, idx = lax.top_k(logits, TOP_K)
    probs = jax.nn.softmax(vals, axis=-1)
    onehot = jnp.sum(jax.nn.one_hot(idx, E, dtype=jnp.int32), axis=1)
    rank = jnp.cumsum(onehot, axis=0) - onehot
    sizes = jnp.sum(onehot, axis=0)
    padded = ((sizes + tm - 1) // tm) * tm
    ends = jnp.cumsum(padded)
    starts = ends - padded
    dest = starts[idx] + jnp.take_along_axis(rank, idx, axis=1)
    n_rows = T * TOP_K + E * tm
    tile_start = jnp.arange(n_rows // tm, dtype=jnp.int32) * tm
    tile_expert = jnp.minimum(jnp.searchsorted(ends, tile_start, side='right'), E - 1).astype(jnp.int32)
    order = jnp.argsort(idx.reshape(-1), stable=True).astype(jnp.int32)
    row_expert = jnp.repeat(tile_expert, tm)
    within = jnp.arange(n_rows, dtype=jnp.int32) - starts[row_expert]
    valid = within < sizes[row_expert]
    src = order[jnp.clip((jnp.cumsum(sizes) - sizes)[row_expert] + within, 0, T * TOP_K - 1)]
    row_token = jnp.where(valid, src // TOP_K, 0)
    row_gate = jnp.where(valid, probs.reshape(-1)[src], 0.0)
    xs = t.astype(BF16)[row_token]
    n_used = (ends[-1] // tm).astype(jnp.int32).reshape(1)
    ys = moe_grouped(xs, tile_expert, n_used, w_gu_bf16, b_gu, w_dn_bf16, b_dn,
                     row_gate.reshape(n_rows, 1))
    return jnp.sum(ys[dest], axis=1)


HY_COLS = 256
HY_ROWS = 512


def _twiddle_kernel(c1_ref, s1_ref, c2_ref, s2_ref, cs_ref, ss_ref):
    c1, s1 = c1_ref[0], s1_ref[0]
    c2, s2 = c2_ref[...], s2_ref[...]
    cs_ref[...] = (c1 * c2 - s1 * s2).astype(BF16)
    ss_ref[...] = (s1 * c2 + c1 * s2).astype(BF16)


def dft_matrices(L):
    R = min(64, L)
    n_hi = L // R
    period = 8 * L
    a = 2 * jnp.arange(L, dtype=jnp.int32) + 1
    m1 = (a[None, :] * (2 * R * jnp.arange(n_hi, dtype=jnp.int32))[:, None]) % period
    m2 = (a[None, :] * (2 * jnp.arange(R, dtype=jnp.int32) + 1)[:, None]) % period
    ang = lambda m: (m.astype(F32) - jnp.where(m >= period // 2, period, 0).astype(F32)) * (math.pi / (4 * L))
    c1, s1 = jnp.cos(ang(m1)).reshape(n_hi, 1, L), jnp.sin(ang(m1)).reshape(n_hi, 1, L)
    c2, s2 = jnp.cos(ang(m2)), jnp.sin(ang(m2))
    hi = pl.BlockSpec((1, 1, L), lambda i: (i, 0, 0))
    lo = pl.BlockSpec((R, L), lambda i: (0, 0))
    out = pl.BlockSpec((R, L), lambda i: (i, 0))
    return pl.pallas_call(
        _twiddle_kernel,
        grid=(n_hi,),
        in_specs=[hi, hi, lo, lo],
        out_specs=[out, out],
        out_shape=[jax.ShapeDtypeStruct((L, L), BF16)] * 2,
        compiler_params=_params("parallel"),
        name="dft_matrices",
    )(c1, s1, c2, s2)


def _dft_pair_kernel(cs_ref, ss_ref, x_ref, oc_ref, os_ref):
    x = x_ref[...]
    oc_ref[...] = jnp.dot(cs_ref[...], x, preferred_element_type=F32)
    os_ref[...] = jnp.dot(ss_ref[...], x, preferred_element_type=F32)


def dft_pair(cs, ss, x_bf16):
    L, M = x_bf16.shape
    tm = min(HY_ROWS, L)
    tn = min(512, M)
    mat = pl.BlockSpec((tm, L), lambda j, i: (i, 0))
    out = pl.BlockSpec((tm, tn), lambda j, i: (i, j))
    return pl.pallas_call(
        _dft_pair_kernel,
        grid=(M // tn, L // tm),
        in_specs=[mat, mat, pl.BlockSpec((L, tn), lambda j, i: (0, j))],
        out_specs=[out, out],
        out_shape=[jax.ShapeDtypeStruct((L, M), F32)] * 2,
        compiler_params=_params("parallel", "arbitrary"),
        name="dft_pair",
    )(cs, ss, x_bf16)


def _hyena_conv_kernel(z_ref, gate_ref, cs_ref, ss_ref, hre_ref, him_ref, bias_ref, o_ref,
                       zb_ref, yre_ref, yim_ref):
    phase = pl.program_id(1)
    m = pl.program_id(2)
    tm = cs_ref.shape[0]
    L = z_ref.shape[1]
    rows = pl.ds(pl.multiple_of(m * tm, tm), tm)

    @pl.when((phase == 0) & (m == 0))
    def _():
        zb_ref[...] = z_ref[0].astype(BF16)

    @pl.when(phase == 0)
    def _():
        xc = jnp.dot(cs_ref[...], zb_ref[...], preferred_element_type=F32)
        xs = jnp.dot(ss_ref[...], zb_ref[...], preferred_element_type=F32)
        hre, him = hre_ref[...], him_ref[...]
        yre_ref[rows, :] = (xc * hre + xs * him).astype(BF16)
        yim_ref[rows, :] = (xc * him - xs * hre).astype(BF16)

    @pl.when(phase == 1)
    def _():
        y = (jnp.dot(cs_ref[...], yre_ref[...], preferred_element_type=F32)
             - jnp.dot(ss_ref[...], yim_ref[...], preferred_element_type=F32)) * (1.0 / L)
        o_ref[0] = gate_ref[0] * (y + bias_ref[...] * z_ref[0, rows, :])


def hyena_conv(z_src, z_col, gate_src, gate_col, cs, ss, hre, him, bias):
    B, L, _ = z_src.shape
    G = bias.shape[0]
    tc = HY_COLS
    tm = min(HY_ROWS, L)
    ncb = G // tc
    nm = L // tm
    pin = lambda phase, m, keep: jnp.where(phase == keep, m, (nm - 1) * (1 - keep))
    return pl.pallas_call(
        _hyena_conv_kernel,
        grid=(B * ncb, 2, nm),
        in_specs=[
            pl.BlockSpec((1, L, tc), lambda i, p, m: (i // ncb, 0, z_col * ncb + i % ncb)),
            pl.BlockSpec((1, tm, tc), lambda i, p, m: (i // ncb, pin(p, m, 1), gate_col * ncb + i % ncb)),
            pl.BlockSpec((tm, L), lambda i, p, m: (m, 0)),
            pl.BlockSpec((tm, L), lambda i, p, m: (m, 0)),
            pl.BlockSpec((tm, tc), lambda i, p, m: (pin(p, m, 0), i % ncb)),
            pl.BlockSpec((tm, tc), lambda i, p, m: (pin(p, m, 0), i % ncb)),
            pl.BlockSpec((1, tc), lambda i, p, m: (0, i % ncb)),
        ],
        out_specs=pl.BlockSpec((1, tm, tc), lambda i, p, m: (i // ncb, pin(p, m, 1), i % ncb)),
        out_shape=jax.ShapeDtypeStruct((B, L, G), F32),
        scratch_shapes=[pltpu.VMEM((L, tc), BF16), pltpu.VMEM((L, tc), BF16), pltpu.VMEM((L, tc), BF16)],
        compiler_params=_params("parallel", "arbitrary", "arbitrary"),
        name="hyena_conv",
    )(z_src, gate_src, cs, ss, hre, him, bias.reshape(1, G))


def hyena_filter_spectrum(L, G, cs, ss, w1, b1, w2, b2, w3, b3, w4, freq):
    t = jnp.linspace(0.0, 1.0, L, dtype=F32)[:, None]
    bands = (HY_EMB - 1) // 2
    fr = jnp.linspace(1e-4, bands - 1, bands, dtype=F32)
    ang = (2.0 * math.pi / L) * jnp.arange(L, dtype=F32)[:, None] * fr[None, :]
    z = jnp.concatenate([t, jnp.cos(ang), -jnp.sin(ang)], axis=-1)
    h = jnp.sin(freq * (z @ w1 + b1))
    h = jnp.sin(freq * (h @ w2 + b2))
    h = jnp.sin(freq * (h @ w3 + b3))
    h = (h @ w4).reshape(L, 2, HY_ORDER * G)
    deltas = jnp.abs(jnp.linspace(HY_MIN_DECAY, HY_MAX_DECAY, G, dtype=F32))
    h = h * jnp.tile(jnp.exp(-t * deltas), (1, HY_ORDER))[:, None, :]
    fwd = h[:, 0]
    bwd = jnp.concatenate([h[1:, 1], jnp.zeros((1, HY_ORDER * G), F32)], axis=0)
    norm = jnp.sum(jnp.abs(fwd), axis=0) + jnp.sum(jnp.abs(bwd), axis=0)
    xc, xs = dft_pair(cs, ss, jnp.concatenate([fwd, bwd], axis=1).astype(BF16))
    M = HY_ORDER * G
    a = xc[:, :M] + xc[:, M:]
    b = xs[:, M:] - xs[:, :M]
    ph = (math.pi / (2 * L)) * (jnp.arange(L, dtype=F32)[:, None] + 0.5)
    hre = (jnp.cos(ph) * a - jnp.sin(ph) * b) / norm
    him = (jnp.sin(ph) * a + jnp.cos(ph) * b) / norm
    split = lambda t_: jnp.moveaxis(t_.reshape(L, HY_ORDER, G), 1, 0)
    return split(hre), split(him)


def hyena_mixer(u, cs, ss, short_w, short_b, fw1, fb1, fw2, fb2, fw3, fb3, fw4, ffreq, bias):
    L = u.shape[1]
    G = u.shape[2] // 3
    pad = jnp.pad(u, ((0, 0), (1, 1), (0, 0)))
    us = short_w[0] * pad[:, :L] + short_w[1] * u + short_w[2] * pad[:, 2:] + short_b
    hre, him = hyena_filter_spectrum(L, G, cs, ss, fw1, fb1, fw2, fb2, fw3, fb3, fw4, ffreq)
    z = hyena_conv(us, 0, us, 1, cs, ss, hre[0], him[0], bias[0])
    return hyena_conv(z, 0, us, 2, cs, ss, hre[1], him[1], bias[1])


def rms_norm(x, w):
    y = x * lax.rsqrt(jnp.mean(x * x, axis=-1, keepdims=True) + RMS_EPS)
    return y * w


def kernel(x, c, ctx, c_ctx, ada_w, ada_b, norm1_w, norm2_w, w_in, w_out, hy_short_w, hy_short_b, hy_f_w1, hy_f_b1, hy_f_w2, hy_f_b2, hy_f_w3, hy_f_b3, hy_f_w4, hy_f_freq, hy_bias, rg_conv_w, rg_conv_b, rg_wa, rg_ba, rg_wx, rg_bx, rg_lambda, rw_mu, rw_w0, rw_w1, rw_w2, rw_a0, rw_a1, rw_a2, rw_g1, rw_g2, rw_k_k, rw_k_a, rw_r_k, rw_ln_w, rw_ln_b, rw_v0, rw_v1, rw_v2, rt_decay, rt_gn_w, moe_router_w, moe_router_b, moe_w_gu, moe_b_gu, moe_w_dn, moe_b_dn, final_norm_w):
    depth = ada_w.shape[0]
    B, L, D = x.shape
    n_ctx = ctx.shape[1]
    G = D // N_GROUPS
    cond = jnp.concatenate([jnp.broadcast_to(jax.nn.silu(c_ctx.astype(F32))[None, :], (B, D)),
                            jax.nn.silu(c.astype(F32))], axis=0)
    xs = jnp.concatenate([ctx.astype(x.dtype), x], axis=1)
    cos_tab, sin_tab = rope_tables(n_ctx, L, G)
    dft_lat = dft_matrices(L)
    dft_ctx = dft_matrices(n_ctx)
    u_first = None

    for l in range(depth):
        last = l == depth - 1
        mod = cond @ ada_w[l].astype(F32) + ada_b[l]
        mod = jnp.stack([mod[:B], mod[B:]], axis=1)
        mods = jnp.split(mod, N_MOD, axis=-1)

        u = modulated_matmul(xs, norm1_w[l], mods[0], mods[1], w_in[l].astype(BF16), n_ctx,
                             tn=w_in.shape[2] // 4)
        if l == 0:
            u_first = u

        hy_p = (hy_short_w[l], hy_short_b[l], hy_f_w1[l], hy_f_b1[l], hy_f_w2[l], hy_f_b2[l],
                hy_f_w3[l], hy_f_b3[l], hy_f_w4[l], hy_f_freq[l], hy_bias[l])
        hy_l = hyena_mixer(u[:, n_ctx:, : 3 * G], *dft_lat, *hy_p)
        if last:
            hy_c = jnp.zeros((B, n_ctx, G), F32)
        else:
            hy_c = hyena_mixer(u[:, :n_ctx, : 3 * G], *dft_ctx, *hy_p)
        hy = jnp.concatenate([hy_c, hy_l], axis=1)

        rg = [rglru_direction(u, 3, n_ctx, rg_conv_w[l, d], rg_conv_b[l, d], rg_wa[l, d], rg_ba[l, d],
                              rg_wx[l, d], rg_bx[l, d], rg_lambda[l, d], reverse=(d == 1))
              for d in range(2)]
        vmix = None if l == 0 else (rw_v0[l - 1], rw_v1[l - 1], rw_v2[l - 1])
        rw = [rwkv_direction(u, 5, n_ctx, u_first, rw_mu[l, d], rw_w0[l, d], rw_w1[l, d], rw_w2[l, d],
                             rw_a0[l, d], rw_a1[l, d], rw_a2[l, d], rw_g1[l], rw_g2[l], rw_k_k[l],
                             rw_k_a[l], rw_r_k[l], rw_ln_w[l], rw_ln_b[l], vmix, reverse=(d == 1))
              for d in range(2)]
        lg = -jax.nn.softplus(rt_decay[l].astype(F32))
        rt0 = retention_direction(u, 9, n_ctx, cos_tab, sin_tab, lg[0], rt_gn_w[l], None, reverse=False)
        rt = retention_direction(u, 9, n_ctx, cos_tab, sin_tab, lg[1], rt_gn_w[l], rt0, reverse=True)

        xs = gated_out_proj((hy, rg[0], rg[1], rw[0], rw[1], rt), w_out[l].astype(BF16), xs, mods[2], n_ctx)

        moe_p = (moe_router_w[l], moe_router_b[l], moe_w_gu[l].astype(BF16), moe_b_gu[l],
                 moe_w_dn[l].astype(BF16), moe_b_dn[l])
        seg = jnp.arange(xs.shape[1]) >= n_ctx
        pick = lambda m: jnp.where(seg[None, :, None], m[:, 1:2, :], m[:, 0:1, :])
        if last:
            xl = xs[:, n_ctx:]
            hl = rms_norm(xl, norm2_w[l]) * (1 + mods[4][:, 1:2, :]) + mods[3][:, 1:2, :]
            f = moe(hl.reshape(-1, D), *moe_p).reshape(B, L, D)
            return rms_norm(xl + mods[5][:, 1:2, :] * f, final_norm_w)
        h = rms_norm(xs, norm2_w[l]) * (1 + pick(mods[4])) + pick(mods[3])
        f = moe(h.reshape(-1, D), *moe_p).reshape(xs.shape)
        xs = xs + pick(mods[5]) * f
```

```python
import functools
import math

import jax
import jax.numpy as jnp
from jax import lax
from jax.experimental import pallas as pl
from jax.experimental.pallas import tpu as pltpu

F32 = jnp.float32
BF16 = jnp.bfloat16
HIGHEST = lax.Precision.HIGHEST

GRID_W = 64
N_GROUPS = 4
N_MOD = 6
RMS_EPS = 1e-6

HY_ORDER = 2
HY_EMB = 33
HY_TARGET = 1e-2
HY_MIN_DECAY = math.log(HY_TARGET) / 1.5
HY_MAX_DECAY = math.log(HY_TARGET) / 0.3

RG_HEADS = 8
RG_CONV = 4
RG_C = 8.0

RW_HEAD = 64
RW_GN_EPS = 64e-5
RW_CHUNK = 64
RW_SUB = 16
RW_BATCH = 2
RW_ONES = 256

RT_HEADS = 4
RT_CHUNK = 128
RT_GN_EPS = 1e-6
ROPE_BASE = 10000.0

N_EXPERTS = 32
TOP_K = 4
SWIGLU_LIMIT = 7.0
SWIGLU_ALPHA = 1.702

VMEM_LIMIT_BYTES = 52 * 1024 * 1024
LRU_CHUNK = 256
MOE_TILE = 256
ROW_TILE = 256

NT_DIMS = (((1,), (1,)), ((), ()))


def _params(*sem):
    return pltpu.CompilerParams(dimension_semantics=sem, vmem_limit_bytes=VMEM_LIMIT_BYTES)


def _scan_block(n, n_ctx_blocks, n_blocks, reverse):
    if not reverse:
        return n
    return jnp.where(n < n_ctx_blocks, n_ctx_blocks - 1 - n, n_blocks - 1 + n_ctx_blocks - n)


def _shift_rows(cur, prv, j, reverse):
    T = cur.shape[0]
    row = lax.broadcasted_iota(jnp.int32, cur.shape, 0)
    if not reverse:
        return jnp.where(row < j, pltpu.roll(prv, j, 0), pltpu.roll(cur, j, 0))
    return jnp.where(row >= T - j, pltpu.roll(prv, T - j, 0), pltpu.roll(cur, T - j, 0))


def _mm(a, b):
    return jnp.dot(a.astype(BF16), b.astype(BF16), preferred_element_type=F32)


def _mm_nt(a, b):
    return lax.dot_general(a.astype(BF16), b.astype(BF16), NT_DIMS, preferred_element_type=F32)


def _split_dot(x, m_bf16, parts):
    acc = None
    for _ in range(parts):
        hi = x.astype(BF16)
        d = jnp.dot(hi, m_bf16, preferred_element_type=F32)
        acc = d if acc is None else acc + d
        x = x - hi.astype(F32)
    return acc


def _expm1(x):
    small = x * (1.0 + x * (0.5 + x * (1.0 / 6.0 + x * (1.0 / 24.0 + x * (1.0 / 120.0)))))
    return jnp.where(jnp.abs(x) < 0.1, small, jnp.exp(x) - 1.0)


def _softplus(x):
    return jnp.maximum(x, 0.0) + jnp.log(1.0 + jnp.exp(-jnp.abs(x)))


def _sigmoid(x):
    return 1.0 / (1.0 + jnp.exp(-x))


def _modmm_kernel(x_ref, nw_ref, sh_ref, sc_ref, w_ref, o_ref):
    x = x_ref[0]
    y = x * lax.rsqrt(jnp.mean(x * x, axis=-1, keepdims=True) + RMS_EPS)
    y = y * nw_ref[...] * (1.0 + sc_ref[0]) + sh_ref[0]
    o_ref[0] = jnp.dot(y.astype(BF16), w_ref[...], preferred_element_type=F32)


def modulated_matmul(x, nw, shift, scale, w_bf16, n_ctx, tn):
    B, L, D = x.shape
    N = w_bf16.shape[1]
    tm = ROW_TILE
    ncb = n_ctx // tm
    mod_idx = lambda j, b, i: (2 * b + (i >= ncb).astype(jnp.int32), 0, 0)
    return pl.pallas_call(
        _modmm_kernel,
        grid=(N // tn, B, L // tm),
        in_specs=[
            pl.BlockSpec((1, tm, D), lambda j, b, i: (b, i, 0)),
            pl.BlockSpec((1, D), lambda j, b, i: (0, 0)),
            pl.BlockSpec((1, 1, D), mod_idx),
            pl.BlockSpec((1, 1, D), mod_idx),
            pl.BlockSpec((D, tn), lambda j, b, i: (0, j)),
        ],
        out_specs=pl.BlockSpec((1, tm, tn), lambda j, b, i: (b, i, j)),
        out_shape=jax.ShapeDtypeStruct((B, L, N), F32),
        compiler_params=_params("parallel", "parallel", "parallel"),
        name="modulated_matmul",
    )(x, nw.reshape(1, D), shift.reshape(2 * B, 1, D), scale.reshape(2 * B, 1, D), w_bf16)


def _outproj_kernel(hy_ref, rg0_ref, rg1_ref, rw0_ref, rw1_ref, rt_ref, w_ref, res_ref, g_ref, o_ref):
    G = hy_ref.shape[2]
    slabs = (hy_ref[0], rg0_ref[0] + rg1_ref[0], rw0_ref[0] + rw1_ref[0], rt_ref[0])
    acc = None
    for i, s in enumerate(slabs):
        d = jnp.dot(s.astype(BF16), w_ref[i * G:(i + 1) * G, :], preferred_element_type=F32)
        acc = d if acc is None else acc + d
    o_ref[0] = res_ref[0] + g_ref[0] * acc


def gated_out_proj(slabs, w_bf16, res, gate, n_ctx):
    B, L, D = res.shape
    G = slabs[0].shape[2]
    tm = ROW_TILE
    ncb = n_ctx // tm
    g_idx = lambda b, i: (2 * b + (i >= ncb).astype(jnp.int32), 0, 0)
    slab_spec = pl.BlockSpec((1, tm, G), lambda b, i: (b, i, 0))
    return pl.pallas_call(
        _outproj_kernel,
        grid=(B, L // tm),
        in_specs=[slab_spec] * 6 + [
            pl.BlockSpec((N_GROUPS * G, D), lambda b, i: (0, 0)),
            pl.BlockSpec((1, tm, D), lambda b, i: (b, i, 0)),
            pl.BlockSpec((1, 1, D), g_idx),
        ],
        out_specs=pl.BlockSpec((1, tm, D), lambda b, i: (b, i, 0)),
        out_shape=jax.ShapeDtypeStruct((B, L, D), F32),
        compiler_params=_params("parallel", "parallel"),
        name="gated_out_proj",
    )(*slabs, w_bf16, res, gate.reshape(2 * B, 1, D))


def _rglru_kernel(x_ref, gate_ref, cw_ref, cb_ref, wa_ref, ba_ref, wx_ref, bx_ref, lam_ref,
                  o_ref, prev_ref, carry_ref, *, n_ctx_blocks, reverse):
    T = x_ref.shape[1]
    n = pl.program_id(1)

    @pl.when(n == 0)
    def _():
        carry_ref[...] = jnp.zeros_like(carry_ref)

    @pl.when((n == 0) | (n == n_ctx_blocks))
    def _():
        prev_ref[...] = jnp.zeros_like(prev_ref)

    x = x_ref[0]
    prv = prev_ref[...]
    xc = cb_ref[...] + cw_ref[RG_CONV - 1:RG_CONV, :] * x
    for j in range(1, RG_CONV):
        xc = xc + cw_ref[RG_CONV - 1 - j:RG_CONV - j, :] * _shift_rows(x, prv, j, reverse)
    prev_ref[...] = x

    xb = xc.astype(BF16)
    r = _sigmoid(jnp.dot(xb, wa_ref[...], preferred_element_type=F32) + ba_ref[...])
    i = _sigmoid(jnp.dot(xb, wx_ref[...], preferred_element_type=F32) + bx_ref[...])
    log_a = -RG_C * r * _softplus(-lam_ref[...])
    a = jnp.exp(log_a)
    b = jnp.sqrt(-_expm1(2.0 * log_a)) * (i * xc)

    row = lax.broadcasted_iota(jnp.int32, a.shape, 0)
    s = 1
    while s < T:
        if not reverse:
            m = row >= s
            b = jnp.where(m, a * pltpu.roll(b, s, 0) + b, b)
            a = jnp.where(m, a * pltpu.roll(a, s, 0), a)
        else:
            m = row < T - s
            b = jnp.where(m, a * pltpu.roll(b, T - s, 0) + b, b)
            a = jnp.where(m, a * pltpu.roll(a, T - s, 0), a)
        s *= 2
    h = b + a * carry_ref[...]
    carry_ref[...] = h[0:1, :] if reverse else h[T - 1:T, :]
    o_ref[0] = h * jax.nn.gelu(gate_ref[0])


def _block_diag(w):
    H, di, dj = w.shape
    eye = jnp.eye(H, dtype=w.dtype)
    return (eye[:, None, :, None] * w[:, :, None, :]).reshape(H * di, H * dj)


def rglru_direction(u, col0, n_ctx, conv_w, conv_b, wa, ba, wx, bx, lam, reverse):
    B, L, _ = u.shape
    G = conv_w.shape[-1]
    T = LRU_CHUNK
    nb, ncb = L // T, n_ctx // T
    blk = lambda c: pl.BlockSpec((1, T, G), lambda b, n: (b, _scan_block(n, ncb, nb, reverse), c))
    row = pl.BlockSpec((1, G), lambda b, n: (0, 0))
    mat = pl.BlockSpec((G, G), lambda b, n: (0, 0))
    return pl.pallas_call(
        functools.partial(_rglru_kernel, n_ctx_blocks=ncb, reverse=reverse),
        grid=(B, nb),
        in_specs=[blk(col0), blk(col0 + 1), pl.BlockSpec((RG_CONV, G), lambda b, n: (0, 0)),
                  row, mat, row, mat, row, row],
        out_specs=pl.BlockSpec((1, T, G), lambda b, n: (b, _scan_block(n, ncb, nb, reverse), 0)),
        out_shape=jax.ShapeDtypeStruct((B, L, G), F32),
        scratch_shapes=[pltpu.VMEM((T, G), F32), pltpu.VMEM((1, G), F32)],
        compiler_params=_params("parallel", "arbitrary"),
        name="rglru_mixer",
    )(u, u, conv_w, conv_b.reshape(1, G), _block_diag(wa).astype(BF16), ba.reshape(1, G),
      _block_diag(wx).astype(BF16), bx.reshape(1, G), lam.reshape(1, G))


def _rwkv_kernel(*refs, heads, n_ctx_blocks, reverse, has_vmix):
    it = iter(refs)
    r_ref, k_ref, v_ref, z_ref = next(it), next(it), next(it), next(it)
    vf_ref = next(it) if has_vmix else None
    mu_ref, w0_ref, w1_ref, w2_ref, a0_ref, a1_ref, a2_ref = (next(it) for _ in range(7))
    g1_ref, g2_ref, kk_ref, ka_ref, rk_ref, lnw_ref, lnb_ref = (next(it) for _ in range(7))
    if has_vmix:
        v0_ref, v1_ref, v2_ref = next(it), next(it), next(it)
    ones_ref = next(it)
    y_ref, s_ref, prev_ref = next(it), next(it), next(it)

    nbat, C, G = r_ref.shape
    R = nbat * C
    N = RW_HEAD
    n = pl.program_id(1)

    @pl.when(n == 0)
    def _():
        s_ref[...] = jnp.zeros_like(s_ref)

    @pl.when((n == 0) | (n == n_ctx_blocks))
    def _():
        prev_ref[...] = jnp.zeros_like(prev_ref)

    ones_half = ones_ref[...]

    def head_sum(x):
        W = ones_half.shape[0]
        return jnp.concatenate([_split_dot(x[:, i * W:(i + 1) * W], ones_half, 2) for i in range(G // W)],
                               axis=1)

    raw = [ref[...].reshape(R, G) for ref in (r_ref, k_ref, v_ref, z_ref)]
    z_raw = raw[3]
    if has_vmix:
        vm = _sigmoid(v0_ref[...] + _mm(_mm(z_raw, v1_ref[...]), v2_ref[...]))
        raw[2] = raw[2] + (vf_ref[...].reshape(R, G) - raw[2]) * vm
    gate = _mm(_sigmoid(_mm(z_raw, g1_ref[...])), g2_ref[...])
    mixed = []
    for j in range(4):
        parts = []
        for e in range(nbat):
            cur = raw[j][e * C:(e + 1) * C]
            parts.append(_shift_rows(cur, prev_ref[e * 4 + j], 1, reverse))
            prev_ref[e * 4 + j] = cur
        shifted = parts[0] if nbat == 1 else jnp.concatenate(parts, axis=0)
        mixed.append(raw[j] + (shifted - raw[j]) * mu_ref[j:j + 1, :])
    r, k, v, z = mixed
    w_log = -_softplus(-(w0_ref[...] + _mm(jnp.tanh(_mm(z, w1_ref[...])), w2_ref[...]))) - 0.5
    lw = -jnp.exp(w_log)
    a = _sigmoid(a0_ref[...] + _mm(_mm(z, a1_ref[...]), a2_ref[...]))
    kk = k * kk_ref[...]
    kk = kk / jnp.maximum(jnp.sqrt(head_sum(kk * kk)), 1e-12)
    k = k * (1.0 + (a - 1.0) * ka_ref[...])
    kka = kk * a

    ii = lax.broadcasted_iota(jnp.int32, (C, C), 0)
    jj = lax.broadcasted_iota(jnp.int32, (C, C), 1)
    incl = (ii <= jj) if reverse else (ii >= jj)
    strict = (ii < jj) if reverse else (ii > jj)
    same_sub = (ii & -RW_SUB) == (jj & -RW_SUB)
    eye = (ii == jj).astype(F32)
    ri = lax.broadcasted_iota(jnp.int32, (R, R), 0)
    rj = lax.broadcasted_iota(jnp.int32, (R, R), 1)
    incl_rows = (((ri <= rj) if reverse else (ri >= rj)) & ((ri & -C) == (rj & -C))).astype(BF16)

    cum = None
    lw_part = lw
    for _ in range(3):
        hi = lw_part.astype(BF16)
        d = jnp.dot(incl_rows, hi, preferred_element_type=F32)
        cum = d if cum is None else cum + d
        lw_part = lw_part - hi.astype(F32)
    end_row = 0 if reverse else C - 1
    ends = [cum[e * C + end_row:e * C + end_row + 1, :] for e in range(nbat)]
    cum_end = ends[0] if nbat == 1 else jnp.concatenate(
        [jnp.broadcast_to(t, (C, G)) for t in ends], axis=0)
    e_neg = jnp.exp(-cum)
    e_end = jnp.exp(cum_end - cum)
    bt = kk * jnp.exp(cum - lw)
    at = -kka * e_neg
    kt = k * e_neg
    rt = r * jnp.exp(cum)
    at_end = -kka * e_end
    kt_end = k * e_end

    chains = [(e, h) for e in range(nbat) for h in range(heads)]
    cs_ = range(len(chains))
    cut = lambda t, c: t[chains[c][0] * C:(chains[c][0] + 1) * C, chains[c][1] * N:(chains[c][1] + 1) * N]
    s0 = [s_ref[c] for c in cs_]
    vh = [cut(v, c) for c in cs_]
    br = [jnp.concatenate([cut(bt, c), cut(rt, c)], axis=0) for c in cs_]
    ak = [jnp.concatenate([cut(at, c), cut(kt, c)], axis=0) for c in cs_]
    a_all = [_mm_nt(br[c], ak[c]) for c in cs_]
    a_ab = [jnp.where(strict, a_all[c][:C, :C], 0.0) for c in cs_]
    a_bk = [jnp.where(strict, a_all[c][:C, C:], 0.0) for c in cs_]
    a_r = [jnp.concatenate([jnp.where(incl, a_all[c][C:, :C], 0.0),
                            jnp.where(incl, a_all[c][C:, C:], 0.0)], axis=1) for c in cs_]
    a_d = [jnp.where(same_sub, a_ab[c], 0.0) for c in cs_]
    a_o = [jnp.where(same_sub, 0.0, a_ab[c]) for c in cs_]
    tinv = [eye + a_d[c] for c in cs_]
    p = a_d
    for _ in range(int(math.log2(RW_SUB)) - 1):
        p = [_mm(p[c], p[c]) for c in cs_]
        tinv = [tinv[c] + _mm(tinv[c], p[c]) for c in cs_]
    nn = [_mm(tinv[c], a_o[c]) for c in cs_]
    levels = int(math.log2(C // RW_SUB))
    for lvl in range(levels):
        tinv = [tinv[c] + _mm(nn[c], tinv[c]) for c in cs_]
        if lvl + 1 < levels:
            nn = [_mm(nn[c], nn[c]) for c in cs_]

    br_s = [_mm_nt(br[c], s0[c]) for c in cs_]
    rhs = [br_s[c][:C] + _mm(a_bk[c], vh[c]) for c in cs_]
    u = [_mm(tinv[c], rhs[c]) for c in cs_]
    uv = [jnp.concatenate([u[c], vh[c]], axis=0) for c in cs_]
    y = [br_s[c][C:] + _mm(a_r[c], uv[c]) for c in cs_]
    for c in cs_:
        ak_end = jnp.concatenate([cut(at_end, c), cut(kt_end, c)], axis=0)
        g_end = jnp.exp(ends[chains[c][0]][:, chains[c][1] * N:(chains[c][1] + 1) * N])
        s_ref[c] = s0[c] * g_end + _mm(uv[c].T, ak_end)
    y = jnp.concatenate([jnp.concatenate(y[e * heads:(e + 1) * heads], axis=1) for e in range(nbat)],
                        axis=0)

    inv_n = 1.0 / N
    mean = head_sum(y) * inv_n
    yc = y - mean
    var = head_sum(yc * yc) * inv_n
    yn = yc * lax.rsqrt(var + RW_GN_EPS) * lnw_ref[...] + lnb_ref[...]
    bonus = head_sum(r * k * rk_ref[...]) * v
    y_ref[...] = ((yn + bonus) * gate).reshape(nbat, C, G)


def rwkv_direction(u, col0, n_ctx, vf_u, mu, w0, w1, w2, a0, a1, a2, g1, g2, k_k, k_a, r_k,
                   ln_w, ln_b, vmix, reverse):
    B, L, _ = u.shape
    G = w0.shape[-1]
    heads = G // RW_HEAD
    C = RW_CHUNK
    nbat = RW_BATCH if B % RW_BATCH == 0 else 1
    nb, ncb = L // C, n_ctx // C
    has_vmix = vmix is not None
    blk = lambda c: pl.BlockSpec((nbat, C, G), lambda b, n: (b, _scan_block(n, ncb, nb, reverse), c))
    full = lambda arr: pl.BlockSpec(arr.shape, lambda b, n: (0,) * arr.ndim)
    row = lambda t: t.reshape(1, G)
    hid = jnp.arange(RW_ONES) // RW_HEAD
    ones_half = (hid[:, None] == hid[None, :]).astype(BF16)
    ins = [u, u, u, u]
    specs = [blk(col0), blk(col0 + 1), blk(col0 + 2), blk(col0 + 3)]
    if has_vmix:
        ins.append(vf_u)
        specs.append(blk(col0 + 2))
    params = [mu, row(w0), w1.astype(BF16), w2.astype(BF16), row(a0), a1.astype(BF16), a2.astype(BF16),
              g1.astype(BF16), g2.astype(BF16), row(k_k), row(k_a), row(r_k), row(ln_w), row(ln_b)]
    if has_vmix:
        v0, v1, v2 = vmix
        params += [row(v0), v1.astype(BF16), v2.astype(BF16)]
    params.append(ones_half)
    ins += params
    specs += [full(p_) for p_ in params]
    return pl.pallas_call(
        functools.partial(_rwkv_kernel, heads=heads, n_ctx_blocks=ncb, reverse=reverse,
                          has_vmix=has_vmix),
        grid=(B // nbat, nb),
        in_specs=specs,
        out_specs=pl.BlockSpec((nbat, C, G), lambda b, n: (b, _scan_block(n, ncb, nb, reverse), 0)),
        out_shape=jax.ShapeDtypeStruct((B, L, G), F32),
        scratch_shapes=[pltpu.VMEM((nbat * heads, RW_HEAD, RW_HEAD), F32),
                        pltpu.VMEM((nbat * 4, C, G), F32)],
        compiler_params=_params("parallel", "arbitrary"),
        name="rwkv_mixer",
    )(*ins)


def _retention_kernel(*refs, heads, reverse, finish):
    it = iter(refs)
    q_ref, k_ref, v_ref = next(it), next(it), next(it)
    cos_ref, sin_ref, dm_ref, kdec_ref, qdec_ref, cdec_ref = (next(it) for _ in range(6))
    if finish:
        g_ref, o0_ref, gnw_ref = next(it), next(it), next(it)
    o_ref, s_ref = next(it), next(it)

    C, G = q_ref.shape[1], q_ref.shape[2]
    dh = G // heads
    quarter = dh // 4

    @pl.when(pl.program_id(1) == 0)
    def _():
        s_ref[...] = jnp.zeros_like(s_ref)

    lane = lax.broadcasted_iota(jnp.int32, (C, G), 1)
    first = (lane & quarter) == 0
    cos = cos_ref[...]
    sin = sin_ref[...]

    def rope(x):
        swapped = jnp.where(first, pltpu.roll(x, G - quarter, 1), pltpu.roll(x, quarter, 1))
        return x * cos + swapped * sin

    q = rope(q_ref[0])
    k = rope(k_ref[0]) * (dh ** -0.5)
    vb = v_ref[0].astype(BF16)
    qd = (q * qdec_ref[...]).astype(BF16)
    kd = (k * kdec_ref[...]).astype(BF16)
    qb = q.astype(BF16)
    kb = k.astype(BF16)
    cdec = cdec_ref[...]
    outs = []
    for h in range(heads):
        sl = slice(h * dh, (h + 1) * dh)
        s0 = s_ref[h]
        inner = lax.dot_general(qb[:, sl], kb[:, sl], NT_DIMS, preferred_element_type=F32) * dm_ref[h]
        out = jnp.dot(inner.astype(BF16), vb[:, sl], preferred_element_type=F32)
        out = out + jnp.dot(qd[:, sl], s0.astype(BF16), preferred_element_type=F32)
        kv = jnp.dot(kd[:, sl].T, vb[:, sl], preferred_element_type=F32)
        s_ref[h] = cdec[:, sl] * s0 + kv
        if finish:
            out = out + o0_ref[0, :, sl]
            m = jnp.mean(out, axis=-1, keepdims=True)
            var = jnp.mean(jnp.square(out - m), axis=-1, keepdims=True)
            out = (out - m) * lax.rsqrt(var + RT_GN_EPS)
        outs.append(out)
    o = jnp.concatenate(outs, axis=1)
    if finish:
        g = g_ref[0]
        o = o * gnw_ref[...] * (g * _sigmoid(g))
    o_ref[0] = o


def retention_direction(u, col0, n_ctx, cos_tab, sin_tab, lg, gn_w, prev_out, reverse):
    B, L, _ = u.shape
    G = gn_w.shape[-1]
    H = RT_HEADS
    dh = G // H
    C = RT_CHUNK
    nb, ncb = L // C, n_ctx // C
    finish = prev_out is not None
    idx = jnp.arange(C, dtype=F32)
    pos = (C - 1 - idx) if reverse else idx
    diff = pos[:, None] - pos[None, :]
    keep = diff > 0 if reverse else diff >= 0
    dmask = jnp.where(keep, jnp.exp(jnp.where(keep, diff, 0.0)[None] * lg[:, None, None]), 0.0)
    kdec = jnp.repeat(jnp.exp((C - 1 - pos)[:, None] * lg[None, :]), dh, axis=1)
    qdec = jnp.repeat(jnp.exp((pos + 1.0)[:, None] * lg[None, :]), dh, axis=1)
    cdec = jnp.repeat(jnp.exp(C * lg), dh)[None, :]
    tblk = lambda b, n: _scan_block(n, ncb, nb, reverse)
    blk = lambda c: pl.BlockSpec((1, C, G), lambda b, n: (b, tblk(b, n), c))
    tab = pl.BlockSpec((C, G), lambda b, n: (tblk(b, n), 0))
    ins = [u, u, u, cos_tab, sin_tab, dmask, kdec, qdec, cdec]
    specs = [blk(col0), blk(col0 + 1), blk(col0 + 2), tab, tab,
             pl.BlockSpec((H, C, C), lambda b, n: (0, 0, 0)),
             pl.BlockSpec((C, G), lambda b, n: (0, 0)),
             pl.BlockSpec((C, G), lambda b, n: (0, 0)),
             pl.BlockSpec((1, G), lambda b, n: (0, 0))]
    if finish:
        ins += [u, prev_out, gn_w.reshape(1, G)]
        specs += [blk(col0 + 3), blk(0), pl.BlockSpec((1, G), lambda b, n: (0, 0))]
    return pl.pallas_call(
        functools.partial(_retention_kernel, heads=H, reverse=reverse, finish=finish),
        grid=(B, nb),
        in_specs=specs,
        out_specs=blk(0),
        out_shape=jax.ShapeDtypeStruct((B, L, G), F32),
        scratch_shapes=[pltpu.VMEM((H, dh, dh), F32)],
        compiler_params=_params("parallel", "arbitrary"),
        name="retention_mixer",
    )(*ins)


def rope_tables(n_ctx, L, G):
    dh = G // RT_HEADS
    quarter = dh // 4
    n_rows = L // GRID_W
    rows = jnp.repeat(jnp.arange(n_rows, dtype=F32), GRID_W)
    cols = jnp.tile(jnp.arange(GRID_W, dtype=F32), n_rows)
    inv = jnp.power(ROPE_BASE, -jnp.arange(quarter, dtype=F32) / quarter)
    lane = jnp.arange(G)
    use_cols = (lane % dh) >= (dh // 2)
    pos = jnp.where(use_cols[None, :], cols[:, None], rows[:, None])
    ang = pos * inv[lane % quarter][None, :]
    sign = jnp.where((lane % (2 * quarter)) < quarter, -1.0, 1.0)[None, :]
    cos = jnp.concatenate([jnp.ones((n_ctx, G), F32), jnp.cos(ang)], axis=0)
    sin = jnp.concatenate([jnp.zeros((n_ctx, G), F32), jnp.sin(ang) * sign], axis=0)
    return cos, sin


def _moe_kernel(te_ref, nt_ref, x_ref, wgu_ref, bgu_ref, wdn_ref, bdn_ref, g_ref, o_ref):
    i = pl.program_id(0)
    F = wdn_ref.shape[1]

    @pl.when(i < nt_ref[0])
    def _():
        gu = jnp.dot(x_ref[...], wgu_ref[0], preferred_element_type=F32) + bgu_ref[0]
        glu = jnp.minimum(gu[:, :F], SWIGLU_LIMIT)
        lin = jnp.clip(gu[:, F:], -SWIGLU_LIMIT, SWIGLU_LIMIT)
        act = glu * jax.nn.sigmoid(SWIGLU_ALPHA * glu) * (lin + 1.0)
        y = jnp.dot(act.astype(BF16), wdn_ref[0], preferred_element_type=F32) + bdn_ref[0]
        o_ref[...] = (y * g_ref[...]).astype(o_ref.dtype)

    @pl.when(i >= nt_ref[0])
    def _():
        o_ref[...] = jnp.zeros_like(o_ref)


def moe_grouped(xs, tile_expert, n_tiles_used, w_gu, b_gu, w_dn, b_dn, row_gate):
    P, D = xs.shape
    E, _, F2 = w_gu.shape
    F = F2 // 2
    tm = MOE_TILE
    grid_spec = pltpu.PrefetchScalarGridSpec(
        num_scalar_prefetch=2,
        grid=(P // tm,),
        in_specs=[
            pl.BlockSpec((tm, D), lambda i, te, nt: (i, 0)),
            pl.BlockSpec((1, D, F2), lambda i, te, nt: (te[i], 0, 0)),
            pl.BlockSpec((1, 1, F2), lambda i, te, nt: (te[i], 0, 0)),
            pl.BlockSpec((1, F, D), lambda i, te, nt: (te[i], 0, 0)),
            pl.BlockSpec((1, 1, D), lambda i, te, nt: (te[i], 0, 0)),
            pl.BlockSpec((tm, 1), lambda i, te, nt: (i, 0)),
        ],
        out_specs=pl.BlockSpec((tm, D), lambda i, te, nt: (i, 0)),
    )
    return pl.pallas_call(
        _moe_kernel,
        grid_spec=grid_spec,
        out_shape=jax.ShapeDtypeStruct((P, D), BF16),
        compiler_params=_params("arbitrary"),
        name="moe_grouped",
    )(tile_expert, n_tiles_used, xs, w_gu, b_gu.reshape(E, 1, F2), w_dn, b_dn.reshape(E, 1, D),
      row_gate)


def _router_kernel(x_ref, nw_ref, sh_ref, sc_ref, rwh_ref, rwl_ref, rb_ref, h_ref, idx_ref, p_ref):
    x = x_ref[0]
    y = x * lax.rsqrt(jnp.mean(x * x, axis=-1, keepdims=True) + RMS_EPS)
    y = y * nw_ref[...] * (1.0 + sc_ref[0]) + sh_ref[0]
    y_hi = y.astype(BF16)
    h_ref[0] = y_hi
    y_lo = (y - y_hi.astype(F32)).astype(BF16)
    logits = (jnp.dot(y_hi, rwh_ref[...], preferred_element_type=F32)
              + jnp.dot(y_lo, rwh_ref[...], preferred_element_type=F32)
              + jnp.dot(y_hi, rwl_ref[...], preferred_element_type=F32)) + rb_ref[...]
    E = logits.shape[-1]
    lane = lax.broadcasted_iota(jnp.int32, logits.shape, 1)
    vals, ids = [], []
    for _ in range(TOP_K):
        m = jnp.max(logits, axis=-1, keepdims=True)
        i = jnp.min(jnp.where(logits == m, lane, E), axis=-1, keepdims=True)
        vals.append(m)
        ids.append(i)
        logits = jnp.where(lane == i, -jnp.inf, logits)
    e = jnp.exp(jnp.concatenate(vals, axis=1) - vals[0])
    p_ref[0] = e / jnp.sum(e, axis=-1, keepdims=True)
    idx_ref[0] = jnp.concatenate(ids, axis=1)


def moe_router(xs, row0, n_rows, n_ctx, nw, shift, scale, router_w, router_b):
    B, _, D = xs.shape
    E = router_w.shape[1]
    tm = ROW_TILE
    ncb, b0 = n_ctx // tm, row0 // tm
    mod_idx = lambda b, i: (2 * b + ((i + b0) >= ncb).astype(jnp.int32), 0, 0)
    rw_hi = router_w.astype(BF16)
    rw_lo = (router_w - rw_hi.astype(F32)).astype(BF16)
    full = lambda shape: pl.BlockSpec(shape, lambda b, i: (0,) * len(shape))
    out = lambda w: pl.BlockSpec((1, tm, w), lambda b, i: (b, i, 0))
    return pl.pallas_call(
        _router_kernel,
        grid=(B, n_rows // tm),
        in_specs=[pl.BlockSpec((1, tm, D), lambda b, i: (b, i + b0, 0)), full((1, D)),
                  pl.BlockSpec((1, 1, D), mod_idx), pl.BlockSpec((1, 1, D), mod_idx),
                  full((D, E)), full((D, E)), full((1, E))],
        out_specs=[out(D), out(TOP_K), out(TOP_K)],
        out_shape=[jax.ShapeDtypeStruct((B, n_rows, D), BF16),
                   jax.ShapeDtypeStruct((B, n_rows, TOP_K), jnp.int32),
                   jax.ShapeDtypeStruct((B, n_rows, TOP_K), F32)],
        compiler_params=_params("parallel", "parallel"),
        name="moe_router",
    )(xs, nw.reshape(1, D), shift.reshape(2 * B, 1, D), scale.reshape(2 * B, 1, D),
      rw_hi, rw_lo, router_b.reshape(1, E))


def _cast_kernel(x_ref, o_ref):
    o_ref[...] = x_ref[...].astype(BF16)


def cast_bf16(w):
    C = w.shape[-1]
    R = w.size // C
    tr = 8
    while tr * 2 * C <= 2 * 1024 * 1024 and R % (tr * 2) == 0:
        tr *= 2
    spec = pl.BlockSpec((tr, C), lambda i: (i, 0))
    return pl.pallas_call(
        _cast_kernel,
        grid=(R // tr,),
        in_specs=[spec],
        out_specs=spec,
        out_shape=jax.ShapeDtypeStruct((R, C), BF16),
        compiler_params=_params("parallel"),
        name="cast_bf16",
    )(w.reshape(R, C)).reshape(w.shape)


def moe(t, idx, probs, w_gu_bf16, b_gu, w_dn_bf16, b_dn):
    T, D = t.shape
    E = N_EXPERTS
    tm = MOE_TILE
    onehot = jnp.sum(jax.nn.one_hot(idx, E, dtype=jnp.int32), axis=1)
    rank = jnp.cumsum(onehot, axis=0) - onehot
    sizes = jnp.sum(onehot, axis=0)
    padded = ((sizes + tm - 1) // tm) * tm
    ends = jnp.cumsum(padded)
    starts = ends - padded
    dest = starts[idx] + jnp.take_along_axis(rank, idx, axis=1)
    n_rows = T * TOP_K + E * tm
    tile_start = jnp.arange(n_rows // tm, dtype=jnp.int32) * tm
    tile_expert = jnp.minimum(jnp.searchsorted(ends, tile_start, side='right'), E - 1).astype(jnp.int32)
    order = jnp.argsort(idx.reshape(-1), stable=True).astype(jnp.int32)
    row_expert = jnp.repeat(tile_expert, tm)
    within = jnp.arange(n_rows, dtype=jnp.int32) - starts[row_expert]
    valid = within < sizes[row_expert]
    src = order[jnp.clip((jnp.cumsum(sizes) - sizes)[row_expert] + within, 0, T * TOP_K - 1)]
    row_token = jnp.where(valid, src // TOP_K, 0)
    row_gate = jnp.where(valid, probs.reshape(-1)[src], 0.0)
    xs = t[row_token]
    n_used = (ends[-1] // tm).astype(jnp.int32).reshape(1)
    ys = moe_grouped(xs, tile_expert, n_used, w_gu_bf16, b_gu, w_dn_bf16, b_dn,
                     row_gate.reshape(n_rows, 1))
    return jnp.sum(ys[dest.T].astype(F32), axis=0)


HY_COLS = 256
HY_ROWS = 512


def _twiddle_kernel(c1_ref, s1_ref, c2_ref, s2_ref, cs_ref, ss_ref):
    c1, s1 = c1_ref[0], s1_ref[0]
    c2, s2 = c2_ref[...], s2_ref[...]
    cs_ref[...] = (c1 * c2 - s1 * s2).astype(BF16)
    ss_ref[...] = (s1 * c2 + c1 * s2).astype(BF16)


def dft_matrices(L):
    R = min(64, L)
    n_hi = L // R
    period = 8 * L
    a = 2 * jnp.arange(L, dtype=jnp.int32) + 1
    m1 = (a[None, :] * (2 * R * jnp.arange(n_hi, dtype=jnp.int32))[:, None]) % period
    m2 = (a[None, :] * (2 * jnp.arange(R, dtype=jnp.int32) + 1)[:, None]) % period
    ang = lambda m: (m.astype(F32) - jnp.where(m >= period // 2, period, 0).astype(F32)) * (math.pi / (4 * L))
    c1, s1 = jnp.cos(ang(m1)).reshape(n_hi, 1, L), jnp.sin(ang(m1)).reshape(n_hi, 1, L)
    c2, s2 = jnp.cos(ang(m2)), jnp.sin(ang(m2))
    hi = pl.BlockSpec((1, 1, L), lambda i: (i, 0, 0))
    lo = pl.BlockSpec((R, L), lambda i: (0, 0))
    out = pl.BlockSpec((R, L), lambda i: (i, 0))
    return pl.pallas_call(
        _twiddle_kernel,
        grid=(n_hi,),
        in_specs=[hi, hi, lo, lo],
        out_specs=[out, out],
        out_shape=[jax.ShapeDtypeStruct((L, L), BF16)] * 2,
        compiler_params=_params("parallel"),
        name="dft_matrices",
    )(c1, s1, c2, s2)


def _dft_pair_kernel(cs_ref, ss_ref, x_ref, oc_ref, os_ref):
    x = x_ref[...]
    oc_ref[...] = jnp.dot(cs_ref[...], x, preferred_element_type=F32)
    os_ref[...] = jnp.dot(ss_ref[...], x, preferred_element_type=F32)


def dft_pair(cs, ss, x_bf16):
    L, M = x_bf16.shape
    tm = min(HY_ROWS, L)
    tn = min(512, M)
    mat = pl.BlockSpec((tm, L), lambda j, i: (i, 0))
    out = pl.BlockSpec((tm, tn), lambda j, i: (i, j))
    return pl.pallas_call(
        _dft_pair_kernel,
        grid=(M // tn, L // tm),
        in_specs=[mat, mat, pl.BlockSpec((L, tn), lambda j, i: (0, j))],
        out_specs=[out, out],
        out_shape=[jax.ShapeDtypeStruct((L, M), F32)] * 2,
        compiler_params=_params("parallel", "arbitrary"),
        name="dft_pair",
    )(cs, ss, x_bf16)


def _hyena_conv_kernel(z_ref, gate_ref, cs_ref, ss_ref, hre_ref, him_ref, bias_ref, o_ref,
                       zb_ref, yre_ref, yim_ref):
    phase = pl.program_id(1)
    m = pl.program_id(2)
    tm = cs_ref.shape[0]
    L = z_ref.shape[1]
    rows = pl.ds(pl.multiple_of(m * tm, tm), tm)

    @pl.when((phase == 0) & (m == 0))
    def _():
        zb_ref[...] = z_ref[0].astype(BF16)

    @pl.when(phase == 0)
    def _():
        xc = jnp.dot(cs_ref[...], zb_ref[...], preferred_element_type=F32)
        xs = jnp.dot(ss_ref[...], zb_ref[...], preferred_element_type=F32)
        hre, him = hre_ref[...], him_ref[...]
        yre_ref[rows, :] = (xc * hre + xs * him).astype(BF16)
        yim_ref[rows, :] = (xc * him - xs * hre).astype(BF16)

    @pl.when(phase == 1)
    def _():
        y = (jnp.dot(cs_ref[...], yre_ref[...], preferred_element_type=F32)
             - jnp.dot(ss_ref[...], yim_ref[...], preferred_element_type=F32)) * (1.0 / L)
        o_ref[0] = gate_ref[0] * (y + bias_ref[...] * z_ref[0, rows, :])


def hyena_conv(z_src, z_col, gate_src, gate_col, cs, ss, hre, him, bias):
    B, L, _ = z_src.shape
    G = bias.shape[0]
    tc = HY_COLS
    tm = min(HY_ROWS, L)
    ncb = G // tc
    nm = L // tm
    pin = lambda phase, m, keep: jnp.where(phase == keep, m, (nm - 1) * (1 - keep))
    return pl.pallas_call(
        _hyena_conv_kernel,
        grid=(B * ncb, 2, nm),
        in_specs=[
            pl.BlockSpec((1, L, tc), lambda i, p, m: (i // ncb, 0, z_col * ncb + i % ncb)),
            pl.BlockSpec((1, tm, tc), lambda i, p, m: (i // ncb, pin(p, m, 1), gate_col * ncb + i % ncb)),
            pl.BlockSpec((tm, L), lambda i, p, m: (m, 0)),
            pl.BlockSpec((tm, L), lambda i, p, m: (m, 0)),
            pl.BlockSpec((tm, tc), lambda i, p, m: (pin(p, m, 0), i % ncb)),
            pl.BlockSpec((tm, tc), lambda i, p, m: (pin(p, m, 0), i % ncb)),
            pl.BlockSpec((1, tc), lambda i, p, m: (0, i % ncb)),
        ],
        out_specs=pl.BlockSpec((1, tm, tc), lambda i, p, m: (i // ncb, pin(p, m, 1), i % ncb)),
        out_shape=jax.ShapeDtypeStruct((B, L, G), F32),
        scratch_shapes=[pltpu.VMEM((L, tc), BF16), pltpu.VMEM((L, tc), BF16), pltpu.VMEM((L, tc), BF16)],
        compiler_params=_params("parallel", "arbitrary", "arbitrary"),
        name="hyena_conv",
    )(z_src, gate_src, cs, ss, hre, him, bias.reshape(1, G))


def hyena_filter_spectrum(L, G, cs, ss, w1, b1, w2, b2, w3, b3, w4, freq):
    t = jnp.linspace(0.0, 1.0, L, dtype=F32)[:, None]
    bands = (HY_EMB - 1) // 2
    fr = jnp.linspace(1e-4, bands - 1, bands, dtype=F32)
    ang = (2.0 * math.pi / L) * jnp.arange(L, dtype=F32)[:, None] * fr[None, :]
    z = jnp.concatenate([t, jnp.cos(ang), -jnp.sin(ang)], axis=-1)
    h = jnp.sin(freq * (z @ w1 + b1))
    h = jnp.sin(freq * (h @ w2 + b2))
    h = jnp.sin(freq * (h @ w3 + b3))
    h = (h @ w4).reshape(L, 2, HY_ORDER * G)
    deltas = jnp.abs(jnp.linspace(HY_MIN_DECAY, HY_MAX_DECAY, G, dtype=F32))
    h = h * jnp.tile(jnp.exp(-t * deltas), (1, HY_ORDER))[:, None, :]
    fwd = h[:, 0]
    bwd = jnp.concatenate([h[1:, 1], jnp.zeros((1, HY_ORDER * G), F32)], axis=0)
    norm = jnp.sum(jnp.abs(fwd), axis=0) + jnp.sum(jnp.abs(bwd), axis=0)
    xc, xs = dft_pair(cs, ss, jnp.concatenate([fwd, bwd], axis=1).astype(BF16))
    M = HY_ORDER * G
    a = xc[:, :M] + xc[:, M:]
    b = xs[:, M:] - xs[:, :M]
    ph = (math.pi / (2 * L)) * (jnp.arange(L, dtype=F32)[:, None] + 0.5)
    hre = (jnp.cos(ph) * a - jnp.sin(ph) * b) / norm
    him = (jnp.sin(ph) * a + jnp.cos(ph) * b) / norm
    split = lambda t_: jnp.moveaxis(t_.reshape(L, HY_ORDER, G), 1, 0)
    return split(hre), split(him)


def hyena_mixer(u, cs, ss, short_w, short_b, fw1, fb1, fw2, fb2, fw3, fb3, fw4, ffreq, bias):
    L = u.shape[1]
    G = u.shape[2] // 3
    pad = jnp.pad(u, ((0, 0), (1, 1), (0, 0)))
    us = short_w[0] * pad[:, :L] + short_w[1] * u + short_w[2] * pad[:, 2:] + short_b
    hre, him = hyena_filter_spectrum(L, G, cs, ss, fw1, fb1, fw2, fb2, fw3, fb3, fw4, ffreq)
    z = hyena_conv(us, 0, us, 1, cs, ss, hre[0], him[0], bias[0])
    return hyena_conv(z, 0, us, 2, cs, ss, hre[1], him[1], bias[1])


def rms_norm(x, w):
    y = x * lax.rsqrt(jnp.mean(x * x, axis=-1, keepdims=True) + RMS_EPS)
    return y * w


def kernel(x, c, ctx, c_ctx, ada_w, ada_b, norm1_w, norm2_w, w_in, w_out, hy_short_w, hy_short_b, hy_f_w1, hy_f_b1, hy_f_w2, hy_f_b2, hy_f_w3, hy_f_b3, hy_f_w4, hy_f_freq, hy_bias, rg_conv_w, rg_conv_b, rg_wa, rg_ba, rg_wx, rg_bx, rg_lambda, rw_mu, rw_w0, rw_w1, rw_w2, rw_a0, rw_a1, rw_a2, rw_g1, rw_g2, rw_k_k, rw_k_a, rw_r_k, rw_ln_w, rw_ln_b, rw_v0, rw_v1, rw_v2, rt_decay, rt_gn_w, moe_router_w, moe_router_b, moe_w_gu, moe_b_gu, moe_w_dn, moe_b_dn, final_norm_w):
    depth = ada_w.shape[0]
    B, L, D = x.shape
    n_ctx = ctx.shape[1]
    G = D // N_GROUPS
    cond = jnp.concatenate([jnp.broadcast_to(jax.nn.silu(c_ctx.astype(F32))[None, :], (B, D)),
                            jax.nn.silu(c.astype(F32))], axis=0)
    xs = jnp.concatenate([ctx.astype(x.dtype), x], axis=1)
    cos_tab, sin_tab = rope_tables(n_ctx, L, G)
    dft_lat = dft_matrices(L)
    dft_ctx = dft_matrices(n_ctx)
    u_first = None

    for l in range(depth):
        last = l == depth - 1
        mod = cond @ ada_w[l].astype(F32) + ada_b[l]
        mod = jnp.stack([mod[:B], mod[B:]], axis=1)
        mods = jnp.split(mod, N_MOD, axis=-1)

        u = modulated_matmul(xs, norm1_w[l], mods[0], mods[1], cast_bf16(w_in[l]), n_ctx,
                             tn=w_in.shape[2] // 4)
        if l == 0:
            u_first = u

        hy_p = (hy_short_w[l], hy_short_b[l], hy_f_w1[l], hy_f_b1[l], hy_f_w2[l], hy_f_b2[l],
                hy_f_w3[l], hy_f_b3[l], hy_f_w4[l], hy_f_freq[l], hy_bias[l])
        hy_l = hyena_mixer(u[:, n_ctx:, : 3 * G], *dft_lat, *hy_p)
        if last:
            hy_c = jnp.zeros((B, n_ctx, G), F32)
        else:
            hy_c = hyena_mixer(u[:, :n_ctx, : 3 * G], *dft_ctx, *hy_p)
        hy = jnp.concatenate([hy_c, hy_l], axis=1)

        rg = [rglru_direction(u, 3, n_ctx, rg_conv_w[l, d], rg_conv_b[l, d], rg_wa[l, d], rg_ba[l, d],
                              rg_wx[l, d], rg_bx[l, d], rg_lambda[l, d], reverse=(d == 1))
              for d in range(2)]
        vmix = None if l == 0 else (rw_v0[l - 1], rw_v1[l - 1], rw_v2[l - 1])
        rw = [rwkv_direction(u, 5, n_ctx, u_first, rw_mu[l, d], rw_w0[l, d], rw_w1[l, d], rw_w2[l, d],
                             rw_a0[l, d], rw_a1[l, d], rw_a2[l, d], rw_g1[l], rw_g2[l], rw_k_k[l],
                             rw_k_a[l], rw_r_k[l], rw_ln_w[l], rw_ln_b[l], vmix, reverse=(d == 1))
              for d in range(2)]
        lg = -jax.nn.softplus(rt_decay[l].astype(F32))
        rt0 = retention_direction(u, 9, n_ctx, cos_tab, sin_tab, lg[0], rt_gn_w[l], None, reverse=False)
        rt = retention_direction(u, 9, n_ctx, cos_tab, sin_tab, lg[1], rt_gn_w[l], rt0, reverse=True)

        xs = gated_out_proj((hy, rg[0], rg[1], rw[0], rw[1], rt), cast_bf16(w_out[l]), xs, mods[2], n_ctx)

        moe_p = (cast_bf16(moe_w_gu[l]), moe_b_gu[l], cast_bf16(moe_w_dn[l]), moe_b_dn[l])
        row0 = n_ctx if last else 0
        n_rows = xs.shape[1] - row0
        h, idx, probs = moe_router(xs, row0, n_rows, n_ctx, norm2_w[l], mods[3], mods[4],
                                   moe_router_w[l], moe_router_b[l])
        f = moe(h.reshape(-1, D), idx.reshape(-1, TOP_K), probs.reshape(-1, TOP_K), *moe_p)
        f = f.reshape(B, n_rows, D)
        if last:
            return rms_norm(xs[:, n_ctx:] + mods[5][:, 1:2, :] * f, final_norm_w)
        seg = jnp.arange(xs.shape[1]) >= n_ctx
        xs = xs + jnp.where(seg[None, :, None], mods[5][:, 1:2, :], mods[5][:, 0:1, :]) * f
```

```python
import functools
import math

import jax
import jax.numpy as jnp
from jax import lax
from jax.experimental import pallas as pl
from jax.experimental.pallas import tpu as pltpu

F32 = jnp.float32
BF16 = jnp.bfloat16
HIGHEST = lax.Precision.HIGHEST

GRID_W = 64
N_GROUPS = 4
N_MOD = 6
RMS_EPS = 1e-6

HY_ORDER = 2
HY_EMB = 33
HY_TARGET = 1e-2
HY_MIN_DECAY = math.log(HY_TARGET) / 1.5
HY_MAX_DECAY = math.log(HY_TARGET) / 0.3

RG_HEADS = 8
RG_CONV = 4
RG_C = 8.0

RW_HEAD = 64
RW_GN_EPS = 64e-5
RW_CHUNK = 64
RW_SUB = 16
RW_BATCH = 4
RW_ONES = 256

RT_HEADS = 4
RT_CHUNK = 128
RT_GN_EPS = 1e-6
ROPE_BASE = 10000.0

N_EXPERTS = 32
TOP_K = 4
SWIGLU_LIMIT = 7.0
SWIGLU_ALPHA = 1.702

VMEM_LIMIT_BYTES = 52 * 1024 * 1024
LRU_CHUNK = 256
MOE_TILE = 512
ROW_TILE = 256
CAST_BLOCK_ELEMS = 2 * 1024 * 1024

NT_DIMS = (((1,), (1,)), ((), ()))


def _params(*sem):
    return pltpu.CompilerParams(dimension_semantics=sem, vmem_limit_bytes=VMEM_LIMIT_BYTES)


def _scan_block(n, n_ctx_blocks, n_blocks, reverse):
    if not reverse:
        return n
    return jnp.where(n < n_ctx_blocks, n_ctx_blocks - 1 - n, n_blocks - 1 + n_ctx_blocks - n)


def _shift_rows(cur, prv, j, reverse):
    T = cur.shape[0]
    row = lax.broadcasted_iota(jnp.int32, cur.shape, 0)
    if not reverse:
        return jnp.where(row < j, pltpu.roll(prv, j, 0), pltpu.roll(cur, j, 0))
    return jnp.where(row >= T - j, pltpu.roll(prv, T - j, 0), pltpu.roll(cur, T - j, 0))


def _mm(a, b):
    return jnp.dot(a.astype(BF16), b.astype(BF16), preferred_element_type=F32)


def _mm_nt(a, b):
    return lax.dot_general(a.astype(BF16), b.astype(BF16), NT_DIMS, preferred_element_type=F32)


def _split_dot(x, m_bf16, parts):
    acc = None
    for _ in range(parts):
        hi = x.astype(BF16)
        d = jnp.dot(hi, m_bf16, preferred_element_type=F32)
        acc = d if acc is None else acc + d
        x = x - hi.astype(F32)
    return acc


def _expm1(x):
    small = x * (1.0 + x * (0.5 + x * (1.0 / 6.0 + x * (1.0 / 24.0 + x * (1.0 / 120.0)))))
    return jnp.where(jnp.abs(x) < 0.1, small, jnp.exp(x) - 1.0)


def _softplus(x):
    return jnp.maximum(x, 0.0) + jnp.log(1.0 + jnp.exp(-jnp.abs(x)))


def _sigmoid(x):
    return 1.0 / (1.0 + jnp.exp(-x))


def _modmm_kernel(x_ref, nw_ref, sh_ref, sc_ref, w_ref, o_ref):
    x = x_ref[0]
    y = x * lax.rsqrt(jnp.mean(x * x, axis=-1, keepdims=True) + RMS_EPS)
    y = y * nw_ref[...] * (1.0 + sc_ref[0]) + sh_ref[0]
    o_ref[0] = jnp.dot(y.astype(BF16), w_ref[...], preferred_element_type=F32)


def modulated_matmul(x, nw, shift, scale, w_bf16, layer, n_ctx, tn):
    B, L, D = x.shape
    N = w_bf16.shape[2]
    tm = ROW_TILE
    ncb = n_ctx // tm
    mod_idx = lambda j, b, i: (2 * b + (i >= ncb).astype(jnp.int32), 0, 0)
    return pl.pallas_call(
        _modmm_kernel,
        grid=(N // tn, B, L // tm),
        in_specs=[
            pl.BlockSpec((1, tm, D), lambda j, b, i: (b, i, 0)),
            pl.BlockSpec((1, D), lambda j, b, i: (0, 0)),
            pl.BlockSpec((1, 1, D), mod_idx),
            pl.BlockSpec((1, 1, D), mod_idx),
            pl.BlockSpec((None, D, tn), lambda j, b, i: (layer, 0, j)),
        ],
        out_specs=pl.BlockSpec((1, tm, tn), lambda j, b, i: (b, i, j)),
        out_shape=jax.ShapeDtypeStruct((B, L, N), F32),
        compiler_params=_params("parallel", "parallel", "parallel"),
        name="modulated_matmul",
    )(x, nw.reshape(1, D), shift.reshape(2 * B, 1, D), scale.reshape(2 * B, 1, D), w_bf16)


def _outproj_kernel(hy_ref, rg0_ref, rg1_ref, rw0_ref, rw1_ref, rt_ref, w_ref, res_ref, g_ref, o_ref):
    G = hy_ref.shape[2]
    slabs = (hy_ref[0], rg0_ref[0] + rg1_ref[0], rw0_ref[0] + rw1_ref[0], rt_ref[0])
    acc = None
    for i, s in enumerate(slabs):
        d = jnp.dot(s.astype(BF16), w_ref[i * G:(i + 1) * G, :], preferred_element_type=F32)
        acc = d if acc is None else acc + d
    o_ref[0] = res_ref[0] + g_ref[0] * acc


def gated_out_proj(slabs, w_bf16, layer, res, gate, n_ctx):
    B, L, D = res.shape
    G = slabs[0].shape[2]
    tm = ROW_TILE
    ncb = n_ctx // tm
    g_idx = lambda b, i: (2 * b + (i >= ncb).astype(jnp.int32), 0, 0)
    slab_spec = pl.BlockSpec((1, tm, G), lambda b, i: (b, i, 0))
    return pl.pallas_call(
        _outproj_kernel,
        grid=(B, L // tm),
        in_specs=[slab_spec] * 6 + [
            pl.BlockSpec((None, N_GROUPS * G, D), lambda b, i: (layer, 0, 0)),
            pl.BlockSpec((1, tm, D), lambda b, i: (b, i, 0)),
            pl.BlockSpec((1, 1, D), g_idx),
        ],
        out_specs=pl.BlockSpec((1, tm, D), lambda b, i: (b, i, 0)),
        out_shape=jax.ShapeDtypeStruct((B, L, D), F32),
        compiler_params=_params("parallel", "parallel"),
        name="gated_out_proj",
    )(*slabs, w_bf16, res, gate.reshape(2 * B, 1, D))


def _rglru_kernel(x_ref, gate_ref, cw_ref, cb_ref, wa_ref, ba_ref, wx_ref, bx_ref, lam_ref,
                  o_ref, prev_ref, carry_ref, *, n_ctx_blocks, reverse):
    T = x_ref.shape[1]
    n = pl.program_id(1)

    @pl.when(n == 0)
    def _():
        carry_ref[...] = jnp.zeros_like(carry_ref)

    @pl.when((n == 0) | (n == n_ctx_blocks))
    def _():
        prev_ref[...] = jnp.zeros_like(prev_ref)

    x = x_ref[0]
    prv = prev_ref[...]
    xc = cb_ref[...] + cw_ref[RG_CONV - 1:RG_CONV, :] * x
    for j in range(1, RG_CONV):
        xc = xc + cw_ref[RG_CONV - 1 - j:RG_CONV - j, :] * _shift_rows(x, prv, j, reverse)
    prev_ref[...] = x

    xb = xc.astype(BF16)
    r = _sigmoid(jnp.dot(xb, wa_ref[...], preferred_element_type=F32) + ba_ref[...])
    i = _sigmoid(jnp.dot(xb, wx_ref[...], preferred_element_type=F32) + bx_ref[...])
    log_a = -RG_C * r * _softplus(-lam_ref[...])
    a = jnp.exp(log_a)
    b = jnp.sqrt(-_expm1(2.0 * log_a)) * (i * xc)

    row = lax.broadcasted_iota(jnp.int32, a.shape, 0)
    s = 1
    while s < T:
        if not reverse:
            m = row >= s
            b = jnp.where(m, a * pltpu.roll(b, s, 0) + b, b)
            a = jnp.where(m, a * pltpu.roll(a, s, 0), a)
        else:
            m = row < T - s
            b = jnp.where(m, a * pltpu.roll(b, T - s, 0) + b, b)
            a = jnp.where(m, a * pltpu.roll(a, T - s, 0), a)
        s *= 2
    h = b + a * carry_ref[...]
    carry_ref[...] = h[0:1, :] if reverse else h[T - 1:T, :]
    o_ref[0] = h * jax.nn.gelu(gate_ref[0])


def _block_diag(w):
    H, di, dj = w.shape
    eye = jnp.eye(H, dtype=w.dtype)
    return (eye[:, None, :, None] * w[:, :, None, :]).reshape(H * di, H * dj)


def rglru_direction(u, col0, n_ctx, conv_w, conv_b, wa, ba, wx, bx, lam, reverse):
    B, L, _ = u.shape
    G = conv_w.shape[-1]
    T = LRU_CHUNK
    nb, ncb = L // T, n_ctx // T
    blk = lambda c: pl.BlockSpec((1, T, G), lambda b, n: (b, _scan_block(n, ncb, nb, reverse), c))
    row = pl.BlockSpec((1, G), lambda b, n: (0, 0))
    mat = pl.BlockSpec((G, G), lambda b, n: (0, 0))
    return pl.pallas_call(
        functools.partial(_rglru_kernel, n_ctx_blocks=ncb, reverse=reverse),
        grid=(B, nb),
        in_specs=[blk(col0), blk(col0 + 1), pl.BlockSpec((RG_CONV, G), lambda b, n: (0, 0)),
                  row, mat, row, mat, row, row],
        out_specs=pl.BlockSpec((1, T, G), lambda b, n: (b, _scan_block(n, ncb, nb, reverse), 0)),
        out_shape=jax.ShapeDtypeStruct((B, L, G), F32),
        scratch_shapes=[pltpu.VMEM((T, G), F32), pltpu.VMEM((1, G), F32)],
        compiler_params=_params("parallel", "arbitrary"),
        name="rglru_mixer",
    )(u, u, conv_w, conv_b.reshape(1, G), _block_diag(wa).astype(BF16), ba.reshape(1, G),
      _block_diag(wx).astype(BF16), bx.reshape(1, G), lam.reshape(1, G))


def _rwkv_kernel(*refs, heads, n_ctx_blocks, reverse, has_vmix):
    it = iter(refs)
    r_ref, k_ref, v_ref, z_ref = next(it), next(it), next(it), next(it)
    vf_ref = next(it) if has_vmix else None
    mu_ref, w0_ref, w1_ref, w2_ref, a0_ref, a1_ref, a2_ref = (next(it) for _ in range(7))
    g1_ref, g2_ref, kk_ref, ka_ref, rk_ref, lnw_ref, lnb_ref = (next(it) for _ in range(7))
    if has_vmix:
        v0_ref, v1_ref, v2_ref = next(it), next(it), next(it)
    ones_ref = next(it)
    y_ref, s_ref, prev_ref = next(it), next(it), next(it)

    nbat, C, G = r_ref.shape
    R = nbat * C
    N = RW_HEAD
    n = pl.program_id(1)

    @pl.when(n == 0)
    def _():
        s_ref[...] = jnp.zeros_like(s_ref)

    @pl.when((n == 0) | (n == n_ctx_blocks))
    def _():
        prev_ref[...] = jnp.zeros_like(prev_ref)

    ones_half = ones_ref[...]

    def head_sum(x):
        W = ones_half.shape[0]
        return jnp.concatenate([_split_dot(x[:, i * W:(i + 1) * W], ones_half, 2) for i in range(G // W)],
                               axis=1)

    raw = [ref[...].reshape(R, G) for ref in (r_ref, k_ref, v_ref, z_ref)]
    z_raw = raw[3]
    if has_vmix:
        vm = _sigmoid(v0_ref[...] + _mm(_mm(z_raw, v1_ref[...]), v2_ref[...]))
        raw[2] = raw[2] + (vf_ref[...].reshape(R, G) - raw[2]) * vm
    gate = _mm(_sigmoid(_mm(z_raw, g1_ref[...])), g2_ref[...])
    mixed = []
    for j in range(4):
        parts = []
        for e in range(nbat):
            cur = raw[j][e * C:(e + 1) * C]
            parts.append(_shift_rows(cur, prev_ref[e * 4 + j], 1, reverse))
            prev_ref[e * 4 + j] = cur
        shifted = parts[0] if nbat == 1 else jnp.concatenate(parts, axis=0)
        mixed.append(raw[j] + (shifted - raw[j]) * mu_ref[j:j + 1, :])
    r, k, v, z = mixed
    w_log = -_softplus(-(w0_ref[...] + _mm(jnp.tanh(_mm(z, w1_ref[...])), w2_ref[...]))) - 0.5
    lw = -jnp.exp(w_log)
    a = _sigmoid(a0_ref[...] + _mm(_mm(z, a1_ref[...]), a2_ref[...]))
    kk = k * kk_ref[...]
    kk = kk / jnp.maximum(jnp.sqrt(head_sum(kk * kk)), 1e-12)
    k = k * (1.0 + (a - 1.0) * ka_ref[...])
    kka = kk * a

    ii = lax.broadcasted_iota(jnp.int32, (C, C), 0)
    jj = lax.broadcasted_iota(jnp.int32, (C, C), 1)
    incl = (ii <= jj) if reverse else (ii >= jj)
    strict = (ii < jj) if reverse else (ii > jj)
    same_sub = (ii & -RW_SUB) == (jj & -RW_SUB)
    eye = (ii == jj).astype(F32)
    ri = lax.broadcasted_iota(jnp.int32, (R, R), 0)
    rj = lax.broadcasted_iota(jnp.int32, (R, R), 1)
    incl_rows = (((ri <= rj) if reverse else (ri >= rj)) & ((ri & -C) == (rj & -C))).astype(BF16)

    cum = None
    lw_part = lw
    for _ in range(3):
        hi = lw_part.astype(BF16)
        d = jnp.dot(incl_rows, hi, preferred_element_type=F32)
        cum = d if cum is None else cum + d
        lw_part = lw_part - hi.astype(F32)
    end_row = 0 if reverse else C - 1
    ends = [cum[e * C + end_row:e * C + end_row + 1, :] for e in range(nbat)]
    cum_end = ends[0] if nbat == 1 else jnp.concatenate(
        [jnp.broadcast_to(t, (C, G)) for t in ends], axis=0)
    e_neg = jnp.exp(-cum)
    e_end = jnp.exp(cum_end - cum)
    bt = kk * jnp.exp(cum - lw)
    at = -kka * e_neg
    kt = k * e_neg
    rt = r * jnp.exp(cum)
    at_end = -kka * e_end
    kt_end = k * e_end

    chains = [(e, h) for e in range(nbat) for h in range(heads)]
    cs_ = range(len(chains))
    cut = lambda t, c: t[chains[c][0] * C:(chains[c][0] + 1) * C, chains[c][1] * N:(chains[c][1] + 1) * N]
    s0 = [s_ref[c] for c in cs_]
    vh = [cut(v, c) for c in cs_]
    br = [jnp.concatenate([cut(bt, c), cut(rt, c)], axis=0) for c in cs_]
    ak = [jnp.concatenate([cut(at, c), cut(kt, c)], axis=0) for c in cs_]
    a_all = [_mm_nt(br[c], ak[c]) for c in cs_]
    a_ab = [jnp.where(strict, a_all[c][:C, :C], 0.0) for c in cs_]
    a_bk = [jnp.where(strict, a_all[c][:C, C:], 0.0) for c in cs_]
    a_r = [jnp.concatenate([jnp.where(incl, a_all[c][C:, :C], 0.0),
                            jnp.where(incl, a_all[c][C:, C:], 0.0)], axis=1) for c in cs_]
    a_d = [jnp.where(same_sub, a_ab[c], 0.0) for c in cs_]
    a_o = [jnp.where(same_sub, 0.0, a_ab[c]) for c in cs_]
    tinv = [eye + a_d[c] for c in cs_]
    p = a_d
    for _ in range(int(math.log2(RW_SUB)) - 1):
        p = [_mm(p[c], p[c]) for c in cs_]
        tinv = [tinv[c] + _mm(tinv[c], p[c]) for c in cs_]
    nn = [_mm(tinv[c], a_o[c]) for c in cs_]
    levels = int(math.log2(C // RW_SUB))
    for lvl in range(levels):
        tinv = [tinv[c] + _mm(nn[c], tinv[c]) for c in cs_]
        if lvl + 1 < levels:
            nn = [_mm(nn[c], nn[c]) for c in cs_]

    br_s = [_mm_nt(br[c], s0[c]) for c in cs_]
    rhs = [br_s[c][:C] + _mm(a_bk[c], vh[c]) for c in cs_]
    u = [_mm(tinv[c], rhs[c]) for c in cs_]
    uv = [jnp.concatenate([u[c], vh[c]], axis=0) for c in cs_]
    y = [br_s[c][C:] + _mm(a_r[c], uv[c]) for c in cs_]
    for c in cs_:
        ak_end = jnp.concatenate([cut(at_end, c), cut(kt_end, c)], axis=0)
        g_end = jnp.exp(ends[chains[c][0]][:, chains[c][1] * N:(chains[c][1] + 1) * N])
        s_ref[c] = s0[c] * g_end + _mm(uv[c].T, ak_end)
    y = jnp.concatenate([jnp.concatenate(y[e * heads:(e + 1) * heads], axis=1) for e in range(nbat)],
                        axis=0)

    inv_n = 1.0 / N
    mean = head_sum(y) * inv_n
    yc = y - mean
    var = head_sum(yc * yc) * inv_n
    yn = yc * lax.rsqrt(var + RW_GN_EPS) * lnw_ref[...] + lnb_ref[...]
    bonus = head_sum(r * k * rk_ref[...]) * v
    y_ref[...] = ((yn + bonus) * gate).reshape(nbat, C, G)


def rwkv_direction(u, col0, n_ctx, vf_u, mu, w0, w1, w2, a0, a1, a2, g1, g2, k_k, k_a, r_k,
                   ln_w, ln_b, vmix, reverse):
    B, L, _ = u.shape
    G = w0.shape[-1]
    heads = G // RW_HEAD
    C = RW_CHUNK
    nbat = RW_BATCH if B % RW_BATCH == 0 else 1
    nb, ncb = L // C, n_ctx // C
    has_vmix = vmix is not None
    blk = lambda c: pl.BlockSpec((nbat, C, G), lambda b, n: (b, _scan_block(n, ncb, nb, reverse), c))
    full = lambda arr: pl.BlockSpec(arr.shape, lambda b, n: (0,) * arr.ndim)
    row = lambda t: t.reshape(1, G)
    hid = jnp.arange(RW_ONES) // RW_HEAD
    ones_half = (hid[:, None] == hid[None, :]).astype(BF16)
    ins = [u, u, u, u]
    specs = [blk(col0), blk(col0 + 1), blk(col0 + 2), blk(col0 + 3)]
    if has_vmix:
        ins.append(vf_u)
        specs.append(blk(col0 + 2))
    params = [mu, row(w0), w1.astype(BF16), w2.astype(BF16), row(a0), a1.astype(BF16), a2.astype(BF16),
              g1.astype(BF16), g2.astype(BF16), row(k_k), row(k_a), row(r_k), row(ln_w), row(ln_b)]
    if has_vmix:
        v0, v1, v2 = vmix
        params += [row(v0), v1.astype(BF16), v2.astype(BF16)]
    params.append(ones_half)
    ins += params
    specs += [full(p_) for p_ in params]
    return pl.pallas_call(
        functools.partial(_rwkv_kernel, heads=heads, n_ctx_blocks=ncb, reverse=reverse,
                          has_vmix=has_vmix),
        grid=(B // nbat, nb),
        in_specs=specs,
        out_specs=pl.BlockSpec((nbat, C, G), lambda b, n: (b, _scan_block(n, ncb, nb, reverse), 0)),
        out_shape=jax.ShapeDtypeStruct((B, L, G), F32),
        scratch_shapes=[pltpu.VMEM((nbat * heads, RW_HEAD, RW_HEAD), F32),
                        pltpu.VMEM((nbat * 4, C, G), F32)],
        compiler_params=_params("parallel", "arbitrary"),
        name="rwkv_mixer",
    )(*ins)


def _retention_kernel(*refs, heads, reverse, finish):
    it = iter(refs)
    q_ref, k_ref, v_ref = next(it), next(it), next(it)
    cos_ref, sin_ref, dm_ref, kdec_ref, qdec_ref, cdec_ref = (next(it) for _ in range(6))
    if finish:
        g_ref, o0_ref, gnw_ref = next(it), next(it), next(it)
    o_ref, s_ref = next(it), next(it)

    C, G = q_ref.shape[1], q_ref.shape[2]
    dh = G // heads
    quarter = dh // 4

    @pl.when(pl.program_id(1) == 0)
    def _():
        s_ref[...] = jnp.zeros_like(s_ref)

    lane = lax.broadcasted_iota(jnp.int32, (C, G), 1)
    first = (lane & quarter) == 0
    cos = cos_ref[...]
    sin = sin_ref[...]

    def rope(x):
        swapped = jnp.where(first, pltpu.roll(x, G - quarter, 1), pltpu.roll(x, quarter, 1))
        return x * cos + swapped * sin

    q = rope(q_ref[0])
    k = rope(k_ref[0]) * (dh ** -0.5)
    vb = v_ref[0].astype(BF16)
    qd = (q * qdec_ref[...]).astype(BF16)
    kd = (k * kdec_ref[...]).astype(BF16)
    qb = q.astype(BF16)
    kb = k.astype(BF16)
    cdec = cdec_ref[...]
    outs = []
    for h in range(heads):
        sl = slice(h * dh, (h + 1) * dh)
        s0 = s_ref[h]
        inner = lax.dot_general(qb[:, sl], kb[:, sl], NT_DIMS, preferred_element_type=F32) * dm_ref[h]
        out = jnp.dot(inner.astype(BF16), vb[:, sl], preferred_element_type=F32)
        out = out + jnp.dot(qd[:, sl], s0.astype(BF16), preferred_element_type=F32)
        kv = jnp.dot(kd[:, sl].T, vb[:, sl], preferred_element_type=F32)
        s_ref[h] = cdec[:, sl] * s0 + kv
        if finish:
            out = out + o0_ref[0, :, sl]
            m = jnp.mean(out, axis=-1, keepdims=True)
            var = jnp.mean(jnp.square(out - m), axis=-1, keepdims=True)
            out = (out - m) * lax.rsqrt(var + RT_GN_EPS)
        outs.append(out)
    o = jnp.concatenate(outs, axis=1)
    if finish:
        g = g_ref[0]
        o = o * gnw_ref[...] * (g * _sigmoid(g))
    o_ref[0] = o


def retention_direction(u, col0, n_ctx, cos_tab, sin_tab, lg, gn_w, prev_out, reverse):
    B, L, _ = u.shape
    G = gn_w.shape[-1]
    H = RT_HEADS
    dh = G // H
    C = RT_CHUNK
    nb, ncb = L // C, n_ctx // C
    finish = prev_out is not None
    idx = jnp.arange(C, dtype=F32)
    pos = (C - 1 - idx) if reverse else idx
    diff = pos[:, None] - pos[None, :]
    keep = diff > 0 if reverse else diff >= 0
    dmask = jnp.where(keep, jnp.exp(jnp.where(keep, diff, 0.0)[None] * lg[:, None, None]), 0.0)
    kdec = jnp.repeat(jnp.exp((C - 1 - pos)[:, None] * lg[None, :]), dh, axis=1)
    qdec = jnp.repeat(jnp.exp((pos + 1.0)[:, None] * lg[None, :]), dh, axis=1)
    cdec = jnp.repeat(jnp.exp(C * lg), dh)[None, :]
    tblk = lambda b, n: _scan_block(n, ncb, nb, reverse)
    blk = lambda c: pl.BlockSpec((1, C, G), lambda b, n: (b, tblk(b, n), c))
    tab = pl.BlockSpec((C, G), lambda b, n: (tblk(b, n), 0))
    ins = [u, u, u, cos_tab, sin_tab, dmask, kdec, qdec, cdec]
    specs = [blk(col0), blk(col0 + 1), blk(col0 + 2), tab, tab,
             pl.BlockSpec((H, C, C), lambda b, n: (0, 0, 0)),
             pl.BlockSpec((C, G), lambda b, n: (0, 0)),
             pl.BlockSpec((C, G), lambda b, n: (0, 0)),
             pl.BlockSpec((1, G), lambda b, n: (0, 0))]
    if finish:
        ins += [u, prev_out, gn_w.reshape(1, G)]
        specs += [blk(col0 + 3), blk(0), pl.BlockSpec((1, G), lambda b, n: (0, 0))]
    return pl.pallas_call(
        functools.partial(_retention_kernel, heads=H, reverse=reverse, finish=finish),
        grid=(B, nb),
        in_specs=specs,
        out_specs=blk(0),
        out_shape=jax.ShapeDtypeStruct((B, L, G), F32),
        scratch_shapes=[pltpu.VMEM((H, dh, dh), F32)],
        compiler_params=_params("parallel", "arbitrary"),
        name="retention_mixer",
    )(*ins)


def rope_tables(n_ctx, L, G):
    dh = G // RT_HEADS
    quarter = dh // 4
    n_rows = L // GRID_W
    rows = jnp.repeat(jnp.arange(n_rows, dtype=F32), GRID_W)
    cols = jnp.tile(jnp.arange(GRID_W, dtype=F32), n_rows)
    inv = jnp.power(ROPE_BASE, -jnp.arange(quarter, dtype=F32) / quarter)
    lane = jnp.arange(G)
    use_cols = (lane % dh) >= (dh // 2)
    pos = jnp.where(use_cols[None, :], cols[:, None], rows[:, None])
    ang = pos * inv[lane % quarter][None, :]
    sign = jnp.where((lane % (2 * quarter)) < quarter, -1.0, 1.0)[None, :]
    cos = jnp.concatenate([jnp.ones((n_ctx, G), F32), jnp.cos(ang)], axis=0)
    sin = jnp.concatenate([jnp.zeros((n_ctx, G), F32), jnp.sin(ang) * sign], axis=0)
    return cos, sin


def _moe_kernel(te_ref, nt_ref, x_ref, wgu_ref, bgu_ref, wdn_ref, bdn_ref, g_ref, o_ref):
    i = pl.program_id(0)
    F = wdn_ref.shape[1]

    @pl.when(i < nt_ref[0])
    def _():
        gu = jnp.dot(x_ref[...], wgu_ref[0], preferred_element_type=F32) + bgu_ref[0]
        glu = jnp.minimum(gu[:, :F], SWIGLU_LIMIT)
        lin = jnp.clip(gu[:, F:], -SWIGLU_LIMIT, SWIGLU_LIMIT)
        act = glu * jax.nn.sigmoid(SWIGLU_ALPHA * glu) * (lin + 1.0)
        y = jnp.dot(act.astype(BF16), wdn_ref[0], preferred_element_type=F32) + bdn_ref[0]
        o_ref[...] = (y * g_ref[...]).astype(o_ref.dtype)

    @pl.when(i >= nt_ref[0])
    def _():
        o_ref[...] = jnp.zeros_like(o_ref)


def moe_grouped(xs, tile_expert, n_tiles_used, layer, w_gu, b_gu, w_dn, b_dn, row_gate):
    P, D = xs.shape
    _, E, _, F2 = w_gu.shape
    F = F2 // 2
    tm = MOE_TILE
    grid_spec = pltpu.PrefetchScalarGridSpec(
        num_scalar_prefetch=2,
        grid=(P // tm,),
        in_specs=[
            pl.BlockSpec((tm, D), lambda i, te, nt: (i, 0)),
            pl.BlockSpec((None, 1, D, F2), lambda i, te, nt: (layer, te[i], 0, 0)),
            pl.BlockSpec((1, 1, F2), lambda i, te, nt: (te[i], 0, 0)),
            pl.BlockSpec((None, 1, F, D), lambda i, te, nt: (layer, te[i], 0, 0)),
            pl.BlockSpec((1, 1, D), lambda i, te, nt: (te[i], 0, 0)),
            pl.BlockSpec((tm, 1), lambda i, te, nt: (i, 0)),
        ],
        out_specs=pl.BlockSpec((tm, D), lambda i, te, nt: (i, 0)),
    )
    return pl.pallas_call(
        _moe_kernel,
        grid_spec=grid_spec,
        out_shape=jax.ShapeDtypeStruct((P, D), BF16),
        compiler_params=_params("arbitrary"),
        name="moe_grouped",
    )(tile_expert, n_tiles_used, xs, w_gu, b_gu.reshape(E, 1, F2), w_dn, b_dn.reshape(E, 1, D),
      row_gate)


def _router_kernel(x_ref, nw_ref, sh_ref, sc_ref, rwh_ref, rwl_ref, rb_ref,
                   h_ref, idx_ref, p_ref, rank_ref, cnt_ref, carry_ref):
    @pl.when((pl.program_id(0) == 0) & (pl.program_id(1) == 0))
    def _():
        carry_ref[...] = jnp.zeros_like(carry_ref)

    x = x_ref[0]
    y = x * lax.rsqrt(jnp.mean(x * x, axis=-1, keepdims=True) + RMS_EPS)
    y = y * nw_ref[...] * (1.0 + sc_ref[0]) + sh_ref[0]
    y_hi = y.astype(BF16)
    h_ref[0] = y_hi
    y_lo = (y - y_hi.astype(F32)).astype(BF16)
    logits = (jnp.dot(y_hi, rwh_ref[...], preferred_element_type=F32)
              + jnp.dot(y_lo, rwh_ref[...], preferred_element_type=F32)
              + jnp.dot(y_hi, rwl_ref[...], preferred_element_type=F32)) + rb_ref[...]
    E = logits.shape[-1]
    lane = lax.broadcasted_iota(jnp.int32, logits.shape, 1)
    vals, ids = [], []
    for _ in range(TOP_K):
        m = jnp.max(logits, axis=-1, keepdims=True)
        i = jnp.min(jnp.where(logits == m, lane, E), axis=-1, keepdims=True)
        vals.append(m)
        ids.append(i)
        logits = jnp.where(lane == i, -jnp.inf, logits)
    e = jnp.exp(jnp.concatenate(vals, axis=1) - vals[0])
    p_ref[0] = e / jnp.sum(e, axis=-1, keepdims=True)
    idx_ref[0] = jnp.concatenate(ids, axis=1)

    tm = x.shape[0]
    chosen = [lane == i for i in ids]
    onehot = sum(c.astype(F32) for c in chosen)
    earlier = (lax.broadcasted_iota(jnp.int32, (tm, tm), 1)
               < lax.broadcasted_iota(jnp.int32, (tm, tm), 0)).astype(BF16)
    prefix = jnp.dot(earlier, onehot.astype(BF16), preferred_element_type=F32) + carry_ref[...]
    rank_ref[0] = jnp.concatenate(
        [jnp.sum(jnp.where(c, prefix, 0.0), axis=-1, keepdims=True) for c in chosen], axis=1).astype(jnp.int32)
    total = carry_ref[...] + jnp.sum(onehot, axis=0, keepdims=True)
    carry_ref[...] = total
    cnt_ref[...] = total


def moe_router(xs, row0, n_rows, n_ctx, nw, shift, scale, router_w, router_b):
    B, _, D = xs.shape
    E = router_w.shape[1]
    tm = ROW_TILE
    ncb, b0 = n_ctx // tm, row0 // tm
    mod_idx = lambda b, i: (2 * b + ((i + b0) >= ncb).astype(jnp.int32), 0, 0)
    rw_hi = router_w.astype(BF16)
    rw_lo = (router_w - rw_hi.astype(F32)).astype(BF16)
    full = lambda shape: pl.BlockSpec(shape, lambda b, i: (0,) * len(shape))
    out = lambda w: pl.BlockSpec((1, tm, w), lambda b, i: (b, i, 0))
    return pl.pallas_call(
        _router_kernel,
        grid=(B, n_rows // tm),
        in_specs=[pl.BlockSpec((1, tm, D), lambda b, i: (b, i + b0, 0)), full((1, D)),
                  pl.BlockSpec((1, 1, D), mod_idx), pl.BlockSpec((1, 1, D), mod_idx),
                  full((D, E)), full((D, E)), full((1, E))],
        out_specs=[out(D), out(TOP_K), out(TOP_K), out(TOP_K), full((1, E))],
        out_shape=[jax.ShapeDtypeStruct((B, n_rows, D), BF16),
                   jax.ShapeDtypeStruct((B, n_rows, TOP_K), jnp.int32),
                   jax.ShapeDtypeStruct((B, n_rows, TOP_K), F32),
                   jax.ShapeDtypeStruct((B, n_rows, TOP_K), jnp.int32),
                   jax.ShapeDtypeStruct((1, E), F32)],
        scratch_shapes=[pltpu.VMEM((1, E), F32)],
        compiler_params=_params("arbitrary", "arbitrary"),
        name="moe_router",
    )(xs, nw.reshape(1, D), shift.reshape(2 * B, 1, D), scale.reshape(2 * B, 1, D),
      rw_hi, rw_lo, router_b.reshape(1, E))


def _cast_kernel(x_ref, o_ref):
    o_ref[...] = x_ref[...].astype(BF16)


def cast_bf16(w):
    C = w.shape[-1]
    R = w.size // C
    tr = 8
    while tr * 2 * C <= CAST_BLOCK_ELEMS and R % (tr * 2) == 0:
        tr *= 2
    spec = pl.BlockSpec((tr, C), lambda i: (i, 0))
    return pl.pallas_call(
        _cast_kernel,
        grid=(R // tr,),
        in_specs=[spec],
        out_specs=spec,
        out_shape=jax.ShapeDtypeStruct((R, C), BF16),
        compiler_params=_params("parallel"),
        name="cast_bf16",
    )(w.reshape(R, C)).reshape(w.shape)


def _combine_kernel(g_ref, res_ref, gate_ref, fw_ref, o_ref, *, final_norm):
    f = g_ref[0, 0].astype(F32)
    for k in range(1, g_ref.shape[0]):
        f = f + g_ref[k, 0].astype(F32)
    y = res_ref[0] + gate_ref[0] * f
    if final_norm:
        y = y * lax.rsqrt(jnp.mean(y * y, axis=-1, keepdims=True) + RMS_EPS) * fw_ref[...]
    o_ref[0] = y


def moe_combine(g, xs, row0, n_ctx, gate, final_w):
    K, B, n_rows, D = g.shape
    tm = ROW_TILE
    ncb, b0 = n_ctx // tm, row0 // tm
    g_idx = lambda b, i: (2 * b + ((i + b0) >= ncb).astype(jnp.int32), 0, 0)
    final_norm = final_w is not None
    fw = final_w if final_norm else jnp.ones((D,), F32)
    return pl.pallas_call(
        functools.partial(_combine_kernel, final_norm=final_norm),
        grid=(B, n_rows // tm),
        in_specs=[pl.BlockSpec((K, 1, tm, D), lambda b, i: (0, b, i, 0)),
                  pl.BlockSpec((1, tm, D), lambda b, i: (b, i + b0, 0)),
                  pl.BlockSpec((1, 1, D), g_idx),
                  pl.BlockSpec((1, D), lambda b, i: (0, 0))],
        out_specs=pl.BlockSpec((1, tm, D), lambda b, i: (b, i, 0)),
        out_shape=jax.ShapeDtypeStruct((B, n_rows, D), F32),
        compiler_params=_params("parallel", "parallel"),
        name="moe_combine",
    )(g, xs, gate.reshape(2 * B, 1, D), fw.reshape(1, D))


def moe(t, idx, probs, rank, counts, layer, w_gu_bf16, b_gu, w_dn_bf16, b_dn):
    T, D = t.shape
    E = N_EXPERTS
    tm = MOE_TILE
    sizes = counts.astype(jnp.int32)
    padded = ((sizes + tm - 1) // tm) * tm
    ends = jnp.cumsum(padded)
    starts = ends - padded
    dest = starts[idx] + rank
    n_rows = T * TOP_K + E * tm
    tile_start = jnp.arange(n_rows // tm, dtype=jnp.int32) * tm
    tile_expert = jnp.minimum(jnp.searchsorted(ends, tile_start, side='right'), E - 1).astype(jnp.int32)
    order = jnp.argsort(idx.reshape(-1), stable=True).astype(jnp.int32)
    row_expert = jnp.repeat(tile_expert, tm)
    within = jnp.arange(n_rows, dtype=jnp.int32) - starts[row_expert]
    valid = within < sizes[row_expert]
    src = order[jnp.clip((jnp.cumsum(sizes) - sizes)[row_expert] + within, 0, T * TOP_K - 1)]
    row_token = jnp.where(valid, src // TOP_K, 0)
    row_gate = jnp.where(valid, probs.reshape(-1)[src], 0.0)
    xs = t[row_token]
    n_used = (ends[-1] // tm).astype(jnp.int32).reshape(1)
    ys = moe_grouped(xs, tile_expert, n_used, layer, w_gu_bf16, b_gu, w_dn_bf16, b_dn,
                     row_gate.reshape(n_rows, 1))
    return ys[dest.T]


HY_COLS = 256
HY_ROWS = 512


def _twiddle_kernel(c1_ref, s1_ref, c2_ref, s2_ref, cs_ref, ss_ref):
    c1, s1 = c1_ref[0], s1_ref[0]
    c2, s2 = c2_ref[...], s2_ref[...]
    cs_ref[...] = (c1 * c2 - s1 * s2).astype(BF16)
    ss_ref[...] = (s1 * c2 + c1 * s2).astype(BF16)


def dft_matrices(L):
    R = min(64, L)
    n_hi = L // R
    period = 8 * L
    a = 2 * jnp.arange(L, dtype=jnp.int32) + 1
    m1 = (a[None, :] * (2 * R * jnp.arange(n_hi, dtype=jnp.int32))[:, None]) % period
    m2 = (a[None, :] * (2 * jnp.arange(R, dtype=jnp.int32) + 1)[:, None]) % period
    ang = lambda m: (m.astype(F32) - jnp.where(m >= period // 2, period, 0).astype(F32)) * (math.pi / (4 * L))
    c1, s1 = jnp.cos(ang(m1)).reshape(n_hi, 1, L), jnp.sin(ang(m1)).reshape(n_hi, 1, L)
    c2, s2 = jnp.cos(ang(m2)), jnp.sin(ang(m2))
    hi = pl.BlockSpec((1, 1, L), lambda i: (i, 0, 0))
    lo = pl.BlockSpec((R, L), lambda i: (0, 0))
    out = pl.BlockSpec((R, L), lambda i: (i, 0))
    return pl.pallas_call(
        _twiddle_kernel,
        grid=(n_hi,),
        in_specs=[hi, hi, lo, lo],
        out_specs=[out, out],
        out_shape=[jax.ShapeDtypeStruct((L, L), BF16)] * 2,
        compiler_params=_params("parallel"),
        name="dft_matrices",
    )(c1, s1, c2, s2)


def _dft_pair_kernel(cs_ref, ss_ref, x_ref, oc_ref, os_ref):
    x = x_ref[...]
    oc_ref[...] = jnp.dot(cs_ref[...], x, preferred_element_type=F32)
    os_ref[...] = jnp.dot(ss_ref[...], x, preferred_element_type=F32)


def dft_pair(cs, ss, x_bf16):
    L, M = x_bf16.shape
    tm = min(HY_ROWS, L)
    tn = min(512, M)
    mat = pl.BlockSpec((tm, L), lambda j, i: (i, 0))
    out = pl.BlockSpec((tm, tn), lambda j, i: (i, j))
    return pl.pallas_call(
        _dft_pair_kernel,
        grid=(M // tn, L // tm),
        in_specs=[mat, mat, pl.BlockSpec((L, tn), lambda j, i: (0, j))],
        out_specs=[out, out],
        out_shape=[jax.ShapeDtypeStruct((L, M), F32)] * 2,
        compiler_params=_params("parallel", "arbitrary"),
        name="dft_pair",
    )(cs, ss, x_bf16)


def _hyena_conv_kernel(z_ref, gate_ref, cs_ref, ss_ref, hre_ref, him_ref, bias_ref, o_ref,
                       zb_ref, yre_ref, yim_ref):
    phase = pl.program_id(1)
    m = pl.program_id(2)
    tm = cs_ref.shape[0]
    L = z_ref.shape[1]
    rows = pl.ds(pl.multiple_of(m * tm, tm), tm)

    @pl.when((phase == 0) & (m == 0))
    def _():
        zb_ref[...] = z_ref[0].astype(BF16)

    @pl.when(phase == 0)
    def _():
        xc = jnp.dot(cs_ref[...], zb_ref[...], preferred_element_type=F32)
        xs = jnp.dot(ss_ref[...], zb_ref[...], preferred_element_type=F32)
        hre, him = hre_ref[...], him_ref[...]
        yre_ref[rows, :] = (xc * hre + xs * him).astype(BF16)
        yim_ref[rows, :] = (xc * him - xs * hre).astype(BF16)

    @pl.when(phase == 1)
    def _():
        y = (jnp.dot(cs_ref[...], yre_ref[...], preferred_element_type=F32)
             - jnp.dot(ss_ref[...], yim_ref[...], preferred_element_type=F32)) * (1.0 / L)
        o_ref[0] = gate_ref[0] * (y + bias_ref[...] * z_ref[0, rows, :])


def hyena_conv(z_src, z_col, gate_src, gate_col, cs, ss, hre, him, bias):
    B, L, _ = z_src.shape
    G = bias.shape[0]
    tc = HY_COLS
    tm = min(HY_ROWS, L)
    ncb = G // tc
    nm = L // tm
    pin = lambda phase, m, keep: jnp.where(phase == keep, m, (nm - 1) * (1 - keep))
    return pl.pallas_call(
        _hyena_conv_kernel,
        grid=(B * ncb, 2, nm),
        in_specs=[
            pl.BlockSpec((1, L, tc), lambda i, p, m: (i // ncb, 0, z_col * ncb + i % ncb)),
            pl.BlockSpec((1, tm, tc), lambda i, p, m: (i // ncb, pin(p, m, 1), gate_col * ncb + i % ncb)),
            pl.BlockSpec((tm, L), lambda i, p, m: (m, 0)),
            pl.BlockSpec((tm, L), lambda i, p, m: (m, 0)),
            pl.BlockSpec((tm, tc), lambda i, p, m: (pin(p, m, 0), i % ncb)),
            pl.BlockSpec((tm, tc), lambda i, p, m: (pin(p, m, 0), i % ncb)),
            pl.BlockSpec((1, tc), lambda i, p, m: (0, i % ncb)),
        ],
        out_specs=pl.BlockSpec((1, tm, tc), lambda i, p, m: (i // ncb, pin(p, m, 1), i % ncb)),
        out_shape=jax.ShapeDtypeStruct((B, L, G), F32),
        scratch_shapes=[pltpu.VMEM((L, tc), BF16), pltpu.VMEM((L, tc), BF16), pltpu.VMEM((L, tc), BF16)],
        compiler_params=_params("parallel", "arbitrary", "arbitrary"),
        name="hyena_conv",
    )(z_src, gate_src, cs, ss, hre, him, bias.reshape(1, G))


def hyena_filter_spectrum(L, G, cs, ss, w1, b1, w2, b2, w3, b3, w4, freq):
    t = jnp.linspace(0.0, 1.0, L, dtype=F32)[:, None]
    bands = (HY_EMB - 1) // 2
    fr = jnp.linspace(1e-4, bands - 1, bands, dtype=F32)
    ang = (2.0 * math.pi / L) * jnp.arange(L, dtype=F32)[:, None] * fr[None, :]
    z = jnp.concatenate([t, jnp.cos(ang), -jnp.sin(ang)], axis=-1)
    h = jnp.sin(freq * (z @ w1 + b1))
    h = jnp.sin(freq * (h @ w2 + b2))
    h = jnp.sin(freq * (h @ w3 + b3))
    h = (h @ w4).reshape(L, 2, HY_ORDER * G)
    deltas = jnp.abs(jnp.linspace(HY_MIN_DECAY, HY_MAX_DECAY, G, dtype=F32))
    h = h * jnp.tile(jnp.exp(-t * deltas), (1, HY_ORDER))[:, None, :]
    fwd = h[:, 0]
    bwd = jnp.concatenate([h[1:, 1], jnp.zeros((1, HY_ORDER * G), F32)], axis=0)
    norm = jnp.sum(jnp.abs(fwd), axis=0) + jnp.sum(jnp.abs(bwd), axis=0)
    xc, xs = dft_pair(cs, ss, jnp.concatenate([fwd, bwd], axis=1).astype(BF16))
    M = HY_ORDER * G
    a = xc[:, :M] + xc[:, M:]
    b = xs[:, M:] - xs[:, :M]
    ph = (math.pi / (2 * L)) * (jnp.arange(L, dtype=F32)[:, None] + 0.5)
    hre = (jnp.cos(ph) * a - jnp.sin(ph) * b) / norm
    him = (jnp.sin(ph) * a + jnp.cos(ph) * b) / norm
    split = lambda t_: jnp.moveaxis(t_.reshape(L, HY_ORDER, G), 1, 0)
    return split(hre), split(him)


def hyena_mixer(u, cs, ss, short_w, short_b, fw1, fb1, fw2, fb2, fw3, fb3, fw4, ffreq, bias):
    L = u.shape[1]
    G = u.shape[2] // 3
    pad = jnp.pad(u, ((0, 0), (1, 1), (0, 0)))
    us = short_w[0] * pad[:, :L] + short_w[1] * u + short_w[2] * pad[:, 2:] + short_b
    hre, him = hyena_filter_spectrum(L, G, cs, ss, fw1, fb1, fw2, fb2, fw3, fb3, fw4, ffreq)
    z = hyena_conv(us, 0, us, 1, cs, ss, hre[0], him[0], bias[0])
    return hyena_conv(z, 0, us, 2, cs, ss, hre[1], him[1], bias[1])


def rms_norm(x, w):
    y = x * lax.rsqrt(jnp.mean(x * x, axis=-1, keepdims=True) + RMS_EPS)
    return y * w


def kernel(x, c, ctx, c_ctx, ada_w, ada_b, norm1_w, norm2_w, w_in, w_out, hy_short_w, hy_short_b, hy_f_w1, hy_f_b1, hy_f_w2, hy_f_b2, hy_f_w3, hy_f_b3, hy_f_w4, hy_f_freq, hy_bias, rg_conv_w, rg_conv_b, rg_wa, rg_ba, rg_wx, rg_bx, rg_lambda, rw_mu, rw_w0, rw_w1, rw_w2, rw_a0, rw_a1, rw_a2, rw_g1, rw_g2, rw_k_k, rw_k_a, rw_r_k, rw_ln_w, rw_ln_b, rw_v0, rw_v1, rw_v2, rt_decay, rt_gn_w, moe_router_w, moe_router_b, moe_w_gu, moe_b_gu, moe_w_dn, moe_b_dn, final_norm_w):
    depth = ada_w.shape[0]
    B, L, D = x.shape
    n_ctx = ctx.shape[1]
    G = D // N_GROUPS
    cond = jnp.concatenate([jnp.broadcast_to(jax.nn.silu(c_ctx.astype(F32))[None, :], (B, D)),
                            jax.nn.silu(c.astype(F32))], axis=0)
    xs = jnp.concatenate([ctx.astype(x.dtype), x], axis=1)
    cos_tab, sin_tab = rope_tables(n_ctx, L, G)
    dft_lat = dft_matrices(L)
    dft_ctx = dft_matrices(n_ctx)
    w_in_b, w_out_b, w_gu_b, w_dn_b = (cast_bf16(w) for w in (w_in, w_out, moe_w_gu, moe_w_dn))
    u_first = None

    for l in range(depth):
        last = l == depth - 1
        mod = cond @ ada_w[l].astype(F32) + ada_b[l]
        mod = jnp.stack([mod[:B], mod[B:]], axis=1)
        mods = jnp.split(mod, N_MOD, axis=-1)

        u = modulated_matmul(xs, norm1_w[l], mods[0], mods[1], w_in_b, l, n_ctx, tn=w_in.shape[2] // 4)
        if l == 0:
            u_first = u

        hy_p = (hy_short_w[l], hy_short_b[l], hy_f_w1[l], hy_f_b1[l], hy_f_w2[l], hy_f_b2[l],
                hy_f_w3[l], hy_f_b3[l], hy_f_w4[l], hy_f_freq[l], hy_bias[l])
        hy_l = hyena_mixer(u[:, n_ctx:, : 3 * G], *dft_lat, *hy_p)
        if last:
            hy_c = jnp.zeros((B, n_ctx, G), F32)
        else:
            hy_c = hyena_mixer(u[:, :n_ctx, : 3 * G], *dft_ctx, *hy_p)
        hy = jnp.concatenate([hy_c, hy_l], axis=1)

        rg = [rglru_direction(u, 3, n_ctx, rg_conv_w[l, d], rg_conv_b[l, d], rg_wa[l, d], rg_ba[l, d],
                              rg_wx[l, d], rg_bx[l, d], rg_lambda[l, d], reverse=(d == 1))
              for d in range(2)]
        vmix = None if l == 0 else (rw_v0[l - 1], rw_v1[l - 1], rw_v2[l - 1])
        rw = [rwkv_direction(u, 5, n_ctx, u_first, rw_mu[l, d], rw_w0[l, d], rw_w1[l, d], rw_w2[l, d],
                             rw_a0[l, d], rw_a1[l, d], rw_a2[l, d], rw_g1[l], rw_g2[l], rw_k_k[l],
                             rw_k_a[l], rw_r_k[l], rw_ln_w[l], rw_ln_b[l], vmix, reverse=(d == 1))
              for d in range(2)]
        lg = -jax.nn.softplus(rt_decay[l].astype(F32))
        rt0 = retention_direction(u, 9, n_ctx, cos_tab, sin_tab, lg[0], rt_gn_w[l], None, reverse=False)
        rt = retention_direction(u, 9, n_ctx, cos_tab, sin_tab, lg[1], rt_gn_w[l], rt0, reverse=True)

        xs = gated_out_proj((hy, rg[0], rg[1], rw[0], rw[1], rt), w_out_b, l, xs, mods[2], n_ctx)

        row0 = n_ctx if last else 0
        n_rows = xs.shape[1] - row0
        h, idx, probs, rank, counts = moe_router(xs, row0, n_rows, n_ctx, norm2_w[l], mods[3], mods[4],
                                                 moe_router_w[l], moe_router_b[l])
        flat = lambda t_: t_.reshape(-1, TOP_K)
        g = moe(h.reshape(-1, D), flat(idx), flat(probs), flat(rank), counts.reshape(-1), l,
                w_gu_b, moe_b_gu[l], w_dn_b, moe_b_dn[l])
        xs = moe_combine(g.reshape(TOP_K, B, n_rows, D), xs, row0, n_ctx, mods[5],
                         final_norm_w if last else None)
    return xs
```

```python
import functools
import math

import jax
import jax.numpy as jnp
from jax import lax
from jax.experimental import pallas as pl
from jax.experimental.pallas import tpu as pltpu

F32 = jnp.float32
BF16 = jnp.bfloat16
HIGHEST = lax.Precision.HIGHEST

GRID_W = 64
N_GROUPS = 4
N_MOD = 6
RMS_EPS = 1e-6

HY_ORDER = 2
HY_EMB = 33
HY_TARGET = 1e-2
HY_MIN_DECAY = math.log(HY_TARGET) / 1.5
HY_MAX_DECAY = math.log(HY_TARGET) / 0.3

RG_HEADS = 8
RG_CONV = 4
RG_C = 8.0

RW_HEAD = 64
RW_GN_EPS = 64e-5
RW_CHUNK = 64
RW_SUB = 16
RW_BATCH = 4
RW_ONES = 256

RT_HEADS = 4
RT_CHUNK = 128
RT_GN_EPS = 1e-6
ROPE_BASE = 10000.0

N_EXPERTS = 32
TOP_K = 4
SWIGLU_LIMIT = 7.0
SWIGLU_ALPHA = 1.702

VMEM_LIMIT_BYTES = 52 * 1024 * 1024
LRU_CHUNK = 256
MOE_TILE = 256
MOE_CHUNK_ROWS = 512
MOE_CHUNKS_PER_STEP = 2
ROW_TILE = 256
CAST_BLOCK_ELEMS = 2 * 1024 * 1024

NT_DIMS = (((1,), (1,)), ((), ()))


def _params(*sem):
    return pltpu.CompilerParams(dimension_semantics=sem, vmem_limit_bytes=VMEM_LIMIT_BYTES)


def _scan_block(n, n_ctx_blocks, n_blocks, reverse):
    if not reverse:
        return n
    return jnp.where(n < n_ctx_blocks, n_ctx_blocks - 1 - n, n_blocks - 1 + n_ctx_blocks - n)


def _shift_rows(cur, prv, j, reverse):
    T = cur.shape[0]
    row = lax.broadcasted_iota(jnp.int32, cur.shape, 0)
    if not reverse:
        return jnp.where(row < j, pltpu.roll(prv, j, 0), pltpu.roll(cur, j, 0))
    return jnp.where(row >= T - j, pltpu.roll(prv, T - j, 0), pltpu.roll(cur, T - j, 0))


def _mm(a, b):
    return jnp.dot(a.astype(BF16), b.astype(BF16), preferred_element_type=F32)


def _mm_nt(a, b):
    return lax.dot_general(a.astype(BF16), b.astype(BF16), NT_DIMS, preferred_element_type=F32)


def _split_dot(x, m_bf16, parts):
    acc = None
    for _ in range(parts):
        hi = x.astype(BF16)
        d = jnp.dot(hi, m_bf16, preferred_element_type=F32)
        acc = d if acc is None else acc + d
        x = x - hi.astype(F32)
    return acc


def _expm1(x):
    small = x * (1.0 + x * (0.5 + x * (1.0 / 6.0 + x * (1.0 / 24.0 + x * (1.0 / 120.0)))))
    return jnp.where(jnp.abs(x) < 0.1, small, jnp.exp(x) - 1.0)


def _softplus(x):
    return jnp.maximum(x, 0.0) + jnp.log(1.0 + jnp.exp(-jnp.abs(x)))


def _sigmoid(x):
    return 1.0 / (1.0 + jnp.exp(-x))


def _modmm_kernel(x_ref, nw_ref, sh_ref, sc_ref, w_ref, o_ref):
    x = x_ref[0]
    y = x * lax.rsqrt(jnp.mean(x * x, axis=-1, keepdims=True) + RMS_EPS)
    y = y * nw_ref[...] * (1.0 + sc_ref[0]) + sh_ref[0]
    o_ref[0] = jnp.dot(y.astype(BF16), w_ref[...], preferred_element_type=F32)


def modulated_matmul(x, nw, shift, scale, w_bf16, layer, n_ctx, tn):
    B, L, D = x.shape
    N = w_bf16.shape[2]
    tm = ROW_TILE
    ncb = n_ctx // tm
    mod_idx = lambda j, b, i: (2 * b + (i >= ncb).astype(jnp.int32), 0, 0)
    return pl.pallas_call(
        _modmm_kernel,
        grid=(N // tn, B, L // tm),
        in_specs=[
            pl.BlockSpec((1, tm, D), lambda j, b, i: (b, i, 0)),
            pl.BlockSpec((1, D), lambda j, b, i: (0, 0)),
            pl.BlockSpec((1, 1, D), mod_idx),
            pl.BlockSpec((1, 1, D), mod_idx),
            pl.BlockSpec((None, D, tn), lambda j, b, i: (layer, 0, j)),
        ],
        out_specs=pl.BlockSpec((1, tm, tn), lambda j, b, i: (b, i, j)),
        out_shape=jax.ShapeDtypeStruct((B, L, N), F32),
        compiler_params=_params("parallel", "parallel", "parallel"),
        name="modulated_matmul",
    )(x, nw.reshape(1, D), shift.reshape(2 * B, 1, D), scale.reshape(2 * B, 1, D), w_bf16)


def _outproj_kernel(hy_ref, rg0_ref, rg1_ref, rw0_ref, rw1_ref, rt_ref, w_ref, res_ref, g_ref, o_ref):
    G = hy_ref.shape[2]
    slabs = (hy_ref[0], rg0_ref[0] + rg1_ref[0], rw0_ref[0] + rw1_ref[0], rt_ref[0])
    acc = None
    for i, s in enumerate(slabs):
        d = jnp.dot(s.astype(BF16), w_ref[i * G:(i + 1) * G, :], preferred_element_type=F32)
        acc = d if acc is None else acc + d
    o_ref[0] = res_ref[0] + g_ref[0] * acc


def gated_out_proj(slabs, w_bf16, layer, res, gate, n_ctx):
    B, L, D = res.shape
    G = slabs[0].shape[2]
    tm = ROW_TILE
    ncb = n_ctx // tm
    g_idx = lambda b, i: (2 * b + (i >= ncb).astype(jnp.int32), 0, 0)
    slab_spec = pl.BlockSpec((1, tm, G), lambda b, i: (b, i, 0))
    return pl.pallas_call(
        _outproj_kernel,
        grid=(B, L // tm),
        in_specs=[slab_spec] * 6 + [
            pl.BlockSpec((None, N_GROUPS * G, D), lambda b, i: (layer, 0, 0)),
            pl.BlockSpec((1, tm, D), lambda b, i: (b, i, 0)),
            pl.BlockSpec((1, 1, D), g_idx),
        ],
        out_specs=pl.BlockSpec((1, tm, D), lambda b, i: (b, i, 0)),
        out_shape=jax.ShapeDtypeStruct((B, L, D), F32),
        compiler_params=_params("parallel", "parallel"),
        name="gated_out_proj",
    )(*slabs, w_bf16, res, gate.reshape(2 * B, 1, D))


def _rglru_kernel(x_ref, gate_ref, cw_ref, cb_ref, wa_ref, ba_ref, wx_ref, bx_ref, lam_ref,
                  o_ref, prev_ref, carry_ref, *, n_ctx_blocks, reverse):
    T = x_ref.shape[1]
    n = pl.program_id(1)

    @pl.when(n == 0)
    def _():
        carry_ref[...] = jnp.zeros_like(carry_ref)

    @pl.when((n == 0) | (n == n_ctx_blocks))
    def _():
        prev_ref[...] = jnp.zeros_like(prev_ref)

    x = x_ref[0]
    prv = prev_ref[...]
    xc = cb_ref[...] + cw_ref[RG_CONV - 1:RG_CONV, :] * x
    for j in range(1, RG_CONV):
        xc = xc + cw_ref[RG_CONV - 1 - j:RG_CONV - j, :] * _shift_rows(x, prv, j, reverse)
    prev_ref[...] = x

    xb = xc.astype(BF16)
    r = _sigmoid(jnp.dot(xb, wa_ref[...], preferred_element_type=F32) + ba_ref[...])
    i = _sigmoid(jnp.dot(xb, wx_ref[...], preferred_element_type=F32) + bx_ref[...])
    log_a = -RG_C * r * _softplus(-lam_ref[...])
    a = jnp.exp(log_a)
    b = jnp.sqrt(-_expm1(2.0 * log_a)) * (i * xc)

    row = lax.broadcasted_iota(jnp.int32, a.shape, 0)
    s = 1
    while s < T:
        if not reverse:
            m = row >= s
            b = jnp.where(m, a * pltpu.roll(b, s, 0) + b, b)
            a = jnp.where(m, a * pltpu.roll(a, s, 0), a)
        else:
            m = row < T - s
            b = jnp.where(m, a * pltpu.roll(b, T - s, 0) + b, b)
            a = jnp.where(m, a * pltpu.roll(a, T - s, 0), a)
        s *= 2
    h = b + a * carry_ref[...]
    carry_ref[...] = h[0:1, :] if reverse else h[T - 1:T, :]
    o_ref[0] = h * jax.nn.gelu(gate_ref[0])


def _block_diag(w):
    H, di, dj = w.shape
    eye = jnp.eye(H, dtype=w.dtype)
    return (eye[:, None, :, None] * w[:, :, None, :]).reshape(H * di, H * dj)


def rglru_direction(u, col0, n_ctx, conv_w, conv_b, wa, ba, wx, bx, lam, reverse):
    B, L, _ = u.shape
    G = conv_w.shape[-1]
    T = LRU_CHUNK
    nb, ncb = L // T, n_ctx // T
    blk = lambda c: pl.BlockSpec((1, T, G), lambda b, n: (b, _scan_block(n, ncb, nb, reverse), c))
    row = pl.BlockSpec((1, G), lambda b, n: (0, 0))
    mat = pl.BlockSpec((G, G), lambda b, n: (0, 0))
    return pl.pallas_call(
        functools.partial(_rglru_kernel, n_ctx_blocks=ncb, reverse=reverse),
        grid=(B, nb),
        in_specs=[blk(col0), blk(col0 + 1), pl.BlockSpec((RG_CONV, G), lambda b, n: (0, 0)),
                  row, mat, row, mat, row, row],
        out_specs=pl.BlockSpec((1, T, G), lambda b, n: (b, _scan_block(n, ncb, nb, reverse), 0)),
        out_shape=jax.ShapeDtypeStruct((B, L, G), F32),
        scratch_shapes=[pltpu.VMEM((T, G), F32), pltpu.VMEM((1, G), F32)],
        compiler_params=_params("parallel", "arbitrary"),
        name="rglru_mixer",
    )(u, u, conv_w, conv_b.reshape(1, G), _block_diag(wa).astype(BF16), ba.reshape(1, G),
      _block_diag(wx).astype(BF16), bx.reshape(1, G), lam.reshape(1, G))


def _rwkv_kernel(*refs, heads, n_ctx_blocks, reverse, has_vmix):
    it = iter(refs)
    r_ref, k_ref, v_ref, z_ref = next(it), next(it), next(it), next(it)
    vf_ref = next(it) if has_vmix else None
    mu_ref, w0_ref, w1_ref, w2_ref, a0_ref, a1_ref, a2_ref = (next(it) for _ in range(7))
    g1_ref, g2_ref, kk_ref, ka_ref, rk_ref, lnw_ref, lnb_ref = (next(it) for _ in range(7))
    if has_vmix:
        v0_ref, v1_ref, v2_ref = next(it), next(it), next(it)
    ones_ref = next(it)
    y_ref, s_ref, prev_ref = next(it), next(it), next(it)

    nbat, C, G = r_ref.shape
    R = nbat * C
    N = RW_HEAD
    n = pl.program_id(1)

    @pl.when(n == 0)
    def _():
        s_ref[...] = jnp.zeros_like(s_ref)

    @pl.when((n == 0) | (n == n_ctx_blocks))
    def _():
        prev_ref[...] = jnp.zeros_like(prev_ref)

    ones_half = ones_ref[...]

    def head_sum(x):
        W = ones_half.shape[0]
        return jnp.concatenate([_split_dot(x[:, i * W:(i + 1) * W], ones_half, 2) for i in range(G // W)],
                               axis=1)

    raw = [ref[...].reshape(R, G) for ref in (r_ref, k_ref, v_ref, z_ref)]
    z_raw = raw[3]
    if has_vmix:
        vm = _sigmoid(v0_ref[...] + _mm(_mm(z_raw, v1_ref[...]), v2_ref[...]))
        raw[2] = raw[2] + (vf_ref[...].reshape(R, G) - raw[2]) * vm
    gate = _mm(_sigmoid(_mm(z_raw, g1_ref[...])), g2_ref[...])
    mixed = []
    for j in range(4):
        parts = []
        for e in range(nbat):
            cur = raw[j][e * C:(e + 1) * C]
            parts.append(_shift_rows(cur, prev_ref[e * 4 + j], 1, reverse))
            prev_ref[e * 4 + j] = cur
        shifted = parts[0] if nbat == 1 else jnp.concatenate(parts, axis=0)
        mixed.append(raw[j] + (shifted - raw[j]) * mu_ref[j:j + 1, :])
    r, k, v, z = mixed
    w_log = -_softplus(-(w0_ref[...] + _mm(jnp.tanh(_mm(z, w1_ref[...])), w2_ref[...]))) - 0.5
    lw = -jnp.exp(w_log)
    a = _sigmoid(a0_ref[...] + _mm(_mm(z, a1_ref[...]), a2_ref[...]))
    kk = k * kk_ref[...]
    kk = kk / jnp.maximum(jnp.sqrt(head_sum(kk * kk)), 1e-12)
    k = k * (1.0 + (a - 1.0) * ka_ref[...])
    kka = kk * a

    ii = lax.broadcasted_iota(jnp.int32, (C, C), 0)
    jj = lax.broadcasted_iota(jnp.int32, (C, C), 1)
    incl = (ii <= jj) if reverse else (ii >= jj)
    strict = (ii < jj) if reverse else (ii > jj)
    same_sub = (ii & -RW_SUB) == (jj & -RW_SUB)
    eye = (ii == jj).astype(F32)
    ri = lax.broadcasted_iota(jnp.int32, (R, R), 0)
    rj = lax.broadcasted_iota(jnp.int32, (R, R), 1)
    incl_rows = (((ri <= rj) if reverse else (ri >= rj)) & ((ri & -C) == (rj & -C))).astype(BF16)

    cum = None
    lw_part = lw
    for _ in range(3):
        hi = lw_part.astype(BF16)
        d = jnp.dot(incl_rows, hi, preferred_element_type=F32)
        cum = d if cum is None else cum + d
        lw_part = lw_part - hi.astype(F32)
    end_row = 0 if reverse else C - 1
    ends = [cum[e * C + end_row:e * C + end_row + 1, :] for e in range(nbat)]
    cum_end = ends[0] if nbat == 1 else jnp.concatenate(
        [jnp.broadcast_to(t, (C, G)) for t in ends], axis=0)
    e_neg = jnp.exp(-cum)
    e_end = jnp.exp(cum_end - cum)
    bt = kk * jnp.exp(cum - lw)
    at = -kka * e_neg
    kt = k * e_neg
    rt = r * jnp.exp(cum)
    at_end = -kka * e_end
    kt_end = k * e_end

    chains = [(e, h) for e in range(nbat) for h in range(heads)]
    cs_ = range(len(chains))
    cut = lambda t, c: t[chains[c][0] * C:(chains[c][0] + 1) * C, chains[c][1] * N:(chains[c][1] + 1) * N]
    s0 = [s_ref[c] for c in cs_]
    vh = [cut(v, c) for c in cs_]
    br = [jnp.concatenate([cut(bt, c), cut(rt, c)], axis=0) for c in cs_]
    ak = [jnp.concatenate([cut(at, c), cut(kt, c)], axis=0) for c in cs_]
    a_all = [_mm_nt(br[c], ak[c]) for c in cs_]
    a_ab = [jnp.where(strict, a_all[c][:C, :C], 0.0) for c in cs_]
    a_bk = [jnp.where(strict, a_all[c][:C, C:], 0.0) for c in cs_]
    a_r = [jnp.concatenate([jnp.where(incl, a_all[c][C:, :C], 0.0),
                            jnp.where(incl, a_all[c][C:, C:], 0.0)], axis=1) for c in cs_]
    a_d = [jnp.where(same_sub, a_ab[c], 0.0) for c in cs_]
    a_o = [jnp.where(same_sub, 0.0, a_ab[c]) for c in cs_]
    tinv = [eye + a_d[c] for c in cs_]
    p = a_d
    for _ in range(int(math.log2(RW_SUB)) - 1):
        p = [_mm(p[c], p[c]) for c in cs_]
        tinv = [tinv[c] + _mm(tinv[c], p[c]) for c in cs_]
    nn = [_mm(tinv[c], a_o[c]) for c in cs_]
    levels = int(math.log2(C // RW_SUB))
    for lvl in range(levels):
        tinv = [tinv[c] + _mm(nn[c], tinv[c]) for c in cs_]
        if lvl + 1 < levels:
            nn = [_mm(nn[c], nn[c]) for c in cs_]

    br_s = [_mm_nt(br[c], s0[c]) for c in cs_]
    rhs = [br_s[c][:C] + _mm(a_bk[c], vh[c]) for c in cs_]
    u = [_mm(tinv[c], rhs[c]) for c in cs_]
    uv = [jnp.concatenate([u[c], vh[c]], axis=0) for c in cs_]
    y = [br_s[c][C:] + _mm(a_r[c], uv[c]) for c in cs_]
    for c in cs_:
        ak_end = jnp.concatenate([cut(at_end, c), cut(kt_end, c)], axis=0)
        g_end = jnp.exp(ends[chains[c][0]][:, chains[c][1] * N:(chains[c][1] + 1) * N])
        s_ref[c] = s0[c] * g_end + _mm(uv[c].T, ak_end)
    y = jnp.concatenate([jnp.concatenate(y[e * heads:(e + 1) * heads], axis=1) for e in range(nbat)],
                        axis=0)

    inv_n = 1.0 / N
    mean = head_sum(y) * inv_n
    yc = y - mean
    var = head_sum(yc * yc) * inv_n
    yn = yc * lax.rsqrt(var + RW_GN_EPS) * lnw_ref[...] + lnb_ref[...]
    bonus = head_sum(r * k * rk_ref[...]) * v
    y_ref[...] = ((yn + bonus) * gate).reshape(nbat, C, G)


def rwkv_direction(u, col0, n_ctx, vf_u, mu, w0, w1, w2, a0, a1, a2, g1, g2, k_k, k_a, r_k,
                   ln_w, ln_b, vmix, reverse):
    B, L, _ = u.shape
    G = w0.shape[-1]
    heads = G // RW_HEAD
    C = RW_CHUNK
    nbat = RW_BATCH if B % RW_BATCH == 0 else 1
    nb, ncb = L // C, n_ctx // C
    has_vmix = vmix is not None
    blk = lambda c: pl.BlockSpec((nbat, C, G), lambda b, n: (b, _scan_block(n, ncb, nb, reverse), c))
    full = lambda arr: pl.BlockSpec(arr.shape, lambda b, n: (0,) * arr.ndim)
    row = lambda t: t.reshape(1, G)
    hid = jnp.arange(RW_ONES) // RW_HEAD
    ones_half = (hid[:, None] == hid[None, :]).astype(BF16)
    ins = [u, u, u, u]
    specs = [blk(col0), blk(col0 + 1), blk(col0 + 2), blk(col0 + 3)]
    if has_vmix:
        ins.append(vf_u)
        specs.append(blk(col0 + 2))
    params = [mu, row(w0), w1.astype(BF16), w2.astype(BF16), row(a0), a1.astype(BF16), a2.astype(BF16),
              g1.astype(BF16), g2.astype(BF16), row(k_k), row(k_a), row(r_k), row(ln_w), row(ln_b)]
    if has_vmix:
        v0, v1, v2 = vmix
        params += [row(v0), v1.astype(BF16), v2.astype(BF16)]
    params.append(ones_half)
    ins += params
    specs += [full(p_) for p_ in params]
    return pl.pallas_call(
        functools.partial(_rwkv_kernel, heads=heads, n_ctx_blocks=ncb, reverse=reverse,
                          has_vmix=has_vmix),
        grid=(B // nbat, nb),
        in_specs=specs,
        out_specs=pl.BlockSpec((nbat, C, G), lambda b, n: (b, _scan_block(n, ncb, nb, reverse), 0)),
        out_shape=jax.ShapeDtypeStruct((B, L, G), F32),
        scratch_shapes=[pltpu.VMEM((nbat * heads, RW_HEAD, RW_HEAD), F32),
                        pltpu.VMEM((nbat * 4, C, G), F32)],
        compiler_params=_params("parallel", "arbitrary"),
        name="rwkv_mixer",
    )(*ins)


def _retention_kernel(*refs, heads, reverse, finish):
    it = iter(refs)
    q_ref, k_ref, v_ref = next(it), next(it), next(it)
    cos_ref, sin_ref, dm_ref, kdec_ref, qdec_ref, cdec_ref = (next(it) for _ in range(6))
    if finish:
        g_ref, o0_ref, gnw_ref = next(it), next(it), next(it)
    o_ref, s_ref = next(it), next(it)

    C, G = q_ref.shape[1], q_ref.shape[2]
    dh = G // heads
    quarter = dh // 4

    @pl.when(pl.program_id(1) == 0)
    def _():
        s_ref[...] = jnp.zeros_like(s_ref)

    lane = lax.broadcasted_iota(jnp.int32, (C, G), 1)
    first = (lane & quarter) == 0
    cos = cos_ref[...]
    sin = sin_ref[...]

    def rope(x):
        swapped = jnp.where(first, pltpu.roll(x, G - quarter, 1), pltpu.roll(x, quarter, 1))
        return x * cos + swapped * sin

    q = rope(q_ref[0])
    k = rope(k_ref[0]) * (dh ** -0.5)
    vb = v_ref[0].astype(BF16)
    qd = (q * qdec_ref[...]).astype(BF16)
    kd = (k * kdec_ref[...]).astype(BF16)
    qb = q.astype(BF16)
    kb = k.astype(BF16)
    cdec = cdec_ref[...]
    outs = []
    for h in range(heads):
        sl = slice(h * dh, (h + 1) * dh)
        s0 = s_ref[h]
        inner = lax.dot_general(qb[:, sl], kb[:, sl], NT_DIMS, preferred_element_type=F32) * dm_ref[h]
        out = jnp.dot(inner.astype(BF16), vb[:, sl], preferred_element_type=F32)
        out = out + jnp.dot(qd[:, sl], s0.astype(BF16), preferred_element_type=F32)
        kv = jnp.dot(kd[:, sl].T, vb[:, sl], preferred_element_type=F32)
        s_ref[h] = cdec[:, sl] * s0 + kv
        if finish:
            out = out + o0_ref[0, :, sl]
            m = jnp.mean(out, axis=-1, keepdims=True)
            var = jnp.mean(jnp.square(out - m), axis=-1, keepdims=True)
            out = (out - m) * lax.rsqrt(var + RT_GN_EPS)
        outs.append(out)
    o = jnp.concatenate(outs, axis=1)
    if finish:
        g = g_ref[0]
        o = o * gnw_ref[...] * (g * _sigmoid(g))
    o_ref[0] = o


def retention_direction(u, col0, n_ctx, cos_tab, sin_tab, lg, gn_w, prev_out, reverse):
    B, L, _ = u.shape
    G = gn_w.shape[-1]
    H = RT_HEADS
    dh = G // H
    C = RT_CHUNK
    nb, ncb = L // C, n_ctx // C
    finish = prev_out is not None
    idx = jnp.arange(C, dtype=F32)
    pos = (C - 1 - idx) if reverse else idx
    diff = pos[:, None] - pos[None, :]
    keep = diff > 0 if reverse else diff >= 0
    dmask = jnp.where(keep, jnp.exp(jnp.where(keep, diff, 0.0)[None] * lg[:, None, None]), 0.0)
    kdec = jnp.repeat(jnp.exp((C - 1 - pos)[:, None] * lg[None, :]), dh, axis=1)
    qdec = jnp.repeat(jnp.exp((pos + 1.0)[:, None] * lg[None, :]), dh, axis=1)
    cdec = jnp.repeat(jnp.exp(C * lg), dh)[None, :]
    tblk = lambda b, n: _scan_block(n, ncb, nb, reverse)
    blk = lambda c: pl.BlockSpec((1, C, G), lambda b, n: (b, tblk(b, n), c))
    tab = pl.BlockSpec((C, G), lambda b, n: (tblk(b, n), 0))
    ins = [u, u, u, cos_tab, sin_tab, dmask, kdec, qdec, cdec]
    specs = [blk(col0), blk(col0 + 1), blk(col0 + 2), tab, tab,
             pl.BlockSpec((H, C, C), lambda b, n: (0, 0, 0)),
             pl.BlockSpec((C, G), lambda b, n: (0, 0)),
             pl.BlockSpec((C, G), lambda b, n: (0, 0)),
             pl.BlockSpec((1, G), lambda b, n: (0, 0))]
    if finish:
        ins += [u, prev_out, gn_w.reshape(1, G)]
        specs += [blk(col0 + 3), blk(0), pl.BlockSpec((1, G), lambda b, n: (0, 0))]
    return pl.pallas_call(
        functools.partial(_retention_kernel, heads=H, reverse=reverse, finish=finish),
        grid=(B, nb),
        in_specs=specs,
        out_specs=blk(0),
        out_shape=jax.ShapeDtypeStruct((B, L, G), F32),
        scratch_shapes=[pltpu.VMEM((H, dh, dh), F32)],
        compiler_params=_params("parallel", "arbitrary"),
        name="retention_mixer",
    )(*ins)


def rope_tables(n_ctx, L, G):
    dh = G // RT_HEADS
    quarter = dh // 4
    n_rows = L // GRID_W
    rows = jnp.repeat(jnp.arange(n_rows, dtype=F32), GRID_W)
    cols = jnp.tile(jnp.arange(GRID_W, dtype=F32), n_rows)
    inv = jnp.power(ROPE_BASE, -jnp.arange(quarter, dtype=F32) / quarter)
    lane = jnp.arange(G)
    use_cols = (lane % dh) >= (dh // 2)
    pos = jnp.where(use_cols[None, :], cols[:, None], rows[:, None])
    ang = pos * inv[lane % quarter][None, :]
    sign = jnp.where((lane % (2 * quarter)) < quarter, -1.0, 1.0)[None, :]
    cos = jnp.concatenate([jnp.ones((n_ctx, G), F32), jnp.cos(ang)], axis=0)
    sin = jnp.concatenate([jnp.zeros((n_ctx, G), F32), jnp.sin(ang) * sign], axis=0)
    return cos, sin


def _moe_stream_kernel(te_ref, nt_ref, first_ref, nxt_ref, slot_ref,
                       x_ref, wgu_hbm, bgu_ref, wdn_hbm, bdn_ref, g_ref, o_ref,
                       wgu_bf, wdn_bf, stage, sem, done_ref, *, layer):
    i = pl.program_id(0)
    active = i < nt_ref[0]
    e, s, nxt = te_ref[i], slot_ref[i], nxt_ref[i]
    D, F = wgu_bf.shape[1], wdn_bf.shape[1]
    ch = stage.shape[1]
    n_gu = D // ch
    n_ch = n_gu + F // ch

    def copy(expert, c, gate_up):
        src = (wgu_hbm.at[layer, expert, pl.ds(pl.multiple_of(c * ch, ch), ch), :] if gate_up else
               wdn_hbm.at[layer, expert, pl.ds(pl.multiple_of((c - n_gu) * ch, ch), ch), :])
        return pltpu.make_async_copy(src, stage.at[c % 2], sem.at[c % 2])

    def start(expert, c):
        @pl.when(c < n_gu)
        def _():
            copy(expert, c, True).start()

        @pl.when((c >= n_gu) & (c < n_ch))
        def _():
            copy(expert, c, False).start()

    def land(expert, c, slot):
        pltpu.make_async_copy(wgu_hbm.at[layer, expert, pl.ds(0, ch), :], stage.at[c % 2],
                              sem.at[c % 2]).wait()
        val = stage[c % 2].astype(BF16)

        @pl.when(c < n_gu)
        def _():
            wgu_bf[slot, pl.ds(pl.multiple_of(c * ch, ch), ch), :] = val

        @pl.when(c >= n_gu)
        def _():
            wdn_bf[slot, pl.ds(pl.multiple_of((c - n_gu) * ch, ch), ch), :] = val

        start(expert, c + 2)

    @pl.when(active & (first_ref[i] == 1))
    def _():
        @pl.when(i == 0)
        def _():
            done_ref[0] = 0
            start(e, 0)
            start(e, 1)

        def body(c, carry):
            land(e, c, s)
            return carry

        lax.fori_loop(done_ref[0], n_ch, body, 0)
        done_ref[0] = 0

        @pl.when(nxt >= 0)
        def _():
            start(nxt, 0)
            start(nxt, 1)

    @pl.when(active)
    def _():
        gu = jnp.dot(x_ref[...], wgu_bf[s], preferred_element_type=F32) + bgu_ref[0]
        glu = jnp.minimum(gu[:, :F], SWIGLU_LIMIT)
        lin = jnp.clip(gu[:, F:], -SWIGLU_LIMIT, SWIGLU_LIMIT)
        act = glu * jax.nn.sigmoid(SWIGLU_ALPHA * glu) * (lin + 1.0)
        y = jnp.dot(act.astype(BF16), wdn_bf[s], preferred_element_type=F32) + bdn_ref[0]
        o_ref[...] = (y * g_ref[...]).astype(o_ref.dtype)

    @pl.when(jnp.logical_not(active))
    def _():
        o_ref[...] = jnp.zeros_like(o_ref)

    for _ in range(MOE_CHUNKS_PER_STEP):
        @pl.when(active & (nxt >= 0) & (done_ref[0] < n_ch))
        def _():
            c = done_ref[0]
            land(nxt, c, 1 - s)
            done_ref[0] = c + 1


def moe_grouped(xs, tile_expert, n_tiles_used, sizes, layer, w_gu, b_gu, w_dn, b_dn, row_gate):
    P, D = xs.shape
    _, E, _, F2 = w_gu.shape
    F = F2 // 2
    tm = MOE_TILE
    nt = P // tm
    tile = jnp.arange(nt, dtype=jnp.int32)
    prev = jnp.concatenate([jnp.full((1,), -1, jnp.int32), tile_expert[:-1]])
    first = ((tile_expert != prev) & (tile < n_tiles_used[0])).astype(jnp.int32)
    slot = (jnp.cumsum(first) - 1) % 2
    expert = jnp.arange(E, dtype=jnp.int32)
    later = jnp.where((sizes[None, :] > 0) & (expert[None, :] > expert[:, None]), expert[None, :], E)
    next_expert = jnp.min(later, axis=1)
    nxt = jnp.where(next_expert < E, next_expert, -1).astype(jnp.int32)[tile_expert]
    grid_spec = pltpu.PrefetchScalarGridSpec(
        num_scalar_prefetch=5,
        grid=(nt,),
        in_specs=[
            pl.BlockSpec((tm, D), lambda i, *_: (i, 0)),
            pl.BlockSpec(memory_space=pl.ANY),
            pl.BlockSpec((1, 1, F2), lambda i, te, *_: (te[i], 0, 0)),
            pl.BlockSpec(memory_space=pl.ANY),
            pl.BlockSpec((1, 1, D), lambda i, te, *_: (te[i], 0, 0)),
            pl.BlockSpec((tm, 1), lambda i, *_: (i, 0)),
        ],
        out_specs=pl.BlockSpec((tm, D), lambda i, *_: (i, 0)),
        scratch_shapes=[pltpu.VMEM((2, D, F2), BF16), pltpu.VMEM((2, F, D), BF16),
                        pltpu.VMEM((2, MOE_CHUNK_ROWS, D), F32), pltpu.SemaphoreType.DMA((2,)),
                        pltpu.SMEM((1,), jnp.int32)],
    )
    return pl.pallas_call(
        functools.partial(_moe_stream_kernel, layer=layer),
        grid_spec=grid_spec,
        out_shape=jax.ShapeDtypeStruct((P, D), BF16),
        compiler_params=_params("arbitrary"),
        name="moe_grouped",
    )(tile_expert, n_tiles_used, first, nxt, slot.astype(jnp.int32),
      xs, w_gu, b_gu.reshape(E, 1, F2), w_dn, b_dn.reshape(E, 1, D), row_gate)


def _router_kernel(x_ref, nw_ref, sh_ref, sc_ref, rwh_ref, rwl_ref, rb_ref,
                   h_ref, idx_ref, p_ref, rank_ref, cnt_ref, carry_ref):
    @pl.when((pl.program_id(0) == 0) & (pl.program_id(1) == 0))
    def _():
        carry_ref[...] = jnp.zeros_like(carry_ref)

    x = x_ref[0]
    y = x * lax.rsqrt(jnp.mean(x * x, axis=-1, keepdims=True) + RMS_EPS)
    y = y * nw_ref[...] * (1.0 + sc_ref[0]) + sh_ref[0]
    y_hi = y.astype(BF16)
    h_ref[0] = y_hi
    y_lo = (y - y_hi.astype(F32)).astype(BF16)
    logits = (jnp.dot(y_hi, rwh_ref[...], preferred_element_type=F32)
              + jnp.dot(y_lo, rwh_ref[...], preferred_element_type=F32)
              + jnp.dot(y_hi, rwl_ref[...], preferred_element_type=F32)) + rb_ref[...]
    E = logits.shape[-1]
    lane = lax.broadcasted_iota(jnp.int32, logits.shape, 1)
    vals, ids = [], []
    for _ in range(TOP_K):
        m = jnp.max(logits, axis=-1, keepdims=True)
        i = jnp.min(jnp.where(logits == m, lane, E), axis=-1, keepdims=True)
        vals.append(m)
        ids.append(i)
        logits = jnp.where(lane == i, -jnp.inf, logits)
    e = jnp.exp(jnp.concatenate(vals, axis=1) - vals[0])
    p_ref[0] = e / jnp.sum(e, axis=-1, keepdims=True)
    idx_ref[0] = jnp.concatenate(ids, axis=1)

    tm = x.shape[0]
    chosen = [lane == i for i in ids]
    onehot = sum(c.astype(F32) for c in chosen)
    earlier = (lax.broadcasted_iota(jnp.int32, (tm, tm), 1)
               < lax.broadcasted_iota(jnp.int32, (tm, tm), 0)).astype(BF16)
    prefix = jnp.dot(earlier, onehot.astype(BF16), preferred_element_type=F32) + carry_ref[...]
    rank_ref[0] = jnp.concatenate(
        [jnp.sum(jnp.where(c, prefix, 0.0), axis=-1, keepdims=True) for c in chosen], axis=1).astype(jnp.int32)
    total = carry_ref[...] + jnp.sum(onehot, axis=0, keepdims=True)
    carry_ref[...] = total
    cnt_ref[...] = total


def moe_router(xs, row0, n_rows, n_ctx, nw, shift, scale, router_w, router_b):
    B, _, D = xs.shape
    E = router_w.shape[1]
    tm = ROW_TILE
    ncb, b0 = n_ctx // tm, row0 // tm
    mod_idx = lambda b, i: (2 * b + ((i + b0) >= ncb).astype(jnp.int32), 0, 0)
    rw_hi = router_w.astype(BF16)
    rw_lo = (router_w - rw_hi.astype(F32)).astype(BF16)
    full = lambda shape: pl.BlockSpec(shape, lambda b, i: (0,) * len(shape))
    out = lambda w: pl.BlockSpec((1, tm, w), lambda b, i: (b, i, 0))
    return pl.pallas_call(
        _router_kernel,
        grid=(B, n_rows // tm),
        in_specs=[pl.BlockSpec((1, tm, D), lambda b, i: (b, i + b0, 0)), full((1, D)),
                  pl.BlockSpec((1, 1, D), mod_idx), pl.BlockSpec((1, 1, D), mod_idx),
                  full((D, E)), full((D, E)), full((1, E))],
        out_specs=[out(D), out(TOP_K), out(TOP_K), out(TOP_K), full((1, E))],
        out_shape=[jax.ShapeDtypeStruct((B, n_rows, D), BF16),
                   jax.ShapeDtypeStruct((B, n_rows, TOP_K), jnp.int32),
                   jax.ShapeDtypeStruct((B, n_rows, TOP_K), F32),
                   jax.ShapeDtypeStruct((B, n_rows, TOP_K), jnp.int32),
                   jax.ShapeDtypeStruct((1, E), F32)],
        scratch_shapes=[pltpu.VMEM((1, E), F32)],
        compiler_params=_params("arbitrary", "arbitrary"),
        name="moe_router",
    )(xs, nw.reshape(1, D), shift.reshape(2 * B, 1, D), scale.reshape(2 * B, 1, D),
      rw_hi, rw_lo, router_b.reshape(1, E))


def _cast_kernel(x_ref, o_ref):
    o_ref[...] = x_ref[...].astype(BF16)


def cast_bf16(w):
    C = w.shape[-1]
    R = w.size // C
    tr = 8
    while tr * 2 * C <= CAST_BLOCK_ELEMS and R % (tr * 2) == 0:
        tr *= 2
    spec = pl.BlockSpec((tr, C), lambda i: (i, 0))
    return pl.pallas_call(
        _cast_kernel,
        grid=(R // tr,),
        in_specs=[spec],
        out_specs=spec,
        out_shape=jax.ShapeDtypeStruct((R, C), BF16),
        compiler_params=_params("parallel"),
        name="cast_bf16",
    )(w.reshape(R, C)).reshape(w.shape)


def _combine_kernel(g_ref, res_ref, gate_ref, fw_ref, o_ref, *, final_norm):
    f = g_ref[0, 0].astype(F32)
    for k in range(1, g_ref.shape[0]):
        f = f + g_ref[k, 0].astype(F32)
    y = res_ref[0] + gate_ref[0] * f
    if final_norm:
        y = y * lax.rsqrt(jnp.mean(y * y, axis=-1, keepdims=True) + RMS_EPS) * fw_ref[...]
    o_ref[0] = y


def moe_combine(g, xs, row0, n_ctx, gate, final_w):
    K, B, n_rows, D = g.shape
    tm = ROW_TILE
    ncb, b0 = n_ctx // tm, row0 // tm
    g_idx = lambda b, i: (2 * b + ((i + b0) >= ncb).astype(jnp.int32), 0, 0)
    final_norm = final_w is not None
    fw = final_w if final_norm else jnp.ones((D,), F32)
    return pl.pallas_call(
        functools.partial(_combine_kernel, final_norm=final_norm),
        grid=(B, n_rows // tm),
        in_specs=[pl.BlockSpec((K, 1, tm, D), lambda b, i: (0, b, i, 0)),
                  pl.BlockSpec((1, tm, D), lambda b, i: (b, i + b0, 0)),
                  pl.BlockSpec((1, 1, D), g_idx),
                  pl.BlockSpec((1, D), lambda b, i: (0, 0))],
        out_specs=pl.BlockSpec((1, tm, D), lambda b, i: (b, i, 0)),
        out_shape=jax.ShapeDtypeStruct((B, n_rows, D), F32),
        compiler_params=_params("parallel", "parallel"),
        name="moe_combine",
    )(g, xs, gate.reshape(2 * B, 1, D), fw.reshape(1, D))


def moe(t, idx, probs, rank, counts, layer, w_gu, b_gu, w_dn, b_dn):
    T, D = t.shape
    E = N_EXPERTS
    tm = MOE_TILE
    sizes = counts.astype(jnp.int32)
    padded = ((sizes + tm - 1) // tm) * tm
    ends = jnp.cumsum(padded)
    starts = ends - padded
    dest = starts[idx] + rank
    n_rows = T * TOP_K + E * tm
    tile_start = jnp.arange(n_rows // tm, dtype=jnp.int32) * tm
    tile_expert = jnp.minimum(jnp.sum((ends[None, :] <= tile_start[:, None]).astype(jnp.int32), axis=1),
                              E - 1)
    order = jnp.argsort(idx.reshape(-1), stable=True).astype(jnp.int32)
    row_expert = jnp.broadcast_to(tile_expert[:, None], (n_rows // tm, tm)).reshape(-1)
    within = jnp.arange(n_rows, dtype=jnp.int32) - starts[row_expert]
    valid = within < sizes[row_expert]
    src = order[jnp.clip((jnp.cumsum(sizes) - sizes)[row_expert] + within, 0, T * TOP_K - 1)]
    row_token = jnp.where(valid, src // TOP_K, jnp.arange(n_rows, dtype=jnp.int32) % T)
    row_gate = jnp.where(valid, probs.reshape(-1)[src], 0.0)
    xs = t[row_token]
    n_used = (ends[-1] // tm).astype(jnp.int32).reshape(1)
    ys = moe_grouped(xs, tile_expert, n_used, sizes, layer, w_gu, b_gu, w_dn, b_dn,
                     row_gate.reshape(n_rows, 1))
    return ys[dest.T]


HY_COLS = 256
HY_ROWS = 512


def _twiddle_kernel(c1_ref, s1_ref, c2_ref, s2_ref, cs_ref, ss_ref):
    c1, s1 = c1_ref[0], s1_ref[0]
    c2, s2 = c2_ref[...], s2_ref[...]
    cs_ref[...] = (c1 * c2 - s1 * s2).astype(BF16)
    ss_ref[...] = (s1 * c2 + c1 * s2).astype(BF16)


def dft_matrices(L):
    R = min(64, L)
    n_hi = L // R
    period = 8 * L
    a = 2 * jnp.arange(L, dtype=jnp.int32) + 1
    m1 = (a[None, :] * (2 * R * jnp.arange(n_hi, dtype=jnp.int32))[:, None]) % period
    m2 = (a[None, :] * (2 * jnp.arange(R, dtype=jnp.int32) + 1)[:, None]) % period
    ang = lambda m: (m.astype(F32) - jnp.where(m >= period // 2, period, 0).astype(F32)) * (math.pi / (4 * L))
    c1, s1 = jnp.cos(ang(m1)).reshape(n_hi, 1, L), jnp.sin(ang(m1)).reshape(n_hi, 1, L)
    c2, s2 = jnp.cos(ang(m2)), jnp.sin(ang(m2))
    hi = pl.BlockSpec((1, 1, L), lambda i: (i, 0, 0))
    lo = pl.BlockSpec((R, L), lambda i: (0, 0))
    out = pl.BlockSpec((R, L), lambda i: (i, 0))
    return pl.pallas_call(
        _twiddle_kernel,
        grid=(n_hi,),
        in_specs=[hi, hi, lo, lo],
        out_specs=[out, out],
        out_shape=[jax.ShapeDtypeStruct((L, L), BF16)] * 2,
        compiler_params=_params("parallel"),
        name="dft_matrices",
    )(c1, s1, c2, s2)


def _dft_pair_kernel(cs_ref, ss_ref, x_ref, oc_ref, os_ref):
    x = x_ref[...]
    oc_ref[...] = jnp.dot(cs_ref[...], x, preferred_element_type=F32)
    os_ref[...] = jnp.dot(ss_ref[...], x, preferred_element_type=F32)


def dft_pair(cs, ss, x_bf16):
    L, M = x_bf16.shape
    tm = min(HY_ROWS, L)
    tn = min(512, M)
    mat = pl.BlockSpec((tm, L), lambda j, i: (i, 0))
    out = pl.BlockSpec((tm, tn), lambda j, i: (i, j))
    return pl.pallas_call(
        _dft_pair_kernel,
        grid=(M // tn, L // tm),
        in_specs=[mat, mat, pl.BlockSpec((L, tn), lambda j, i: (0, j))],
        out_specs=[out, out],
        out_shape=[jax.ShapeDtypeStruct((L, M), F32)] * 2,
        compiler_params=_params("parallel", "arbitrary"),
        name="dft_pair",
    )(cs, ss, x_bf16)


def _hyena_conv_kernel(z_ref, gate_ref, cs_ref, ss_ref, hre_ref, him_ref, bias_ref, o_ref,
                       zb_ref, yre_ref, yim_ref):
    phase = pl.program_id(1)
    m = pl.program_id(2)
    tm = cs_ref.shape[0]
    L = z_ref.shape[1]
    rows = pl.ds(pl.multiple_of(m * tm, tm), tm)

    @pl.when((phase == 0) & (m == 0))
    def _():
        zb_ref[...] = z_ref[0].astype(BF16)

    @pl.when(phase == 0)
    def _():
        xc = jnp.dot(cs_ref[...], zb_ref[...], preferred_element_type=F32)
        xs = jnp.dot(ss_ref[...], zb_ref[...], preferred_element_type=F32)
        hre, him = hre_ref[...], him_ref[...]
        yre_ref[rows, :] = (xc * hre + xs * him).astype(BF16)
        yim_ref[rows, :] = (xc * him - xs * hre).astype(BF16)

    @pl.when(phase == 1)
    def _():
        y = (jnp.dot(cs_ref[...], yre_ref[...], preferred_element_type=F32)
             - jnp.dot(ss_ref[...], yim_ref[...], preferred_element_type=F32)) * (1.0 / L)
        o_ref[0] = gate_ref[0] * (y + bias_ref[...] * z_ref[0, rows, :])


def hyena_conv(z_src, z_col, gate_src, gate_col, cs, ss, hre, him, bias):
    B, L, _ = z_src.shape
    G = bias.shape[0]
    tc = HY_COLS
    tm = min(HY_ROWS, L)
    ncb = G // tc
    nm = L // tm
    pin = lambda phase, m, keep: jnp.where(phase == keep, m, (nm - 1) * (1 - keep))
    return pl.pallas_call(
        _hyena_conv_kernel,
        grid=(B * ncb, 2, nm),
        in_specs=[
            pl.BlockSpec((1, L, tc), lambda i, p, m: (i // ncb, 0, z_col * ncb + i % ncb)),
            pl.BlockSpec((1, tm, tc), lambda i, p, m: (i // ncb, pin(p, m, 1), gate_col * ncb + i % ncb)),
            pl.BlockSpec((tm, L), lambda i, p, m: (m, 0)),
            pl.BlockSpec((tm, L), lambda i, p, m: (m, 0)),
            pl.BlockSpec((tm, tc), lambda i, p, m: (pin(p, m, 0), i % ncb)),
            pl.BlockSpec((tm, tc), lambda i, p, m: (pin(p, m, 0), i % ncb)),
            pl.BlockSpec((1, tc), lambda i, p, m: (0, i % ncb)),
        ],
        out_specs=pl.BlockSpec((1, tm, tc), lambda i, p, m: (i // ncb, pin(p, m, 1), i % ncb)),
        out_shape=jax.ShapeDtypeStruct((B, L, G), F32),
        scratch_shapes=[pltpu.VMEM((L, tc), BF16), pltpu.VMEM((L, tc), BF16), pltpu.VMEM((L, tc), BF16)],
        compiler_params=_params("parallel", "arbitrary", "arbitrary"),
        name="hyena_conv",
    )(z_src, gate_src, cs, ss, hre, him, bias.reshape(1, G))


def hyena_filter_spectrum(L, G, cs, ss, w1, b1, w2, b2, w3, b3, w4, freq):
    t = jnp.linspace(0.0, 1.0, L, dtype=F32)[:, None]
    bands = (HY_EMB - 1) // 2
    fr = jnp.linspace(1e-4, bands - 1, bands, dtype=F32)
    ang = (2.0 * math.pi / L) * jnp.arange(L, dtype=F32)[:, None] * fr[None, :]
    z = jnp.concatenate([t, jnp.cos(ang), -jnp.sin(ang)], axis=-1)
    h = jnp.sin(freq * (z @ w1 + b1))
    h = jnp.sin(freq * (h @ w2 + b2))
    h = jnp.sin(freq * (h @ w3 + b3))
    h = (h @ w4).reshape(L, 2, HY_ORDER * G)
    deltas = jnp.abs(jnp.linspace(HY_MIN_DECAY, HY_MAX_DECAY, G, dtype=F32))
    h = h * jnp.tile(jnp.exp(-t * deltas), (1, HY_ORDER))[:, None, :]
    fwd = h[:, 0]
    bwd = jnp.concatenate([h[1:, 1], jnp.zeros((1, HY_ORDER * G), F32)], axis=0)
    norm = jnp.sum(jnp.abs(fwd), axis=0) + jnp.sum(jnp.abs(bwd), axis=0)
    xc, xs = dft_pair(cs, ss, jnp.concatenate([fwd, bwd], axis=1).astype(BF16))
    M = HY_ORDER * G
    a = xc[:, :M] + xc[:, M:]
    b = xs[:, M:] - xs[:, :M]
    ph = (math.pi / (2 * L)) * (jnp.arange(L, dtype=F32)[:, None] + 0.5)
    hre = (jnp.cos(ph) * a - jnp.sin(ph) * b) / norm
    him = (jnp.sin(ph) * a + jnp.cos(ph) * b) / norm
    split = lambda t_: jnp.moveaxis(t_.reshape(L, HY_ORDER, G), 1, 0)
    return split(hre), split(him)


def hyena_mixer(u, cs, ss, short_w, short_b, fw1, fb1, fw2, fb2, fw3, fb3, fw4, ffreq, bias):
    L = u.shape[1]
    G = u.shape[2] // 3
    pad = jnp.pad(u, ((0, 0), (1, 1), (0, 0)))
    us = short_w[0] * pad[:, :L] + short_w[1] * u + short_w[2] * pad[:, 2:] + short_b
    hre, him = hyena_filter_spectrum(L, G, cs, ss, fw1, fb1, fw2, fb2, fw3, fb3, fw4, ffreq)
    z = hyena_conv(us, 0, us, 1, cs, ss, hre[0], him[0], bias[0])
    return hyena_conv(z, 0, us, 2, cs, ss, hre[1], him[1], bias[1])


def rms_norm(x, w):
    y = x * lax.rsqrt(jnp.mean(x * x, axis=-1, keepdims=True) + RMS_EPS)
    return y * w


def kernel(x, c, ctx, c_ctx, ada_w, ada_b, norm1_w, norm2_w, w_in, w_out, hy_short_w, hy_short_b, hy_f_w1, hy_f_b1, hy_f_w2, hy_f_b2, hy_f_w3, hy_f_b3, hy_f_w4, hy_f_freq, hy_bias, rg_conv_w, rg_conv_b, rg_wa, rg_ba, rg_wx, rg_bx, rg_lambda, rw_mu, rw_w0, rw_w1, rw_w2, rw_a0, rw_a1, rw_a2, rw_g1, rw_g2, rw_k_k, rw_k_a, rw_r_k, rw_ln_w, rw_ln_b, rw_v0, rw_v1, rw_v2, rt_decay, rt_gn_w, moe_router_w, moe_router_b, moe_w_gu, moe_b_gu, moe_w_dn, moe_b_dn, final_norm_w):
    depth = ada_w.shape[0]
    B, L, D = x.shape
    n_ctx = ctx.shape[1]
    G = D // N_GROUPS
    cond = jnp.concatenate([jnp.broadcast_to(jax.nn.silu(c_ctx.astype(F32))[None, :], (B, D)),
                            jax.nn.silu(c.astype(F32))], axis=0)
    xs = jnp.concatenate([ctx.astype(x.dtype), x], axis=1)
    cos_tab, sin_tab = rope_tables(n_ctx, L, G)
    dft_lat = dft_matrices(L)
    dft_ctx = dft_matrices(n_ctx)
    w_in_b, w_out_b = cast_bf16(w_in), cast_bf16(w_out)
    u_first = None

    for l in range(depth):
        last = l == depth - 1
        mod = cond @ ada_w[l].astype(F32) + ada_b[l]
        mod = jnp.stack([mod[:B], mod[B:]], axis=1)
        mods = jnp.split(mod, N_MOD, axis=-1)

        u = modulated_matmul(xs, norm1_w[l], mods[0], mods[1], w_in_b, l, n_ctx, tn=w_in.shape[2] // 2)
        if l == 0:
            u_first = u

        hy_p = (hy_short_w[l], hy_short_b[l], hy_f_w1[l], hy_f_b1[l], hy_f_w2[l], hy_f_b2[l],
                hy_f_w3[l], hy_f_b3[l], hy_f_w4[l], hy_f_freq[l], hy_bias[l])
        hy_l = hyena_mixer(u[:, n_ctx:, : 3 * G], *dft_lat, *hy_p)
        if last:
            hy_c = jnp.zeros((B, n_ctx, G), F32)
        else:
            hy_c = hyena_mixer(u[:, :n_ctx, : 3 * G], *dft_ctx, *hy_p)
        hy = jnp.concatenate([hy_c, hy_l], axis=1)

        rg = [rglru_direction(u, 3, n_ctx, rg_conv_w[l, d], rg_conv_b[l, d], rg_wa[l, d], rg_ba[l, d],
                              rg_wx[l, d], rg_bx[l, d], rg_lambda[l, d], reverse=(d == 1))
              for d in range(2)]
        vmix = None if l == 0 else (rw_v0[l - 1], rw_v1[l - 1], rw_v2[l - 1])
        rw = [rwkv_direction(u, 5, n_ctx, u_first, rw_mu[l, d], rw_w0[l, d], rw_w1[l, d], rw_w2[l, d],
                             rw_a0[l, d], rw_a1[l, d], rw_a2[l, d], rw_g1[l], rw_g2[l], rw_k_k[l],
                             rw_k_a[l], rw_r_k[l], rw_ln_w[l], rw_ln_b[l], vmix, reverse=(d == 1))
              for d in range(2)]
        lg = -jax.nn.softplus(rt_decay[l].astype(F32))
        rt0 = retention_direction(u, 9, n_ctx, cos_tab, sin_tab, lg[0], rt_gn_w[l], None, reverse=False)
        rt = retention_direction(u, 9, n_ctx, cos_tab, sin_tab, lg[1], rt_gn_w[l], rt0, reverse=True)

        xs = gated_out_proj((hy, rg[0], rg[1], rw[0], rw[1], rt), w_out_b, l, xs, mods[2], n_ctx)

        row0 = n_ctx if last else 0
        n_rows = xs.shape[1] - row0
        h, idx, probs, rank, counts = moe_router(xs, row0, n_rows, n_ctx, norm2_w[l], mods[3], mods[4],
                                                 moe_router_w[l], moe_router_b[l])
        flat = lambda t_: t_.reshape(-1, TOP_K)
        g = moe(h.reshape(-1, D), flat(idx), flat(probs), flat(rank), counts.reshape(-1), l,
                moe_w_gu, moe_b_gu[l], moe_w_dn, moe_b_dn[l])
        xs = moe_combine(g.reshape(TOP_K, B, n_rows, D), xs, row0, n_ctx, mods[5],
                         final_norm_w if last else None)
    return xs
```

```python
import functools
import math

import jax
import jax.numpy as jnp
from jax import lax
from jax.experimental import pallas as pl
from jax.experimental.pallas import tpu as pltpu

F32 = jnp.float32
BF16 = jnp.bfloat16
HIGHEST = lax.Precision.HIGHEST

GRID_W = 64
N_GROUPS = 4
N_MOD = 6
RMS_EPS = 1e-6

HY_ORDER = 2
HY_EMB = 33
HY_TARGET = 1e-2
HY_MIN_DECAY = math.log(HY_TARGET) / 1.5
HY_MAX_DECAY = math.log(HY_TARGET) / 0.3

RG_HEADS = 8
RG_CONV = 4
RG_C = 8.0

RW_HEAD = 64
RW_GN_EPS = 64e-5
RW_CHUNK = 64
RW_SUB = 16
RW_BATCH = 4
RW_ONES = 256

RT_HEADS = 4
RT_CHUNK = 128
RT_GN_EPS = 1e-6
ROPE_BASE = 10000.0

N_EXPERTS = 32
TOP_K = 4
SWIGLU_LIMIT = 7.0
SWIGLU_ALPHA = 1.702

VMEM_LIMIT_BYTES = 52 * 1024 * 1024
LRU_CHUNK = 256
MOE_TILE = 256
MOE_CHUNK_ROWS = 512
MOE_CHUNKS_PER_STEP = 2
ROW_TILE = 256
CAST_BLOCK_ELEMS = 2 * 1024 * 1024

NT_DIMS = (((1,), (1,)), ((), ()))


def _params(*sem):
    return pltpu.CompilerParams(dimension_semantics=sem, vmem_limit_bytes=VMEM_LIMIT_BYTES)


def _scan_block(n, n_ctx_blocks, n_blocks, reverse):
    if not reverse:
        return n
    return jnp.where(n < n_ctx_blocks, n_ctx_blocks - 1 - n, n_blocks - 1 + n_ctx_blocks - n)


def _shift_rows(cur, prv, j, reverse):
    T = cur.shape[0]
    row = lax.broadcasted_iota(jnp.int32, cur.shape, 0)
    if not reverse:
        return jnp.where(row < j, pltpu.roll(prv, j, 0), pltpu.roll(cur, j, 0))
    return jnp.where(row >= T - j, pltpu.roll(prv, T - j, 0), pltpu.roll(cur, T - j, 0))


def _mm(a, b):
    return jnp.dot(a.astype(BF16), b.astype(BF16), preferred_element_type=F32)


def _mm_nt(a, b):
    return lax.dot_general(a.astype(BF16), b.astype(BF16), NT_DIMS, preferred_element_type=F32)


def _split_dot(x, m_bf16, parts):
    acc = None
    for _ in range(parts):
        hi = x.astype(BF16)
        d = jnp.dot(hi, m_bf16, preferred_element_type=F32)
        acc = d if acc is None else acc + d
        x = x - hi.astype(F32)
    return acc


def _expm1(x):
    small = x * (1.0 + x * (0.5 + x * (1.0 / 6.0 + x * (1.0 / 24.0 + x * (1.0 / 120.0)))))
    return jnp.where(jnp.abs(x) < 0.1, small, jnp.exp(x) - 1.0)


def _softplus(x):
    return jnp.maximum(x, 0.0) + jnp.log(1.0 + jnp.exp(-jnp.abs(x)))


def _sigmoid(x):
    return 1.0 / (1.0 + jnp.exp(-x))


def _modmm_kernel(x_ref, nw_ref, sh_ref, sc_ref, w_ref, o_ref):
    x = x_ref[0]
    y = x * lax.rsqrt(jnp.mean(x * x, axis=-1, keepdims=True) + RMS_EPS)
    y = y * nw_ref[...] * (1.0 + sc_ref[0]) + sh_ref[0]
    o_ref[0] = jnp.dot(y.astype(BF16), w_ref[...], preferred_element_type=F32)


def modulated_matmul(x, nw, shift, scale, w_bf16, layer, n_ctx, tn):
    B, L, D = x.shape
    N = w_bf16.shape[2]
    tm = ROW_TILE
    ncb = n_ctx // tm
    mod_idx = lambda j, b, i: (2 * b + (i >= ncb).astype(jnp.int32), 0, 0)
    return pl.pallas_call(
        _modmm_kernel,
        grid=(N // tn, B, L // tm),
        in_specs=[
            pl.BlockSpec((1, tm, D), lambda j, b, i: (b, i, 0)),
            pl.BlockSpec((1, D), lambda j, b, i: (0, 0)),
            pl.BlockSpec((1, 1, D), mod_idx),
            pl.BlockSpec((1, 1, D), mod_idx),
            pl.BlockSpec((None, D, tn), lambda j, b, i: (layer, 0, j)),
        ],
        out_specs=pl.BlockSpec((1, tm, tn), lambda j, b, i: (b, i, j)),
        out_shape=jax.ShapeDtypeStruct((B, L, N), F32),
        compiler_params=_params("parallel", "parallel", "parallel"),
        name="modulated_matmul",
    )(x, nw.reshape(1, D), shift.reshape(2 * B, 1, D), scale.reshape(2 * B, 1, D), w_bf16)


def _outproj_kernel(hy_ref, rg0_ref, rg1_ref, rw0_ref, rw1_ref, rt_ref, w_ref, res_ref, g_ref, o_ref):
    G = hy_ref.shape[2]
    slabs = (hy_ref[0], rg0_ref[0] + rg1_ref[0], rw0_ref[0] + rw1_ref[0], rt_ref[0])
    acc = None
    for i, s in enumerate(slabs):
        d = jnp.dot(s.astype(BF16), w_ref[i * G:(i + 1) * G, :], preferred_element_type=F32)
        acc = d if acc is None else acc + d
    o_ref[0] = res_ref[0] + g_ref[0] * acc


def gated_out_proj(slabs, w_bf16, layer, res, gate, n_ctx):
    B, L, D = res.shape
    G = slabs[0].shape[2]
    tm = ROW_TILE
    ncb = n_ctx // tm
    g_idx = lambda b, i: (2 * b + (i >= ncb).astype(jnp.int32), 0, 0)
    slab_spec = pl.BlockSpec((1, tm, G), lambda b, i: (b, i, 0))
    return pl.pallas_call(
        _outproj_kernel,
        grid=(B, L // tm),
        in_specs=[slab_spec] * 6 + [
            pl.BlockSpec((None, N_GROUPS * G, D), lambda b, i: (layer, 0, 0)),
            pl.BlockSpec((1, tm, D), lambda b, i: (b, i, 0)),
            pl.BlockSpec((1, 1, D), g_idx),
        ],
        out_specs=pl.BlockSpec((1, tm, D), lambda b, i: (b, i, 0)),
        out_shape=jax.ShapeDtypeStruct((B, L, D), F32),
        compiler_params=_params("parallel", "parallel"),
        name="gated_out_proj",
    )(*slabs, w_bf16, res, gate.reshape(2 * B, 1, D))


def _rglru_kernel(x_ref, gate_ref, cw_ref, cb_ref, wa_ref, ba_ref, wx_ref, bx_ref, lam_ref,
                  o_ref, prev_ref, carry_ref, *, n_ctx_blocks, reverse):
    T = x_ref.shape[1]
    n = pl.program_id(1)

    @pl.when(n == 0)
    def _():
        carry_ref[...] = jnp.zeros_like(carry_ref)

    @pl.when((n == 0) | (n == n_ctx_blocks))
    def _():
        prev_ref[...] = jnp.zeros_like(prev_ref)

    x = x_ref[0]
    prv = prev_ref[...]
    xc = cb_ref[...] + cw_ref[RG_CONV - 1:RG_CONV, :] * x
    for j in range(1, RG_CONV):
        xc = xc + cw_ref[RG_CONV - 1 - j:RG_CONV - j, :] * _shift_rows(x, prv, j, reverse)
    prev_ref[...] = x

    xb = xc.astype(BF16)
    r = _sigmoid(jnp.dot(xb, wa_ref[...], preferred_element_type=F32) + ba_ref[...])
    i = _sigmoid(jnp.dot(xb, wx_ref[...], preferred_element_type=F32) + bx_ref[...])
    log_a = -RG_C * r * _softplus(-lam_ref[...])
    a = jnp.exp(log_a)
    b = jnp.sqrt(-_expm1(2.0 * log_a)) * (i * xc)

    row = lax.broadcasted_iota(jnp.int32, a.shape, 0)
    s = 1
    while s < T:
        if not reverse:
            m = row >= s
            b = jnp.where(m, a * pltpu.roll(b, s, 0) + b, b)
            a = jnp.where(m, a * pltpu.roll(a, s, 0), a)
        else:
            m = row < T - s
            b = jnp.where(m, a * pltpu.roll(b, T - s, 0) + b, b)
            a = jnp.where(m, a * pltpu.roll(a, T - s, 0), a)
        s *= 2
    h = b + a * carry_ref[...]
    carry_ref[...] = h[0:1, :] if reverse else h[T - 1:T, :]
    o_ref[0] = h * jax.nn.gelu(gate_ref[0])


def _block_diag(w):
    H, di, dj = w.shape
    eye = jnp.eye(H, dtype=w.dtype)
    return (eye[:, None, :, None] * w[:, :, None, :]).reshape(H * di, H * dj)


def rglru_direction(u, col0, n_ctx, conv_w, conv_b, wa, ba, wx, bx, lam, reverse):
    B, L, _ = u.shape
    G = conv_w.shape[-1]
    T = LRU_CHUNK
    nb, ncb = L // T, n_ctx // T
    blk = lambda c: pl.BlockSpec((1, T, G), lambda b, n: (b, _scan_block(n, ncb, nb, reverse), c))
    row = pl.BlockSpec((1, G), lambda b, n: (0, 0))
    mat = pl.BlockSpec((G, G), lambda b, n: (0, 0))
    return pl.pallas_call(
        functools.partial(_rglru_kernel, n_ctx_blocks=ncb, reverse=reverse),
        grid=(B, nb),
        in_specs=[blk(col0), blk(col0 + 1), pl.BlockSpec((RG_CONV, G), lambda b, n: (0, 0)),
                  row, mat, row, mat, row, row],
        out_specs=pl.BlockSpec((1, T, G), lambda b, n: (b, _scan_block(n, ncb, nb, reverse), 0)),
        out_shape=jax.ShapeDtypeStruct((B, L, G), F32),
        scratch_shapes=[pltpu.VMEM((T, G), F32), pltpu.VMEM((1, G), F32)],
        compiler_params=_params("parallel", "arbitrary"),
        name="rglru_mixer",
    )(u, u, conv_w, conv_b.reshape(1, G), _block_diag(wa).astype(BF16), ba.reshape(1, G),
      _block_diag(wx).astype(BF16), bx.reshape(1, G), lam.reshape(1, G))


def _rwkv_kernel(*refs, heads, n_ctx_blocks, reverse, has_vmix):
    it = iter(refs)
    r_ref, k_ref, v_ref, z_ref = next(it), next(it), next(it), next(it)
    vf_ref = next(it) if has_vmix else None
    mu_ref, w0_ref, w1_ref, w2_ref, a0_ref, a1_ref, a2_ref = (next(it) for _ in range(7))
    g1_ref, g2_ref, kk_ref, ka_ref, rk_ref, lnw_ref, lnb_ref = (next(it) for _ in range(7))
    if has_vmix:
        v0_ref, v1_ref, v2_ref = next(it), next(it), next(it)
    ones_ref = next(it)
    y_ref, s_ref, prev_ref = next(it), next(it), next(it)

    nbat, C, G = r_ref.shape
    R = nbat * C
    N = RW_HEAD
    n = pl.program_id(1)

    @pl.when(n == 0)
    def _():
        s_ref[...] = jnp.zeros_like(s_ref)

    @pl.when((n == 0) | (n == n_ctx_blocks))
    def _():
        prev_ref[...] = jnp.zeros_like(prev_ref)

    ones_half = ones_ref[...]

    def head_sum(x):
        W = ones_half.shape[0]
        return jnp.concatenate([_split_dot(x[:, i * W:(i + 1) * W], ones_half, 2) for i in range(G // W)],
                               axis=1)

    raw = [ref[...].reshape(R, G) for ref in (r_ref, k_ref, v_ref, z_ref)]
    z_raw = raw[3]
    if has_vmix:
        vm = _sigmoid(v0_ref[...] + _mm(_mm(z_raw, v1_ref[...]), v2_ref[...]))
        raw[2] = raw[2] + (vf_ref[...].reshape(R, G) - raw[2]) * vm
    gate = _mm(_sigmoid(_mm(z_raw, g1_ref[...])), g2_ref[...])
    mixed = []
    for j in range(4):
        parts = []
        for e in range(nbat):
            cur = raw[j][e * C:(e + 1) * C]
            parts.append(_shift_rows(cur, prev_ref[e * 4 + j], 1, reverse))
            prev_ref[e * 4 + j] = cur
        shifted = parts[0] if nbat == 1 else jnp.concatenate(parts, axis=0)
        mixed.append(raw[j] + (shifted - raw[j]) * mu_ref[j:j + 1, :])
    r, k, v, z = mixed
    w_log = -_softplus(-(w0_ref[...] + _mm(jnp.tanh(_mm(z, w1_ref[...])), w2_ref[...]))) - 0.5
    lw = -jnp.exp(w_log)
    a = _sigmoid(a0_ref[...] + _mm(_mm(z, a1_ref[...]), a2_ref[...]))
    kk = k * kk_ref[...]
    kk = kk / jnp.maximum(jnp.sqrt(head_sum(kk * kk)), 1e-12)
    k = k * (1.0 + (a - 1.0) * ka_ref[...])
    kka = kk * a

    ii = lax.broadcasted_iota(jnp.int32, (C, C), 0)
    jj = lax.broadcasted_iota(jnp.int32, (C, C), 1)
    incl = (ii <= jj) if reverse else (ii >= jj)
    strict = (ii < jj) if reverse else (ii > jj)
    same_sub = (ii & -RW_SUB) == (jj & -RW_SUB)
    eye = (ii == jj).astype(F32)
    ri = lax.broadcasted_iota(jnp.int32, (R, R), 0)
    rj = lax.broadcasted_iota(jnp.int32, (R, R), 1)
    incl_rows = (((ri <= rj) if reverse else (ri >= rj)) & ((ri & -C) == (rj & -C))).astype(BF16)

    cum = None
    lw_part = lw
    for _ in range(3):
        hi = lw_part.astype(BF16)
        d = jnp.dot(incl_rows, hi, preferred_element_type=F32)
        cum = d if cum is None else cum + d
        lw_part = lw_part - hi.astype(F32)
    end_row = 0 if reverse else C - 1
    ends = [cum[e * C + end_row:e * C + end_row + 1, :] for e in range(nbat)]
    cum_end = ends[0] if nbat == 1 else jnp.concatenate(
        [jnp.broadcast_to(t, (C, G)) for t in ends], axis=0)
    e_neg = jnp.exp(-cum)
    e_end = jnp.exp(cum_end - cum)
    bt = kk * jnp.exp(cum - lw)
    at = -kka * e_neg
    kt = k * e_neg
    rt = r * jnp.exp(cum)
    at_end = -kka * e_end
    kt_end = k * e_end

    chains = [(e, h) for e in range(nbat) for h in range(heads)]
    cs_ = range(len(chains))
    cut = lambda t, c: t[chains[c][0] * C:(chains[c][0] + 1) * C, chains[c][1] * N:(chains[c][1] + 1) * N]
    s0 = [s_ref[c] for c in cs_]
    vh = [cut(v, c) for c in cs_]
    br = [jnp.concatenate([cut(bt, c), cut(rt, c)], axis=0) for c in cs_]
    ak = [jnp.concatenate([cut(at, c), cut(kt, c)], axis=0) for c in cs_]
    a_all = [_mm_nt(br[c], ak[c]) for c in cs_]
    a_ab = [jnp.where(strict, a_all[c][:C, :C], 0.0) for c in cs_]
    a_bk = [jnp.where(strict, a_all[c][:C, C:], 0.0) for c in cs_]
    a_r = [jnp.concatenate([jnp.where(incl, a_all[c][C:, :C], 0.0),
                            jnp.where(incl, a_all[c][C:, C:], 0.0)], axis=1) for c in cs_]
    a_d = [jnp.where(same_sub, a_ab[c], 0.0) for c in cs_]
    a_o = [jnp.where(same_sub, 0.0, a_ab[c]) for c in cs_]
    tinv = [eye + a_d[c] for c in cs_]
    p = a_d
    for _ in range(int(math.log2(RW_SUB)) - 1):
        p = [_mm(p[c], p[c]) for c in cs_]
        tinv = [tinv[c] + _mm(tinv[c], p[c]) for c in cs_]
    nn = [_mm(tinv[c], a_o[c]) for c in cs_]
    levels = int(math.log2(C // RW_SUB))
    for lvl in range(levels):
        tinv = [tinv[c] + _mm(nn[c], tinv[c]) for c in cs_]
        if lvl + 1 < levels:
            nn = [_mm(nn[c], nn[c]) for c in cs_]

    br_s = [_mm_nt(br[c], s0[c]) for c in cs_]
    rhs = [br_s[c][:C] + _mm(a_bk[c], vh[c]) for c in cs_]
    u = [_mm(tinv[c], rhs[c]) for c in cs_]
    uv = [jnp.concatenate([u[c], vh[c]], axis=0) for c in cs_]
    y = [br_s[c][C:] + _mm(a_r[c], uv[c]) for c in cs_]
    for c in cs_:
        ak_end = jnp.concatenate([cut(at_end, c), cut(kt_end, c)], axis=0)
        g_end = jnp.exp(ends[chains[c][0]][:, chains[c][1] * N:(chains[c][1] + 1) * N])
        s_ref[c] = s0[c] * g_end + _mm(uv[c].T, ak_end)
    y = jnp.concatenate([jnp.concatenate(y[e * heads:(e + 1) * heads], axis=1) for e in range(nbat)],
                        axis=0)

    inv_n = 1.0 / N
    mean = head_sum(y) * inv_n
    yc = y - mean
    var = head_sum(yc * yc) * inv_n
    yn = yc * lax.rsqrt(var + RW_GN_EPS) * lnw_ref[...] + lnb_ref[...]
    bonus = head_sum(r * k * rk_ref[...]) * v
    y_ref[...] = ((yn + bonus) * gate).reshape(nbat, C, G)


def rwkv_direction(u, col0, n_ctx, vf_u, mu, w0, w1, w2, a0, a1, a2, g1, g2, k_k, k_a, r_k,
                   ln_w, ln_b, vmix, reverse):
    B, L, _ = u.shape
    G = w0.shape[-1]
    heads = G // RW_HEAD
    C = RW_CHUNK
    nbat = RW_BATCH if B % RW_BATCH == 0 else 1
    nb, ncb = L // C, n_ctx // C
    has_vmix = vmix is not None
    blk = lambda c: pl.BlockSpec((nbat, C, G), lambda b, n: (b, _scan_block(n, ncb, nb, reverse), c))
    full = lambda arr: pl.BlockSpec(arr.shape, lambda b, n: (0,) * arr.ndim)
    row = lambda t: t.reshape(1, G)
    hid = jnp.arange(RW_ONES) // RW_HEAD
    ones_half = (hid[:, None] == hid[None, :]).astype(BF16)
    ins = [u, u, u, u]
    specs = [blk(col0), blk(col0 + 1), blk(col0 + 2), blk(col0 + 3)]
    if has_vmix:
        ins.append(vf_u)
        specs.append(blk(col0 + 2))
    params = [mu, row(w0), w1.astype(BF16), w2.astype(BF16), row(a0), a1.astype(BF16), a2.astype(BF16),
              g1.astype(BF16), g2.astype(BF16), row(k_k), row(k_a), row(r_k), row(ln_w), row(ln_b)]
    if has_vmix:
        v0, v1, v2 = vmix
        params += [row(v0), v1.astype(BF16), v2.astype(BF16)]
    params.append(ones_half)
    ins += params
    specs += [full(p_) for p_ in params]
    return pl.pallas_call(
        functools.partial(_rwkv_kernel, heads=heads, n_ctx_blocks=ncb, reverse=reverse,
                          has_vmix=has_vmix),
        grid=(B // nbat, nb),
        in_specs=specs,
        out_specs=pl.BlockSpec((nbat, C, G), lambda b, n: (b, _scan_block(n, ncb, nb, reverse), 0)),
        out_shape=jax.ShapeDtypeStruct((B, L, G), F32),
        scratch_shapes=[pltpu.VMEM((nbat * heads, RW_HEAD, RW_HEAD), F32),
                        pltpu.VMEM((nbat * 4, C, G), F32)],
        compiler_params=_params("parallel", "arbitrary"),
        name="rwkv_mixer",
    )(*ins)


def _retention_kernel(*refs, heads, reverse, finish):
    it = iter(refs)
    q_ref, k_ref, v_ref = next(it), next(it), next(it)
    cos_ref, sin_ref, dm_ref, kdec_ref, qdec_ref, cdec_ref = (next(it) for _ in range(6))
    if finish:
        g_ref, o0_ref, gnw_ref = next(it), next(it), next(it)
    o_ref, s_ref = next(it), next(it)

    C, G = q_ref.shape[1], q_ref.shape[2]
    dh = G // heads
    quarter = dh // 4

    @pl.when(pl.program_id(1) == 0)
    def _():
        s_ref[...] = jnp.zeros_like(s_ref)

    lane = lax.broadcasted_iota(jnp.int32, (C, G), 1)
    first = (lane & quarter) == 0
    cos = cos_ref[...]
    sin = sin_ref[...]

    def rope(x):
        swapped = jnp.where(first, pltpu.roll(x, G - quarter, 1), pltpu.roll(x, quarter, 1))
        return x * cos + swapped * sin

    q = rope(q_ref[0])
    k = rope(k_ref[0]) * (dh ** -0.5)
    vb = v_ref[0].astype(BF16)
    qd = (q * qdec_ref[...]).astype(BF16)
    kd = (k * kdec_ref[...]).astype(BF16)
    qb = q.astype(BF16)
    kb = k.astype(BF16)
    cdec = cdec_ref[...]
    outs = []
    for h in range(heads):
        sl = slice(h * dh, (h + 1) * dh)
        s0 = s_ref[h]
        inner = lax.dot_general(qb[:, sl], kb[:, sl], NT_DIMS, preferred_element_type=F32) * dm_ref[h]
        out = jnp.dot(inner.astype(BF16), vb[:, sl], preferred_element_type=F32)
        out = out + jnp.dot(qd[:, sl], s0.astype(BF16), preferred_element_type=F32)
        kv = jnp.dot(kd[:, sl].T, vb[:, sl], preferred_element_type=F32)
        s_ref[h] = cdec[:, sl] * s0 + kv
        if finish:
            out = out + o0_ref[0, :, sl]
            m = jnp.mean(out, axis=-1, keepdims=True)
            var = jnp.mean(jnp.square(out - m), axis=-1, keepdims=True)
            out = (out - m) * lax.rsqrt(var + RT_GN_EPS)
        outs.append(out)
    o = jnp.concatenate(outs, axis=1)
    if finish:
        g = g_ref[0]
        o = o * gnw_ref[...] * (g * _sigmoid(g))
    o_ref[0] = o


def retention_direction(u, col0, n_ctx, cos_tab, sin_tab, lg, gn_w, prev_out, reverse):
    B, L, _ = u.shape
    G = gn_w.shape[-1]
    H = RT_HEADS
    dh = G // H
    C = RT_CHUNK
    nb, ncb = L // C, n_ctx // C
    finish = prev_out is not None
    idx = jnp.arange(C, dtype=F32)
    pos = (C - 1 - idx) if reverse else idx
    diff = pos[:, None] - pos[None, :]
    keep = diff > 0 if reverse else diff >= 0
    dmask = jnp.where(keep, jnp.exp(jnp.where(keep, diff, 0.0)[None] * lg[:, None, None]), 0.0)
    kdec = jnp.repeat(jnp.exp((C - 1 - pos)[:, None] * lg[None, :]), dh, axis=1)
    qdec = jnp.repeat(jnp.exp((pos + 1.0)[:, None] * lg[None, :]), dh, axis=1)
    cdec = jnp.repeat(jnp.exp(C * lg), dh)[None, :]
    tblk = lambda b, n: _scan_block(n, ncb, nb, reverse)
    blk = lambda c: pl.BlockSpec((1, C, G), lambda b, n: (b, tblk(b, n), c))
    tab = pl.BlockSpec((C, G), lambda b, n: (tblk(b, n), 0))
    ins = [u, u, u, cos_tab, sin_tab, dmask, kdec, qdec, cdec]
    specs = [blk(col0), blk(col0 + 1), blk(col0 + 2), tab, tab,
             pl.BlockSpec((H, C, C), lambda b, n: (0, 0, 0)),
             pl.BlockSpec((C, G), lambda b, n: (0, 0)),
             pl.BlockSpec((C, G), lambda b, n: (0, 0)),
             pl.BlockSpec((1, G), lambda b, n: (0, 0))]
    if finish:
        ins += [u, prev_out, gn_w.reshape(1, G)]
        specs += [blk(col0 + 3), blk(0), pl.BlockSpec((1, G), lambda b, n: (0, 0))]
    return pl.pallas_call(
        functools.partial(_retention_kernel, heads=H, reverse=reverse, finish=finish),
        grid=(B, nb),
        in_specs=specs,
        out_specs=blk(0),
        out_shape=jax.ShapeDtypeStruct((B, L, G), F32),
        scratch_shapes=[pltpu.VMEM((H, dh, dh), F32)],
        compiler_params=_params("parallel", "arbitrary"),
        name="retention_mixer",
    )(*ins)


def rope_tables(n_ctx, L, G):
    dh = G // RT_HEADS
    quarter = dh // 4
    n_rows = L // GRID_W
    rows = jnp.repeat(jnp.arange(n_rows, dtype=F32), GRID_W)
    cols = jnp.tile(jnp.arange(GRID_W, dtype=F32), n_rows)
    inv = jnp.power(ROPE_BASE, -jnp.arange(quarter, dtype=F32) / quarter)
    lane = jnp.arange(G)
    use_cols = (lane % dh) >= (dh // 2)
    pos = jnp.where(use_cols[None, :], cols[:, None], rows[:, None])
    ang = pos * inv[lane % quarter][None, :]
    sign = jnp.where((lane % (2 * quarter)) < quarter, -1.0, 1.0)[None, :]
    cos = jnp.concatenate([jnp.ones((n_ctx, G), F32), jnp.cos(ang)], axis=0)
    sin = jnp.concatenate([jnp.zeros((n_ctx, G), F32), jnp.sin(ang) * sign], axis=0)
    return cos, sin


def _moe_stream_kernel(te_ref, nt_ref, first_ref, nxt_ref, slot_ref,
                       x_ref, wgu_hbm, bgu_ref, wdn_hbm, bdn_ref, g_ref, o_ref,
                       wgu_bf, wdn_bf, stage, sem, done_ref, *, layer):
    i = pl.program_id(0)
    active = i < nt_ref[0]
    e, s, nxt = te_ref[i], slot_ref[i], nxt_ref[i]
    D, F = wgu_bf.shape[1], wdn_bf.shape[1]
    ch = stage.shape[1]
    n_gu = D // ch
    n_ch = n_gu + F // ch

    def copy(expert, c, gate_up):
        src = (wgu_hbm.at[layer, expert, pl.ds(pl.multiple_of(c * ch, ch), ch), :] if gate_up else
               wdn_hbm.at[layer, expert, pl.ds(pl.multiple_of((c - n_gu) * ch, ch), ch), :])
        return pltpu.make_async_copy(src, stage.at[c % 2], sem.at[c % 2])

    def start(expert, c):
        @pl.when(c < n_gu)
        def _():
            copy(expert, c, True).start()

        @pl.when((c >= n_gu) & (c < n_ch))
        def _():
            copy(expert, c, False).start()

    def land(expert, c, slot):
        pltpu.make_async_copy(wgu_hbm.at[layer, expert, pl.ds(0, ch), :], stage.at[c % 2],
                              sem.at[c % 2]).wait()
        val = stage[c % 2].astype(BF16)

        @pl.when(c < n_gu)
        def _():
            wgu_bf[slot, pl.ds(pl.multiple_of(c * ch, ch), ch), :] = val

        @pl.when(c >= n_gu)
        def _():
            wdn_bf[slot, pl.ds(pl.multiple_of((c - n_gu) * ch, ch), ch), :] = val

        start(expert, c + 2)

    @pl.when(active & (first_ref[i] == 1))
    def _():
        @pl.when(i == 0)
        def _():
            done_ref[0] = 0
            start(e, 0)
            start(e, 1)

        def body(c, carry):
            land(e, c, s)
            return carry

        lax.fori_loop(done_ref[0], n_ch, body, 0)
        done_ref[0] = 0

        @pl.when(nxt >= 0)
        def _():
            start(nxt, 0)
            start(nxt, 1)

    @pl.when(active)
    def _():
        gu = jnp.dot(x_ref[...], wgu_bf[s], preferred_element_type=F32) + bgu_ref[0]
        glu = jnp.minimum(gu[:, :F], SWIGLU_LIMIT)
        lin = jnp.clip(gu[:, F:], -SWIGLU_LIMIT, SWIGLU_LIMIT)
        act = glu * jax.nn.sigmoid(SWIGLU_ALPHA * glu) * (lin + 1.0)
        y = jnp.dot(act.astype(BF16), wdn_bf[s], preferred_element_type=F32) + bdn_ref[0]
        o_ref[...] = (y * g_ref[...]).astype(o_ref.dtype)

    @pl.when(jnp.logical_not(active))
    def _():
        o_ref[...] = jnp.zeros_like(o_ref)

    for _ in range(MOE_CHUNKS_PER_STEP):
        @pl.when(active & (nxt >= 0) & (done_ref[0] < n_ch))
        def _():
            c = done_ref[0]
            land(nxt, c, 1 - s)
            done_ref[0] = c + 1


def moe_grouped(xs, tile_expert, n_tiles_used, sizes, layer, w_gu, b_gu, w_dn, b_dn, row_gate):
    P, D = xs.shape
    _, E, _, F2 = w_gu.shape
    F = F2 // 2
    tm = MOE_TILE
    nt = P // tm
    tile = jnp.arange(nt, dtype=jnp.int32)
    prev = jnp.concatenate([jnp.full((1,), -1, jnp.int32), tile_expert[:-1]])
    first = ((tile_expert != prev) & (tile < n_tiles_used[0])).astype(jnp.int32)
    slot = (jnp.cumsum(first) - 1) % 2
    expert = jnp.arange(E, dtype=jnp.int32)
    later = jnp.where((sizes[None, :] > 0) & (expert[None, :] > expert[:, None]), expert[None, :], E)
    next_expert = jnp.min(later, axis=1)
    nxt = jnp.where(next_expert < E, next_expert, -1).astype(jnp.int32)[tile_expert]
    grid_spec = pltpu.PrefetchScalarGridSpec(
        num_scalar_prefetch=5,
        grid=(nt,),
        in_specs=[
            pl.BlockSpec((tm, D), lambda i, *_: (i, 0)),
            pl.BlockSpec(memory_space=pl.ANY),
            pl.BlockSpec((1, 1, F2), lambda i, te, *_: (te[i], 0, 0)),
            pl.BlockSpec(memory_space=pl.ANY),
            pl.BlockSpec((1, 1, D), lambda i, te, *_: (te[i], 0, 0)),
            pl.BlockSpec((tm, 1), lambda i, *_: (i, 0)),
        ],
        out_specs=pl.BlockSpec((tm, D), lambda i, *_: (i, 0)),
        scratch_shapes=[pltpu.VMEM((2, D, F2), BF16), pltpu.VMEM((2, F, D), BF16),
                        pltpu.VMEM((2, MOE_CHUNK_ROWS, D), F32), pltpu.SemaphoreType.DMA((2,)),
                        pltpu.SMEM((1,), jnp.int32)],
    )
    return pl.pallas_call(
        functools.partial(_moe_stream_kernel, layer=layer),
        grid_spec=grid_spec,
        out_shape=jax.ShapeDtypeStruct((P, D), BF16),
        compiler_params=_params("arbitrary"),
        name="moe_grouped",
    )(tile_expert, n_tiles_used, first, nxt, slot.astype(jnp.int32),
      xs, w_gu, b_gu.reshape(E, 1, F2), w_dn, b_dn.reshape(E, 1, D), row_gate)


def _router_kernel(x_ref, nw_ref, sh_ref, sc_ref, rwh_ref, rwl_ref, rb_ref,
                   h_ref, idx_ref, p_ref, rank_ref, cnt_ref, carry_ref):
    @pl.when((pl.program_id(0) == 0) & (pl.program_id(1) == 0))
    def _():
        carry_ref[...] = jnp.zeros_like(carry_ref)

    x = x_ref[0]
    y = x * lax.rsqrt(jnp.mean(x * x, axis=-1, keepdims=True) + RMS_EPS)
    y = y * nw_ref[...] * (1.0 + sc_ref[0]) + sh_ref[0]
    y_hi = y.astype(BF16)
    h_ref[0] = y_hi
    y_lo = (y - y_hi.astype(F32)).astype(BF16)
    logits = (jnp.dot(y_hi, rwh_ref[...], preferred_element_type=F32)
              + jnp.dot(y_lo, rwh_ref[...], preferred_element_type=F32)
              + jnp.dot(y_hi, rwl_ref[...], preferred_element_type=F32)) + rb_ref[...]
    E = logits.shape[-1]
    lane = lax.broadcasted_iota(jnp.int32, logits.shape, 1)
    vals, ids = [], []
    for _ in range(TOP_K):
        m = jnp.max(logits, axis=-1, keepdims=True)
        i = jnp.min(jnp.where(logits == m, lane, E), axis=-1, keepdims=True)
        vals.append(m)
        ids.append(i)
        logits = jnp.where(lane == i, -jnp.inf, logits)
    e = jnp.exp(jnp.concatenate(vals, axis=1) - vals[0])
    p_ref[0] = e / jnp.sum(e, axis=-1, keepdims=True)
    idx_ref[0] = jnp.concatenate(ids, axis=1)

    tm = x.shape[0]
    chosen = [lane == i for i in ids]
    onehot = sum(c.astype(F32) for c in chosen)
    earlier = (lax.broadcasted_iota(jnp.int32, (tm, tm), 1)
               < lax.broadcasted_iota(jnp.int32, (tm, tm), 0)).astype(BF16)
    prefix = jnp.dot(earlier, onehot.astype(BF16), preferred_element_type=F32) + carry_ref[...]
    rank_ref[0] = jnp.concatenate(
        [jnp.sum(jnp.where(c, prefix, 0.0), axis=-1, keepdims=True) for c in chosen], axis=1).astype(jnp.int32)
    total = carry_ref[...] + jnp.sum(onehot, axis=0, keepdims=True)
    carry_ref[...] = total
    cnt_ref[...] = total


def moe_router(xs, row0, n_rows, n_ctx, nw, shift, scale, router_w, router_b):
    B, _, D = xs.shape
    E = router_w.shape[1]
    tm = ROW_TILE
    ncb, b0 = n_ctx // tm, row0 // tm
    mod_idx = lambda b, i: (2 * b + ((i + b0) >= ncb).astype(jnp.int32), 0, 0)
    rw_hi = router_w.astype(BF16)
    rw_lo = (router_w - rw_hi.astype(F32)).astype(BF16)
    full = lambda shape: pl.BlockSpec(shape, lambda b, i: (0,) * len(shape))
    out = lambda w: pl.BlockSpec((1, tm, w), lambda b, i: (b, i, 0))
    return pl.pallas_call(
        _router_kernel,
        grid=(B, n_rows // tm),
        in_specs=[pl.BlockSpec((1, tm, D), lambda b, i: (b, i + b0, 0)), full((1, D)),
                  pl.BlockSpec((1, 1, D), mod_idx), pl.BlockSpec((1, 1, D), mod_idx),
                  full((D, E)), full((D, E)), full((1, E))],
        out_specs=[out(D), out(TOP_K), out(TOP_K), out(TOP_K), full((1, E))],
        out_shape=[jax.ShapeDtypeStruct((B, n_rows, D), BF16),
                   jax.ShapeDtypeStruct((B, n_rows, TOP_K), jnp.int32),
                   jax.ShapeDtypeStruct((B, n_rows, TOP_K), F32),
                   jax.ShapeDtypeStruct((B, n_rows, TOP_K), jnp.int32),
                   jax.ShapeDtypeStruct((1, E), F32)],
        scratch_shapes=[pltpu.VMEM((1, E), F32)],
        compiler_params=_params("arbitrary", "arbitrary"),
        name="moe_router",
    )(xs, nw.reshape(1, D), shift.reshape(2 * B, 1, D), scale.reshape(2 * B, 1, D),
      rw_hi, rw_lo, router_b.reshape(1, E))


def _cast_kernel(x_ref, o_ref):
    o_ref[...] = x_ref[...].astype(BF16)


def cast_bf16(w):
    C = w.shape[-1]
    R = w.size // C
    tr = 8
    while tr * 2 * C <= CAST_BLOCK_ELEMS and R % (tr * 2) == 0:
        tr *= 2
    spec = pl.BlockSpec((tr, C), lambda i: (i, 0))
    return pl.pallas_call(
        _cast_kernel,
        grid=(R // tr,),
        in_specs=[spec],
        out_specs=spec,
        out_shape=jax.ShapeDtypeStruct((R, C), BF16),
        compiler_params=_params("parallel"),
        name="cast_bf16",
    )(w.reshape(R, C)).reshape(w.shape)


def _combine_kernel(g_ref, res_ref, gate_ref, fw_ref, o_ref, *, final_norm):
    f = g_ref[0, 0].astype(F32)
    for k in range(1, g_ref.shape[0]):
        f = f + g_ref[k, 0].astype(F32)
    y = res_ref[0] + gate_ref[0] * f
    if final_norm:
        y = y * lax.rsqrt(jnp.mean(y * y, axis=-1, keepdims=True) + RMS_EPS) * fw_ref[...]
    o_ref[0] = y


def moe_combine(g, xs, row0, n_ctx, gate, final_w):
    K, B, n_rows, D = g.shape
    tm = ROW_TILE
    ncb, b0 = n_ctx // tm, row0 // tm
    g_idx = lambda b, i: (2 * b + ((i + b0) >= ncb).astype(jnp.int32), 0, 0)
    final_norm = final_w is not None
    fw = final_w if final_norm else jnp.ones((D,), F32)
    return pl.pallas_call(
        functools.partial(_combine_kernel, final_norm=final_norm),
        grid=(B, n_rows // tm),
        in_specs=[pl.BlockSpec((K, 1, tm, D), lambda b, i: (0, b, i, 0)),
                  pl.BlockSpec((1, tm, D), lambda b, i: (b, i + b0, 0)),
                  pl.BlockSpec((1, 1, D), g_idx),
                  pl.BlockSpec((1, D), lambda b, i: (0, 0))],
        out_specs=pl.BlockSpec((1, tm, D), lambda b, i: (b, i, 0)),
        out_shape=jax.ShapeDtypeStruct((B, n_rows, D), F32),
        compiler_params=_params("parallel", "parallel"),
        name="moe_combine",
    )(g, xs, gate.reshape(2 * B, 1, D), fw.reshape(1, D))


def moe(t, idx, probs, rank, counts, layer, w_gu, b_gu, w_dn, b_dn):
    T, D = t.shape
    E = N_EXPERTS
    tm = MOE_TILE
    sizes = counts.astype(jnp.int32)
    padded = ((sizes + tm - 1) // tm) * tm
    ends = jnp.cumsum(padded)
    starts = ends - padded
    idx_f, probs_f, rank_f = idx.reshape(-1), probs.reshape(-1), rank.reshape(-1)
    expert = jnp.arange(E, dtype=jnp.int32)
    dest_f = rank_f + jnp.sum(jnp.where(idx_f[None, :] == expert[:, None], starts[:, None], 0), axis=0)
    n_rows = T * TOP_K + E * tm
    nt = n_rows // tm
    tile_start = jnp.arange(nt, dtype=jnp.int32) * tm
    tile_expert = jnp.minimum(jnp.sum((ends[None, :] <= tile_start[:, None]).astype(jnp.int32), axis=1),
                              E - 1)
    order = jnp.argsort(idx_f, stable=True).astype(jnp.int32)
    within = (tile_start - starts[tile_expert])[:, None] + jnp.arange(tm, dtype=jnp.int32)[None, :]
    valid = (within < sizes[tile_expert][:, None]).reshape(-1)
    first_of = (jnp.cumsum(sizes) - sizes)[tile_expert]
    src = order[jnp.clip(first_of[:, None] + within, 0, T * TOP_K - 1).reshape(-1)]
    row_token = jnp.where(valid, src // TOP_K, jnp.arange(n_rows, dtype=jnp.int32) % T)
    row_gate = jnp.where(valid, probs_f[src], 0.0)
    xs = t[row_token]
    n_used = (ends[-1] // tm).astype(jnp.int32).reshape(1)
    ys = moe_grouped(xs, tile_expert, n_used, sizes, layer, w_gu, b_gu, w_dn, b_dn,
                     row_gate.reshape(n_rows, 1))
    return ys[dest_f.reshape(T, TOP_K).T]


HY_COLS = 256
HY_ROWS = 512


def _twiddle_kernel(c1_ref, s1_ref, c2_ref, s2_ref, cs_ref, ss_ref):
    c1, s1 = c1_ref[0], s1_ref[0]
    c2, s2 = c2_ref[...], s2_ref[...]
    cs_ref[...] = (c1 * c2 - s1 * s2).astype(BF16)
    ss_ref[...] = (s1 * c2 + c1 * s2).astype(BF16)


def dft_matrices(L):
    R = min(64, L)
    n_hi = L // R
    period = 8 * L
    a = 2 * jnp.arange(L, dtype=jnp.int32) + 1
    m1 = (a[None, :] * (2 * R * jnp.arange(n_hi, dtype=jnp.int32))[:, None]) % period
    m2 = (a[None, :] * (2 * jnp.arange(R, dtype=jnp.int32) + 1)[:, None]) % period
    ang = lambda m: (m.astype(F32) - jnp.where(m >= period // 2, period, 0).astype(F32)) * (math.pi / (4 * L))
    c1, s1 = jnp.cos(ang(m1)).reshape(n_hi, 1, L), jnp.sin(ang(m1)).reshape(n_hi, 1, L)
    c2, s2 = jnp.cos(ang(m2)), jnp.sin(ang(m2))
    hi = pl.BlockSpec((1, 1, L), lambda i: (i, 0, 0))
    lo = pl.BlockSpec((R, L), lambda i: (0, 0))
    out = pl.BlockSpec((R, L), lambda i: (i, 0))
    return pl.pallas_call(
        _twiddle_kernel,
        grid=(n_hi,),
        in_specs=[hi, hi, lo, lo],
        out_specs=[out, out],
        out_shape=[jax.ShapeDtypeStruct((L, L), BF16)] * 2,
        compiler_params=_params("parallel"),
        name="dft_matrices",
    )(c1, s1, c2, s2)


def _dft_pair_kernel(cs_ref, ss_ref, x_ref, oc_ref, os_ref):
    x = x_ref[...]
    oc_ref[...] = jnp.dot(cs_ref[...], x, preferred_element_type=F32)
    os_ref[...] = jnp.dot(ss_ref[...], x, preferred_element_type=F32)


def dft_pair(cs, ss, x_bf16):
    L, M = x_bf16.shape
    tm = min(HY_ROWS, L)
    tn = min(512, M)
    mat = pl.BlockSpec((tm, L), lambda j, i: (i, 0))
    out = pl.BlockSpec((tm, tn), lambda j, i: (i, j))
    return pl.pallas_call(
        _dft_pair_kernel,
        grid=(M // tn, L // tm),
        in_specs=[mat, mat, pl.BlockSpec((L, tn), lambda j, i: (0, j))],
        out_specs=[out, out],
        out_shape=[jax.ShapeDtypeStruct((L, M), F32)] * 2,
        compiler_params=_params("parallel", "arbitrary"),
        name="dft_pair",
    )(cs, ss, x_bf16)


def _hyena_conv_kernel(z_ref, gate_ref, cs_ref, ss_ref, hre_ref, him_ref, bias_ref, o_ref,
                       zb_ref, yre_ref, yim_ref):
    phase = pl.program_id(1)
    m = pl.program_id(2)
    tm = cs_ref.shape[0]
    L = z_ref.shape[1]
    rows = pl.ds(pl.multiple_of(m * tm, tm), tm)

    @pl.when((phase == 0) & (m == 0))
    def _():
        zb_ref[...] = z_ref[0].astype(BF16)

    @pl.when(phase == 0)
    def _():
        xc = jnp.dot(cs_ref[...], zb_ref[...], preferred_element_type=F32)
        xs = jnp.dot(ss_ref[...], zb_ref[...], preferred_element_type=F32)
        hre, him = hre_ref[...], him_ref[...]
        yre_ref[rows, :] = (xc * hre + xs * him).astype(BF16)
        yim_ref[rows, :] = (xc * him - xs * hre).astype(BF16)

    @pl.when(phase == 1)
    def _():
        y = (jnp.dot(cs_ref[...], yre_ref[...], preferred_element_type=F32)
             - jnp.dot(ss_ref[...], yim_ref[...], preferred_element_type=F32)) * (1.0 / L)
        o_ref[0] = gate_ref[0] * (y + bias_ref[...] * z_ref[0, rows, :])


def hyena_conv(z_src, z_col, gate_src, gate_col, cs, ss, hre, him, bias):
    B, L, _ = z_src.shape
    G = bias.shape[0]
    tc = HY_COLS
    tm = min(HY_ROWS, L)
    ncb = G // tc
    nm = L // tm
    pin = lambda phase, m, keep: jnp.where(phase == keep, m, (nm - 1) * (1 - keep))
    return pl.pallas_call(
        _hyena_conv_kernel,
        grid=(B * ncb, 2, nm),
        in_specs=[
            pl.BlockSpec((1, L, tc), lambda i, p, m: (i // ncb, 0, z_col * ncb + i % ncb)),
            pl.BlockSpec((1, tm, tc), lambda i, p, m: (i // ncb, pin(p, m, 1), gate_col * ncb + i % ncb)),
            pl.BlockSpec((tm, L), lambda i, p, m: (m, 0)),
            pl.BlockSpec((tm, L), lambda i, p, m: (m, 0)),
            pl.BlockSpec((tm, tc), lambda i, p, m: (pin(p, m, 0), i % ncb)),
            pl.BlockSpec((tm, tc), lambda i, p, m: (pin(p, m, 0), i % ncb)),
            pl.BlockSpec((1, tc), lambda i, p, m: (0, i % ncb)),
        ],
        out_specs=pl.BlockSpec((1, tm, tc), lambda i, p, m: (i // ncb, pin(p, m, 1), i % ncb)),
        out_shape=jax.ShapeDtypeStruct((B, L, G), F32),
        scratch_shapes=[pltpu.VMEM((L, tc), BF16), pltpu.VMEM((L, tc), BF16), pltpu.VMEM((L, tc), BF16)],
        compiler_params=_params("parallel", "arbitrary", "arbitrary"),
        name="hyena_conv",
    )(z_src, gate_src, cs, ss, hre, him, bias.reshape(1, G))


def hyena_filter_spectrum(L, G, cs, ss, w1, b1, w2, b2, w3, b3, w4, freq):
    t = jnp.linspace(0.0, 1.0, L, dtype=F32)[:, None]
    bands = (HY_EMB - 1) // 2
    fr = jnp.linspace(1e-4, bands - 1, bands, dtype=F32)
    ang = (2.0 * math.pi / L) * jnp.arange(L, dtype=F32)[:, None] * fr[None, :]
    z = jnp.concatenate([t, jnp.cos(ang), -jnp.sin(ang)], axis=-1)
    h = jnp.sin(freq * (z @ w1 + b1))
    h = jnp.sin(freq * (h @ w2 + b2))
    h = jnp.sin(freq * (h @ w3 + b3))
    h = (h @ w4).reshape(L, 2, HY_ORDER * G)
    deltas = jnp.abs(jnp.linspace(HY_MIN_DECAY, HY_MAX_DECAY, G, dtype=F32))
    h = h * jnp.tile(jnp.exp(-t * deltas), (1, HY_ORDER))[:, None, :]
    fwd = h[:, 0]
    bwd = jnp.concatenate([h[1:, 1], jnp.zeros((1, HY_ORDER * G), F32)], axis=0)
    norm = jnp.sum(jnp.abs(fwd), axis=0) + jnp.sum(jnp.abs(bwd), axis=0)
    xc, xs = dft_pair(cs, ss, jnp.concatenate([fwd, bwd], axis=1).astype(BF16))
    M = HY_ORDER * G
    a = xc[:, :M] + xc[:, M:]
    b = xs[:, M:] - xs[:, :M]
    ph = (math.pi / (2 * L)) * (jnp.arange(L, dtype=F32)[:, None] + 0.5)
    hre = (jnp.cos(ph) * a - jnp.sin(ph) * b) / norm
    him = (jnp.sin(ph) * a + jnp.cos(ph) * b) / norm
    split = lambda t_: jnp.moveaxis(t_.reshape(L, HY_ORDER, G), 1, 0)
    return split(hre), split(him)


def hyena_mixer(u, cs, ss, short_w, short_b, fw1, fb1, fw2, fb2, fw3, fb3, fw4, ffreq, bias):
    L = u.shape[1]
    G = u.shape[2] // 3
    pad = jnp.pad(u, ((0, 0), (1, 1), (0, 0)))
    us = short_w[0] * pad[:, :L] + short_w[1] * u + short_w[2] * pad[:, 2:] + short_b
    hre, him = hyena_filter_spectrum(L, G, cs, ss, fw1, fb1, fw2, fb2, fw3, fb3, fw4, ffreq)
    z = hyena_conv(us, 0, us, 1, cs, ss, hre[0], him[0], bias[0])
    return hyena_conv(z, 0, us, 2, cs, ss, hre[1], him[1], bias[1])


def rms_norm(x, w):
    y = x * lax.rsqrt(jnp.mean(x * x, axis=-1, keepdims=True) + RMS_EPS)
    return y * w


def kernel(x, c, ctx, c_ctx, ada_w, ada_b, norm1_w, norm2_w, w_in, w_out, hy_short_w, hy_short_b, hy_f_w1, hy_f_b1, hy_f_w2, hy_f_b2, hy_f_w3, hy_f_b3, hy_f_w4, hy_f_freq, hy_bias, rg_conv_w, rg_conv_b, rg_wa, rg_ba, rg_wx, rg_bx, rg_lambda, rw_mu, rw_w0, rw_w1, rw_w2, rw_a0, rw_a1, rw_a2, rw_g1, rw_g2, rw_k_k, rw_k_a, rw_r_k, rw_ln_w, rw_ln_b, rw_v0, rw_v1, rw_v2, rt_decay, rt_gn_w, moe_router_w, moe_router_b, moe_w_gu, moe_b_gu, moe_w_dn, moe_b_dn, final_norm_w):
    depth = ada_w.shape[0]
    B, L, D = x.shape
    n_ctx = ctx.shape[1]
    G = D // N_GROUPS
    cond = jnp.concatenate([jnp.broadcast_to(jax.nn.silu(c_ctx.astype(F32))[None, :], (B, D)),
                            jax.nn.silu(c.astype(F32))], axis=0)
    xs = jnp.concatenate([ctx.astype(x.dtype), x], axis=1)
    cos_tab, sin_tab = rope_tables(n_ctx, L, G)
    dft_lat = dft_matrices(L)
    dft_ctx = dft_matrices(n_ctx)
    w_in_b, w_out_b = cast_bf16(w_in), cast_bf16(w_out)
    u_first = None

    for l in range(depth):
        last = l == depth - 1
        mod = cond @ ada_w[l].astype(F32) + ada_b[l]
        mod = jnp.stack([mod[:B], mod[B:]], axis=1)
        mods = jnp.split(mod, N_MOD, axis=-1)

        u = modulated_matmul(xs, norm1_w[l], mods[0], mods[1], w_in_b, l, n_ctx, tn=w_in.shape[2] // 2)
        if l == 0:
            u_first = u

        hy_p = (hy_short_w[l], hy_short_b[l], hy_f_w1[l], hy_f_b1[l], hy_f_w2[l], hy_f_b2[l],
                hy_f_w3[l], hy_f_b3[l], hy_f_w4[l], hy_f_freq[l], hy_bias[l])
        hy_l = hyena_mixer(u[:, n_ctx:, : 3 * G], *dft_lat, *hy_p)
        if last:
            hy_c = jnp.zeros((B, n_ctx, G), F32)
        else:
            hy_c = hyena_mixer(u[:, :n_ctx, : 3 * G], *dft_ctx, *hy_p)
        hy = jnp.concatenate([hy_c, hy_l], axis=1)

        rg = [rglru_direction(u, 3, n_ctx, rg_conv_w[l, d], rg_conv_b[l, d], rg_wa[l, d], rg_ba[l, d],
                              rg_wx[l, d], rg_bx[l, d], rg_lambda[l, d], reverse=(d == 1))
              for d in range(2)]
        vmix = None if l == 0 else (rw_v0[l - 1], rw_v1[l - 1], rw_v2[l - 1])
        rw = [rwkv_direction(u, 5, n_ctx, u_first, rw_mu[l, d], rw_w0[l, d], rw_w1[l, d], rw_w2[l, d],
                             rw_a0[l, d], rw_a1[l, d], rw_a2[l, d], rw_g1[l], rw_g2[l], rw_k_k[l],
                             rw_k_a[l], rw_r_k[l], rw_ln_w[l], rw_ln_b[l], vmix, reverse=(d == 1))
              for d in range(2)]
        lg = -jax.nn.softplus(rt_decay[l].astype(F32))
        rt0 = retention_direction(u, 9, n_ctx, cos_tab, sin_tab, lg[0], rt_gn_w[l], None, reverse=False)
        rt = retention_direction(u, 9, n_ctx, cos_tab, sin_tab, lg[1], rt_gn_w[l], rt0, reverse=True)

        xs = gated_out_proj((hy, rg[0], rg[1], rw[0], rw[1], rt), w_out_b, l, xs, mods[2], n_ctx)

        row0 = n_ctx if last else 0
        n_rows = xs.shape[1] - row0
        h, idx, probs, rank, counts = moe_router(xs, row0, n_rows, n_ctx, norm2_w[l], mods[3], mods[4],
                                                 moe_router_w[l], moe_router_b[l])
        flat = lambda t_: t_.reshape(-1, TOP_K)
        g = moe(h.reshape(-1, D), flat(idx), flat(probs), flat(rank), counts.reshape(-1), l,
                moe_w_gu, moe_b_gu[l], moe_w_dn, moe_b_dn[l])
        xs = moe_combine(g.reshape(TOP_K, B, n_rows, D), xs, row0, n_ctx, mods[5],
                         final_norm_w if last else None)
    return xs
```

```python
import functools
import math

import jax
import jax.numpy as jnp
from jax import lax
from jax.experimental import pallas as pl
from jax.experimental.pallas import tpu as pltpu

F32 = jnp.float32
BF16 = jnp.bfloat16
HIGHEST = lax.Precision.HIGHEST

GRID_W = 64
N_GROUPS = 4
N_MOD = 6
RMS_EPS = 1e-6

HY_ORDER = 2
HY_EMB = 33
HY_TARGET = 1e-2
HY_MIN_DECAY = math.log(HY_TARGET) / 1.5
HY_MAX_DECAY = math.log(HY_TARGET) / 0.3

RG_HEADS = 8
RG_CONV = 4
RG_C = 8.0

RW_HEAD = 64
RW_GN_EPS = 64e-5
RW_CHUNK = 64
RW_SUB = 16
RW_BATCH = 4
RW_ONES = 256

RT_HEADS = 4
RT_CHUNK = 128
RT_GN_EPS = 1e-6
ROPE_BASE = 10000.0

N_EXPERTS = 32
TOP_K = 4
SWIGLU_LIMIT = 7.0
SWIGLU_ALPHA = 1.702

VMEM_LIMIT_BYTES = 56 * 1024 * 1024
LRU_CHUNK = 256
MOE_TILE = 256
MOE_CHUNK_ROWS = 512
MOE_CHUNKS_PER_STEP = 2
ROW_TILE = 256
CAST_BLOCK_ELEMS = 2 * 1024 * 1024

NT_DIMS = (((1,), (1,)), ((), ()))


def _params(*sem):
    return pltpu.CompilerParams(dimension_semantics=sem, vmem_limit_bytes=VMEM_LIMIT_BYTES)


def _scan_block(n, n_ctx_blocks, n_blocks, reverse):
    if not reverse:
        return n
    return jnp.where(n < n_ctx_blocks, n_ctx_blocks - 1 - n, n_blocks - 1 + n_ctx_blocks - n)


def _shift_rows(cur, prv, j, reverse):
    T = cur.shape[0]
    row = lax.broadcasted_iota(jnp.int32, cur.shape, 0)
    if not reverse:
        return jnp.where(row < j, pltpu.roll(prv, j, 0), pltpu.roll(cur, j, 0))
    return jnp.where(row >= T - j, pltpu.roll(prv, T - j, 0), pltpu.roll(cur, T - j, 0))


def _mm(a, b):
    return jnp.dot(a.astype(BF16), b.astype(BF16), preferred_element_type=F32)


def _mm_nt(a, b):
    return lax.dot_general(a.astype(BF16), b.astype(BF16), NT_DIMS, preferred_element_type=F32)


def _split_dot(x, m_bf16, parts):
    acc = None
    for _ in range(parts):
        hi = x.astype(BF16)
        d = jnp.dot(hi, m_bf16, preferred_element_type=F32)
        acc = d if acc is None else acc + d
        x = x - hi.astype(F32)
    return acc


def _expm1(x):
    small = x * (1.0 + x * (0.5 + x * (1.0 / 6.0 + x * (1.0 / 24.0 + x * (1.0 / 120.0)))))
    return jnp.where(jnp.abs(x) < 0.1, small, jnp.exp(x) - 1.0)


def _softplus(x):
    return jnp.maximum(x, 0.0) + jnp.log(1.0 + jnp.exp(-jnp.abs(x)))


def _sigmoid(x):
    return 1.0 / (1.0 + jnp.exp(-x))


def _modmm_kernel(x_ref, nw_ref, sh_ref, sc_ref, w_ref, o_ref):
    x = x_ref[0]
    y = x * lax.rsqrt(jnp.mean(x * x, axis=-1, keepdims=True) + RMS_EPS)
    y = y * nw_ref[...] * (1.0 + sc_ref[0]) + sh_ref[0]
    o_ref[0] = jnp.dot(y.astype(BF16), w_ref[...], preferred_element_type=F32)


def modulated_matmul(x, nw, shift, scale, w_bf16, layer, n_ctx, tn):
    B, L, D = x.shape
    N = w_bf16.shape[2]
    tm = ROW_TILE
    ncb = n_ctx // tm
    mod_idx = lambda j, b, i: (2 * b + (i >= ncb).astype(jnp.int32), 0, 0)
    return pl.pallas_call(
        _modmm_kernel,
        grid=(N // tn, B, L // tm),
        in_specs=[
            pl.BlockSpec((1, tm, D), lambda j, b, i: (b, i, 0)),
            pl.BlockSpec((1, D), lambda j, b, i: (0, 0)),
            pl.BlockSpec((1, 1, D), mod_idx),
            pl.BlockSpec((1, 1, D), mod_idx),
            pl.BlockSpec((None, D, tn), lambda j, b, i: (layer, 0, j)),
        ],
        out_specs=pl.BlockSpec((1, tm, tn), lambda j, b, i: (b, i, j)),
        out_shape=jax.ShapeDtypeStruct((B, L, N), F32),
        compiler_params=_params("parallel", "parallel", "parallel"),
        name="modulated_matmul",
    )(x, nw.reshape(1, D), shift.reshape(2 * B, 1, D), scale.reshape(2 * B, 1, D), w_bf16)


def _outproj_kernel(hy_ref, rg0_ref, rg1_ref, rw0_ref, rw1_ref, rt_ref, w_ref, res_ref, g_ref, o_ref):
    G = hy_ref.shape[2]
    slabs = (hy_ref[0], rg0_ref[0] + rg1_ref[0], rw0_ref[0] + rw1_ref[0], rt_ref[0])
    acc = None
    for i, s in enumerate(slabs):
        d = jnp.dot(s.astype(BF16), w_ref[i * G:(i + 1) * G, :], preferred_element_type=F32)
        acc = d if acc is None else acc + d
    o_ref[0] = res_ref[0] + g_ref[0] * acc


def gated_out_proj(slabs, w_bf16, layer, res, gate, n_ctx):
    B, L, D = res.shape
    G = slabs[0].shape[2]
    tm = ROW_TILE
    ncb = n_ctx // tm
    g_idx = lambda b, i: (2 * b + (i >= ncb).astype(jnp.int32), 0, 0)
    slab_spec = pl.BlockSpec((1, tm, G), lambda b, i: (b, i, 0))
    return pl.pallas_call(
        _outproj_kernel,
        grid=(B, L // tm),
        in_specs=[slab_spec] * 6 + [
            pl.BlockSpec((None, N_GROUPS * G, D), lambda b, i: (layer, 0, 0)),
            pl.BlockSpec((1, tm, D), lambda b, i: (b, i, 0)),
            pl.BlockSpec((1, 1, D), g_idx),
        ],
        out_specs=pl.BlockSpec((1, tm, D), lambda b, i: (b, i, 0)),
        out_shape=jax.ShapeDtypeStruct((B, L, D), F32),
        compiler_params=_params("parallel", "parallel"),
        name="gated_out_proj",
    )(*slabs, w_bf16, res, gate.reshape(2 * B, 1, D))


def _rglru_kernel(x_ref, gate_ref, cw_ref, cb_ref, wa_ref, ba_ref, wx_ref, bx_ref, lam_ref,
                  o_ref, prev_ref, carry_ref, *, n_ctx_blocks, reverse):
    T = x_ref.shape[1]
    n = pl.program_id(1)

    @pl.when(n == 0)
    def _():
        carry_ref[...] = jnp.zeros_like(carry_ref)

    @pl.when((n == 0) | (n == n_ctx_blocks))
    def _():
        prev_ref[...] = jnp.zeros_like(prev_ref)

    x = x_ref[0]
    prv = prev_ref[...]
    xc = cb_ref[...] + cw_ref[RG_CONV - 1:RG_CONV, :] * x
    for j in range(1, RG_CONV):
        xc = xc + cw_ref[RG_CONV - 1 - j:RG_CONV - j, :] * _shift_rows(x, prv, j, reverse)
    prev_ref[...] = x

    xb = xc.astype(BF16)
    r = _sigmoid(jnp.dot(xb, wa_ref[...], preferred_element_type=F32) + ba_ref[...])
    i = _sigmoid(jnp.dot(xb, wx_ref[...], preferred_element_type=F32) + bx_ref[...])
    log_a = -RG_C * r * _softplus(-lam_ref[...])
    a = jnp.exp(log_a)
    b = jnp.sqrt(-_expm1(2.0 * log_a)) * (i * xc)

    row = lax.broadcasted_iota(jnp.int32, a.shape, 0)
    s = 1
    while s < T:
        if not reverse:
            m = row >= s
            b = jnp.where(m, a * pltpu.roll(b, s, 0) + b, b)
            a = jnp.where(m, a * pltpu.roll(a, s, 0), a)
        else:
            m = row < T - s
            b = jnp.where(m, a * pltpu.roll(b, T - s, 0) + b, b)
            a = jnp.where(m, a * pltpu.roll(a, T - s, 0), a)
        s *= 2
    h = b + a * carry_ref[...]
    carry_ref[...] = h[0:1, :] if reverse else h[T - 1:T, :]
    o_ref[0] = h * jax.nn.gelu(gate_ref[0])


def _block_diag(w):
    H, di, dj = w.shape
    eye = jnp.eye(H, dtype=w.dtype)
    return (eye[:, None, :, None] * w[:, :, None, :]).reshape(H * di, H * dj)


def rglru_direction(u, col0, n_ctx, conv_w, conv_b, wa, ba, wx, bx, lam, reverse):
    B, L, _ = u.shape
    G = conv_w.shape[-1]
    T = LRU_CHUNK
    nb, ncb = L // T, n_ctx // T
    blk = lambda c: pl.BlockSpec((1, T, G), lambda b, n: (b, _scan_block(n, ncb, nb, reverse), c))
    row = pl.BlockSpec((1, G), lambda b, n: (0, 0))
    mat = pl.BlockSpec((G, G), lambda b, n: (0, 0))
    return pl.pallas_call(
        functools.partial(_rglru_kernel, n_ctx_blocks=ncb, reverse=reverse),
        grid=(B, nb),
        in_specs=[blk(col0), blk(col0 + 1), pl.BlockSpec((RG_CONV, G), lambda b, n: (0, 0)),
                  row, mat, row, mat, row, row],
        out_specs=pl.BlockSpec((1, T, G), lambda b, n: (b, _scan_block(n, ncb, nb, reverse), 0)),
        out_shape=jax.ShapeDtypeStruct((B, L, G), F32),
        scratch_shapes=[pltpu.VMEM((T, G), F32), pltpu.VMEM((1, G), F32)],
        compiler_params=_params("parallel", "arbitrary"),
        name="rglru_mixer",
    )(u, u, conv_w, conv_b.reshape(1, G), _block_diag(wa).astype(BF16), ba.reshape(1, G),
      _block_diag(wx).astype(BF16), bx.reshape(1, G), lam.reshape(1, G))


def _rwkv_kernel(*refs, heads, n_ctx_blocks, reverse, has_vmix):
    it = iter(refs)
    r_ref, k_ref, v_ref, z_ref = next(it), next(it), next(it), next(it)
    vf_ref = next(it) if has_vmix else None
    mu_ref, w0_ref, w1_ref, w2_ref, a0_ref, a1_ref, a2_ref = (next(it) for _ in range(7))
    g1_ref, g2_ref, kk_ref, ka_ref, rk_ref, lnw_ref, lnb_ref = (next(it) for _ in range(7))
    if has_vmix:
        v0_ref, v1_ref, v2_ref = next(it), next(it), next(it)
    ones_ref = next(it)
    y_ref, s_ref, prev_ref = next(it), next(it), next(it)

    nbat, C, G = r_ref.shape
    R = nbat * C
    N = RW_HEAD
    n = pl.program_id(1)

    @pl.when(n == 0)
    def _():
        s_ref[...] = jnp.zeros_like(s_ref)

    @pl.when((n == 0) | (n == n_ctx_blocks))
    def _():
        prev_ref[...] = jnp.zeros_like(prev_ref)

    ones_half = ones_ref[...]

    def head_sum(x):
        W = ones_half.shape[0]
        return jnp.concatenate([_split_dot(x[:, i * W:(i + 1) * W], ones_half, 2) for i in range(G // W)],
                               axis=1)

    raw = [ref[...].reshape(R, G) for ref in (r_ref, k_ref, v_ref, z_ref)]
    z_raw = raw[3]
    if has_vmix:
        vm = _sigmoid(v0_ref[...] + _mm(_mm(z_raw, v1_ref[...]), v2_ref[...]))
        raw[2] = raw[2] + (vf_ref[...].reshape(R, G) - raw[2]) * vm
    gate = _mm(_sigmoid(_mm(z_raw, g1_ref[...])), g2_ref[...])
    mixed = []
    for j in range(4):
        parts = []
        for e in range(nbat):
            cur = raw[j][e * C:(e + 1) * C]
            parts.append(_shift_rows(cur, prev_ref[e * 4 + j], 1, reverse))
            prev_ref[e * 4 + j] = cur
        shifted = parts[0] if nbat == 1 else jnp.concatenate(parts, axis=0)
        mixed.append(raw[j] + (shifted - raw[j]) * mu_ref[j:j + 1, :])
    r, k, v, z = mixed
    w_log = -_softplus(-(w0_ref[...] + _mm(jnp.tanh(_mm(z, w1_ref[...])), w2_ref[...]))) - 0.5
    lw = -jnp.exp(w_log)
    a = _sigmoid(a0_ref[...] + _mm(_mm(z, a1_ref[...]), a2_ref[...]))
    kk = k * kk_ref[...]
    kk = kk / jnp.maximum(jnp.sqrt(head_sum(kk * kk)), 1e-12)
    k = k * (1.0 + (a - 1.0) * ka_ref[...])
    kka = kk * a

    ii = lax.broadcasted_iota(jnp.int32, (C, C), 0)
    jj = lax.broadcasted_iota(jnp.int32, (C, C), 1)
    incl = (ii <= jj) if reverse else (ii >= jj)
    strict = (ii < jj) if reverse else (ii > jj)
    same_sub = (ii & -RW_SUB) == (jj & -RW_SUB)
    eye = (ii == jj).astype(F32)
    ri = lax.broadcasted_iota(jnp.int32, (R, R), 0)
    rj = lax.broadcasted_iota(jnp.int32, (R, R), 1)
    incl_rows = (((ri <= rj) if reverse else (ri >= rj)) & ((ri & -C) == (rj & -C))).astype(BF16)

    cum = None
    lw_part = lw
    for _ in range(3):
        hi = lw_part.astype(BF16)
        d = jnp.dot(incl_rows, hi, preferred_element_type=F32)
        cum = d if cum is None else cum + d
        lw_part = lw_part - hi.astype(F32)
    end_row = 0 if reverse else C - 1
    ends = [cum[e * C + end_row:e * C + end_row + 1, :] for e in range(nbat)]
    cum_end = ends[0] if nbat == 1 else jnp.concatenate(
        [jnp.broadcast_to(t, (C, G)) for t in ends], axis=0)
    e_neg = jnp.exp(-cum)
    e_end = jnp.exp(cum_end - cum)
    bt = kk * jnp.exp(cum - lw)
    at = -kka * e_neg
    kt = k * e_neg
    rt = r * jnp.exp(cum)
    at_end = -kka * e_end
    kt_end = k * e_end

    chains = [(e, h) for e in range(nbat) for h in range(heads)]
    cs_ = range(len(chains))
    cut = lambda t, c: t[chains[c][0] * C:(chains[c][0] + 1) * C, chains[c][1] * N:(chains[c][1] + 1) * N]
    s0 = [s_ref[c] for c in cs_]
    vh = [cut(v, c) for c in cs_]
    br = [jnp.concatenate([cut(bt, c), cut(rt, c)], axis=0) for c in cs_]
    ak = [jnp.concatenate([cut(at, c), cut(kt, c)], axis=0) for c in cs_]
    a_all = [_mm_nt(br[c], ak[c]) for c in cs_]
    a_ab = [jnp.where(strict, a_all[c][:C, :C], 0.0) for c in cs_]
    a_bk = [jnp.where(strict, a_all[c][:C, C:], 0.0) for c in cs_]
    a_r = [jnp.concatenate([jnp.where(incl, a_all[c][C:, :C], 0.0),
                            jnp.where(incl, a_all[c][C:, C:], 0.0)], axis=1) for c in cs_]
    a_d = [jnp.where(same_sub, a_ab[c], 0.0) for c in cs_]
    a_o = [jnp.where(same_sub, 0.0, a_ab[c]) for c in cs_]
    tinv = [eye + a_d[c] for c in cs_]
    p = a_d
    for _ in range(int(math.log2(RW_SUB)) - 1):
        p = [_mm(p[c], p[c]) for c in cs_]
        tinv = [tinv[c] + _mm(tinv[c], p[c]) for c in cs_]
    nn = [_mm(tinv[c], a_o[c]) for c in cs_]
    levels = int(math.log2(C // RW_SUB))
    for lvl in range(levels):
        tinv = [tinv[c] + _mm(nn[c], tinv[c]) for c in cs_]
        if lvl + 1 < levels:
            nn = [_mm(nn[c], nn[c]) for c in cs_]

    br_s = [_mm_nt(br[c], s0[c]) for c in cs_]
    rhs = [br_s[c][:C] + _mm(a_bk[c], vh[c]) for c in cs_]
    u = [_mm(tinv[c], rhs[c]) for c in cs_]
    uv = [jnp.concatenate([u[c], vh[c]], axis=0) for c in cs_]
    y = [br_s[c][C:] + _mm(a_r[c], uv[c]) for c in cs_]
    for c in cs_:
        ak_end = jnp.concatenate([cut(at_end, c), cut(kt_end, c)], axis=0)
        g_end = jnp.exp(ends[chains[c][0]][:, chains[c][1] * N:(chains[c][1] + 1) * N])
        s_ref[c] = s0[c] * g_end + _mm(uv[c].T, ak_end)
    y = jnp.concatenate([jnp.concatenate(y[e * heads:(e + 1) * heads], axis=1) for e in range(nbat)],
                        axis=0)

    inv_n = 1.0 / N
    mean = head_sum(y) * inv_n
    yc = y - mean
    var = head_sum(yc * yc) * inv_n
    yn = yc * lax.rsqrt(var + RW_GN_EPS) * lnw_ref[...] + lnb_ref[...]
    bonus = head_sum(r * k * rk_ref[...]) * v
    y_ref[...] = ((yn + bonus) * gate).reshape(nbat, C, G)


def rwkv_direction(u, col0, n_ctx, vf_u, mu, w0, w1, w2, a0, a1, a2, g1, g2, k_k, k_a, r_k,
                   ln_w, ln_b, vmix, reverse):
    B, L, _ = u.shape
    G = w0.shape[-1]
    heads = G // RW_HEAD
    C = RW_CHUNK
    nbat = RW_BATCH if B % RW_BATCH == 0 else 1
    nb, ncb = L // C, n_ctx // C
    has_vmix = vmix is not None
    blk = lambda c: pl.BlockSpec((nbat, C, G), lambda b, n: (b, _scan_block(n, ncb, nb, reverse), c))
    full = lambda arr: pl.BlockSpec(arr.shape, lambda b, n: (0,) * arr.ndim)
    row = lambda t: t.reshape(1, G)
    hid = jnp.arange(RW_ONES) // RW_HEAD
    ones_half = (hid[:, None] == hid[None, :]).astype(BF16)
    ins = [u, u, u, u]
    specs = [blk(col0), blk(col0 + 1), blk(col0 + 2), blk(col0 + 3)]
    if has_vmix:
        ins.append(vf_u)
        specs.append(blk(col0 + 2))
    params = [mu, row(w0), w1.astype(BF16), w2.astype(BF16), row(a0), a1.astype(BF16), a2.astype(BF16),
              g1.astype(BF16), g2.astype(BF16), row(k_k), row(k_a), row(r_k), row(ln_w), row(ln_b)]
    if has_vmix:
        v0, v1, v2 = vmix
        params += [row(v0), v1.astype(BF16), v2.astype(BF16)]
    params.append(ones_half)
    ins += params
    specs += [full(p_) for p_ in params]
    return pl.pallas_call(
        functools.partial(_rwkv_kernel, heads=heads, n_ctx_blocks=ncb, reverse=reverse,
                          has_vmix=has_vmix),
        grid=(B // nbat, nb),
        in_specs=specs,
        out_specs=pl.BlockSpec((nbat, C, G), lambda b, n: (b, _scan_block(n, ncb, nb, reverse), 0)),
        out_shape=jax.ShapeDtypeStruct((B, L, G), F32),
        scratch_shapes=[pltpu.VMEM((nbat * heads, RW_HEAD, RW_HEAD), F32),
                        pltpu.VMEM((nbat * 4, C, G), F32)],
        compiler_params=_params("parallel", "arbitrary"),
        name="rwkv_mixer",
    )(*ins)


def _retention_kernel(*refs, heads, reverse, finish):
    it = iter(refs)
    q_ref, k_ref, v_ref = next(it), next(it), next(it)
    cos_ref, sin_ref, dm_ref, kdec_ref, qdec_ref, cdec_ref = (next(it) for _ in range(6))
    if finish:
        g_ref, o0_ref, gnw_ref = next(it), next(it), next(it)
    o_ref, s_ref = next(it), next(it)

    C, G = q_ref.shape[1], q_ref.shape[2]
    dh = G // heads
    quarter = dh // 4

    @pl.when(pl.program_id(1) == 0)
    def _():
        s_ref[...] = jnp.zeros_like(s_ref)

    lane = lax.broadcasted_iota(jnp.int32, (C, G), 1)
    first = (lane & quarter) == 0
    cos = cos_ref[...]
    sin = sin_ref[...]

    def rope(x):
        swapped = jnp.where(first, pltpu.roll(x, G - quarter, 1), pltpu.roll(x, quarter, 1))
        return x * cos + swapped * sin

    q = rope(q_ref[0])
    k = rope(k_ref[0]) * (dh ** -0.5)
    vb = v_ref[0].astype(BF16)
    qd = (q * qdec_ref[...]).astype(BF16)
    kd = (k * kdec_ref[...]).astype(BF16)
    qb = q.astype(BF16)
    kb = k.astype(BF16)
    cdec = cdec_ref[...]
    outs = []
    for h in range(heads):
        sl = slice(h * dh, (h + 1) * dh)
        s0 = s_ref[h]
        inner = lax.dot_general(qb[:, sl], kb[:, sl], NT_DIMS, preferred_element_type=F32) * dm_ref[h]
        out = jnp.dot(inner.astype(BF16), vb[:, sl], preferred_element_type=F32)
        out = out + jnp.dot(qd[:, sl], s0.astype(BF16), preferred_element_type=F32)
        kv = jnp.dot(kd[:, sl].T, vb[:, sl], preferred_element_type=F32)
        s_ref[h] = cdec[:, sl] * s0 + kv
        if finish:
            out = out + o0_ref[0, :, sl]
            m = jnp.mean(out, axis=-1, keepdims=True)
            var = jnp.mean(jnp.square(out - m), axis=-1, keepdims=True)
            out = (out - m) * lax.rsqrt(var + RT_GN_EPS)
        outs.append(out)
    o = jnp.concatenate(outs, axis=1)
    if finish:
        g = g_ref[0]
        o = o * gnw_ref[...] * (g * _sigmoid(g))
    o_ref[0] = o


def retention_direction(u, col0, n_ctx, cos_tab, sin_tab, lg, gn_w, prev_out, reverse):
    B, L, _ = u.shape
    G = gn_w.shape[-1]
    H = RT_HEADS
    dh = G // H
    C = RT_CHUNK
    nb, ncb = L // C, n_ctx // C
    finish = prev_out is not None
    idx = jnp.arange(C, dtype=F32)
    pos = (C - 1 - idx) if reverse else idx
    diff = pos[:, None] - pos[None, :]
    keep = diff > 0 if reverse else diff >= 0
    dmask = jnp.where(keep, jnp.exp(jnp.where(keep, diff, 0.0)[None] * lg[:, None, None]), 0.0)
    kdec = jnp.repeat(jnp.exp((C - 1 - pos)[:, None] * lg[None, :]), dh, axis=1)
    qdec = jnp.repeat(jnp.exp((pos + 1.0)[:, None] * lg[None, :]), dh, axis=1)
    cdec = jnp.repeat(jnp.exp(C * lg), dh)[None, :]
    tblk = lambda b, n: _scan_block(n, ncb, nb, reverse)
    blk = lambda c: pl.BlockSpec((1, C, G), lambda b, n: (b, tblk(b, n), c))
    tab = pl.BlockSpec((C, G), lambda b, n: (tblk(b, n), 0))
    ins = [u, u, u, cos_tab, sin_tab, dmask, kdec, qdec, cdec]
    specs = [blk(col0), blk(col0 + 1), blk(col0 + 2), tab, tab,
             pl.BlockSpec((H, C, C), lambda b, n: (0, 0, 0)),
             pl.BlockSpec((C, G), lambda b, n: (0, 0)),
             pl.BlockSpec((C, G), lambda b, n: (0, 0)),
             pl.BlockSpec((1, G), lambda b, n: (0, 0))]
    if finish:
        ins += [u, prev_out, gn_w.reshape(1, G)]
        specs += [blk(col0 + 3), blk(0), pl.BlockSpec((1, G), lambda b, n: (0, 0))]
    return pl.pallas_call(
        functools.partial(_retention_kernel, heads=H, reverse=reverse, finish=finish),
        grid=(B, nb),
        in_specs=specs,
        out_specs=blk(0),
        out_shape=jax.ShapeDtypeStruct((B, L, G), F32),
        scratch_shapes=[pltpu.VMEM((H, dh, dh), F32)],
        compiler_params=_params("parallel", "arbitrary"),
        name="retention_mixer",
    )(*ins)


def rope_tables(n_ctx, L, G):
    dh = G // RT_HEADS
    quarter = dh // 4
    n_rows = L // GRID_W
    rows = jnp.repeat(jnp.arange(n_rows, dtype=F32), GRID_W)
    cols = jnp.tile(jnp.arange(GRID_W, dtype=F32), n_rows)
    inv = jnp.power(ROPE_BASE, -jnp.arange(quarter, dtype=F32) / quarter)
    lane = jnp.arange(G)
    use_cols = (lane % dh) >= (dh // 2)
    pos = jnp.where(use_cols[None, :], cols[:, None], rows[:, None])
    ang = pos * inv[lane % quarter][None, :]
    sign = jnp.where((lane % (2 * quarter)) < quarter, -1.0, 1.0)[None, :]
    cos = jnp.concatenate([jnp.ones((n_ctx, G), F32), jnp.cos(ang)], axis=0)
    sin = jnp.concatenate([jnp.zeros((n_ctx, G), F32), jnp.sin(ang) * sign], axis=0)
    return cos, sin


def _moe_stream_kernel(te_ref, nt_ref, first_ref, nxt_ref, slot_ref,
                       x_ref, wgu_hbm, bgu_ref, wdn_hbm, bdn_ref, g_ref, o_ref,
                       wgu_bf, wdn_bf, stage, sem, done_ref, *, layer):
    i = pl.program_id(0)
    active = i < nt_ref[0]
    e, s, nxt = te_ref[i], slot_ref[i], nxt_ref[i]
    D, F = wgu_bf.shape[1], wdn_bf.shape[1]
    ch = stage.shape[1]
    n_gu = D // ch
    n_ch = n_gu + F // ch

    def copy(expert, c, gate_up):
        src = (wgu_hbm.at[layer, expert, pl.ds(pl.multiple_of(c * ch, ch), ch), :] if gate_up else
               wdn_hbm.at[layer, expert, pl.ds(pl.multiple_of((c - n_gu) * ch, ch), ch), :])
        return pltpu.make_async_copy(src, stage.at[c % 2], sem.at[c % 2])

    def start(expert, c):
        @pl.when(c < n_gu)
        def _():
            copy(expert, c, True).start()

        @pl.when((c >= n_gu) & (c < n_ch))
        def _():
            copy(expert, c, False).start()

    def land(expert, c, slot):
        pltpu.make_async_copy(wgu_hbm.at[layer, expert, pl.ds(0, ch), :], stage.at[c % 2],
                              sem.at[c % 2]).wait()
        val = stage[c % 2].astype(BF16)

        @pl.when(c < n_gu)
        def _():
            wgu_bf[slot, pl.ds(pl.multiple_of(c * ch, ch), ch), :] = val

        @pl.when(c >= n_gu)
        def _():
            wdn_bf[slot, pl.ds(pl.multiple_of((c - n_gu) * ch, ch), ch), :] = val

        start(expert, c + 2)

    @pl.when(active & (first_ref[i] == 1))
    def _():
        @pl.when(i == 0)
        def _():
            done_ref[0] = 0
            start(e, 0)
            start(e, 1)

        def body(c, carry):
            land(e, c, s)
            return carry

        lax.fori_loop(done_ref[0], n_ch, body, 0)
        done_ref[0] = 0

        @pl.when(nxt >= 0)
        def _():
            start(nxt, 0)
            start(nxt, 1)

    @pl.when(active)
    def _():
        gu = jnp.dot(x_ref[...], wgu_bf[s], preferred_element_type=F32) + bgu_ref[0]
        glu = jnp.minimum(gu[:, :F], SWIGLU_LIMIT)
        lin = jnp.clip(gu[:, F:], -SWIGLU_LIMIT, SWIGLU_LIMIT)
        act = glu * jax.nn.sigmoid(SWIGLU_ALPHA * glu) * (lin + 1.0)
        y = jnp.dot(act.astype(BF16), wdn_bf[s], preferred_element_type=F32) + bdn_ref[0]
        o_ref[...] = (y * g_ref[...]).astype(o_ref.dtype)

    @pl.when(jnp.logical_not(active))
    def _():
        o_ref[...] = jnp.zeros_like(o_ref)

    for _ in range(MOE_CHUNKS_PER_STEP):
        @pl.when(active & (nxt >= 0) & (done_ref[0] < n_ch))
        def _():
            c = done_ref[0]
            land(nxt, c, 1 - s)
            done_ref[0] = c + 1


def moe_grouped(xs, tile_expert, n_tiles_used, sizes, layer, w_gu, b_gu, w_dn, b_dn, row_gate):
    P, D = xs.shape
    _, E, _, F2 = w_gu.shape
    F = F2 // 2
    tm = MOE_TILE
    nt = P // tm
    tile = jnp.arange(nt, dtype=jnp.int32)
    prev = jnp.concatenate([jnp.full((1,), -1, jnp.int32), tile_expert[:-1]])
    first = ((tile_expert != prev) & (tile < n_tiles_used[0])).astype(jnp.int32)
    slot = (jnp.cumsum(first) - 1) % 2
    expert = jnp.arange(E, dtype=jnp.int32)
    later = jnp.where((sizes[None, :] > 0) & (expert[None, :] > expert[:, None]), expert[None, :], E)
    next_expert = jnp.min(later, axis=1)
    nxt = jnp.where(next_expert < E, next_expert, -1).astype(jnp.int32)[tile_expert]
    grid_spec = pltpu.PrefetchScalarGridSpec(
        num_scalar_prefetch=5,
        grid=(nt,),
        in_specs=[
            pl.BlockSpec((tm, D), lambda i, *_: (i, 0)),
            pl.BlockSpec(memory_space=pl.ANY),
            pl.BlockSpec((1, 1, F2), lambda i, te, *_: (te[i], 0, 0)),
            pl.BlockSpec(memory_space=pl.ANY),
            pl.BlockSpec((1, 1, D), lambda i, te, *_: (te[i], 0, 0)),
            pl.BlockSpec((tm, 1), lambda i, *_: (i, 0)),
        ],
        out_specs=pl.BlockSpec((tm, D), lambda i, *_: (i, 0)),
        scratch_shapes=[pltpu.VMEM((2, D, F2), BF16), pltpu.VMEM((2, F, D), BF16),
                        pltpu.VMEM((2, MOE_CHUNK_ROWS, D), F32), pltpu.SemaphoreType.DMA((2,)),
                        pltpu.SMEM((1,), jnp.int32)],
    )
    return pl.pallas_call(
        functools.partial(_moe_stream_kernel, layer=layer),
        grid_spec=grid_spec,
        out_shape=jax.ShapeDtypeStruct((P, D), BF16),
        compiler_params=_params("arbitrary"),
        name="moe_grouped",
    )(tile_expert, n_tiles_used, first, nxt, slot.astype(jnp.int32),
      xs, w_gu, b_gu.reshape(E, 1, F2), w_dn, b_dn.reshape(E, 1, D), row_gate)


def _router_kernel(x_ref, nw_ref, sh_ref, sc_ref, rwh_ref, rwl_ref, rb_ref,
                   h_ref, idx_ref, p_ref, rank_ref, cnt_ref, carry_ref):
    @pl.when((pl.program_id(0) == 0) & (pl.program_id(1) == 0))
    def _():
        carry_ref[...] = jnp.zeros_like(carry_ref)

    x = x_ref[0]
    y = x * lax.rsqrt(jnp.mean(x * x, axis=-1, keepdims=True) + RMS_EPS)
    y = y * nw_ref[...] * (1.0 + sc_ref[0]) + sh_ref[0]
    y_hi = y.astype(BF16)
    h_ref[0] = y_hi
    y_lo = (y - y_hi.astype(F32)).astype(BF16)
    logits = (jnp.dot(y_hi, rwh_ref[...], preferred_element_type=F32)
              + jnp.dot(y_lo, rwh_ref[...], preferred_element_type=F32)
              + jnp.dot(y_hi, rwl_ref[...], preferred_element_type=F32)) + rb_ref[...]
    E = logits.shape[-1]
    lane = lax.broadcasted_iota(jnp.int32, logits.shape, 1)
    vals, ids = [], []
    for _ in range(TOP_K):
        m = jnp.max(logits, axis=-1, keepdims=True)
        i = jnp.min(jnp.where(logits == m, lane, E), axis=-1, keepdims=True)
        vals.append(m)
        ids.append(i)
        logits = jnp.where(lane == i, -jnp.inf, logits)
    e = jnp.exp(jnp.concatenate(vals, axis=1) - vals[0])
    p_ref[0] = e / jnp.sum(e, axis=-1, keepdims=True)
    idx_ref[0] = jnp.concatenate(ids, axis=1)

    tm = x.shape[0]
    chosen = [lane == i for i in ids]
    onehot = sum(c.astype(F32) for c in chosen)
    earlier = (lax.broadcasted_iota(jnp.int32, (tm, tm), 1)
               < lax.broadcasted_iota(jnp.int32, (tm, tm), 0)).astype(BF16)
    prefix = jnp.dot(earlier, onehot.astype(BF16), preferred_element_type=F32) + carry_ref[...]
    rank_ref[0] = jnp.concatenate(
        [jnp.sum(jnp.where(c, prefix, 0.0), axis=-1, keepdims=True) for c in chosen], axis=1).astype(jnp.int32)
    total = carry_ref[...] + jnp.sum(onehot, axis=0, keepdims=True)
    carry_ref[...] = total
    cnt_ref[...] = total


def moe_router(xs, row0, n_rows, n_ctx, nw, shift, scale, router_w, router_b):
    B, _, D = xs.shape
    E = router_w.shape[1]
    tm = ROW_TILE
    ncb, b0 = n_ctx // tm, row0 // tm
    mod_idx = lambda b, i: (2 * b + ((i + b0) >= ncb).astype(jnp.int32), 0, 0)
    rw_hi = router_w.astype(BF16)
    rw_lo = (router_w - rw_hi.astype(F32)).astype(BF16)
    full = lambda shape: pl.BlockSpec(shape, lambda b, i: (0,) * len(shape))
    out = lambda w: pl.BlockSpec((1, tm, w), lambda b, i: (b, i, 0))
    return pl.pallas_call(
        _router_kernel,
        grid=(B, n_rows // tm),
        in_specs=[pl.BlockSpec((1, tm, D), lambda b, i: (b, i + b0, 0)), full((1, D)),
                  pl.BlockSpec((1, 1, D), mod_idx), pl.BlockSpec((1, 1, D), mod_idx),
                  full((D, E)), full((D, E)), full((1, E))],
        out_specs=[out(D), out(TOP_K), out(TOP_K), out(TOP_K), full((1, E))],
        out_shape=[jax.ShapeDtypeStruct((B, n_rows, D), BF16),
                   jax.ShapeDtypeStruct((B, n_rows, TOP_K), jnp.int32),
                   jax.ShapeDtypeStruct((B, n_rows, TOP_K), F32),
                   jax.ShapeDtypeStruct((B, n_rows, TOP_K), jnp.int32),
                   jax.ShapeDtypeStruct((1, E), F32)],
        scratch_shapes=[pltpu.VMEM((1, E), F32)],
        compiler_params=_params("arbitrary", "arbitrary"),
        name="moe_router",
    )(xs, nw.reshape(1, D), shift.reshape(2 * B, 1, D), scale.reshape(2 * B, 1, D),
      rw_hi, rw_lo, router_b.reshape(1, E))


def _cast_kernel(x_ref, o_ref):
    o_ref[...] = x_ref[...].astype(BF16)


def cast_bf16(w):
    C = w.shape[-1]
    R = w.size // C
    tr = 8
    while tr * 2 * C <= CAST_BLOCK_ELEMS and R % (tr * 2) == 0:
        tr *= 2
    spec = pl.BlockSpec((tr, C), lambda i: (i, 0))
    return pl.pallas_call(
        _cast_kernel,
        grid=(R // tr,),
        in_specs=[spec],
        out_specs=spec,
        out_shape=jax.ShapeDtypeStruct((R, C), BF16),
        compiler_params=_params("parallel"),
        name="cast_bf16",
    )(w.reshape(R, C)).reshape(w.shape)


def _combine_kernel(g_ref, res_ref, gate_ref, fw_ref, o_ref, *, final_norm):
    f = g_ref[0, 0].astype(F32)
    for k in range(1, g_ref.shape[0]):
        f = f + g_ref[k, 0].astype(F32)
    y = res_ref[0] + gate_ref[0] * f
    if final_norm:
        y = y * lax.rsqrt(jnp.mean(y * y, axis=-1, keepdims=True) + RMS_EPS) * fw_ref[...]
    o_ref[0] = y


def moe_combine(g, xs, row0, n_ctx, gate, final_w):
    K, B, n_rows, D = g.shape
    tm = ROW_TILE
    ncb, b0 = n_ctx // tm, row0 // tm
    g_idx = lambda b, i: (2 * b + ((i + b0) >= ncb).astype(jnp.int32), 0, 0)
    final_norm = final_w is not None
    fw = final_w if final_norm else jnp.ones((D,), F32)
    return pl.pallas_call(
        functools.partial(_combine_kernel, final_norm=final_norm),
        grid=(B, n_rows // tm),
        in_specs=[pl.BlockSpec((K, 1, tm, D), lambda b, i: (0, b, i, 0)),
                  pl.BlockSpec((1, tm, D), lambda b, i: (b, i + b0, 0)),
                  pl.BlockSpec((1, 1, D), g_idx),
                  pl.BlockSpec((1, D), lambda b, i: (0, 0))],
        out_specs=pl.BlockSpec((1, tm, D), lambda b, i: (b, i, 0)),
        out_shape=jax.ShapeDtypeStruct((B, n_rows, D), F32),
        compiler_params=_params("parallel", "parallel"),
        name="moe_combine",
    )(g, xs, gate.reshape(2 * B, 1, D), fw.reshape(1, D))


def moe(t, idx, probs, rank, counts, layer, w_gu, b_gu, w_dn, b_dn):
    T, D = t.shape
    E = N_EXPERTS
    tm = MOE_TILE
    sizes = counts.astype(jnp.int32)
    padded = ((sizes + tm - 1) // tm) * tm
    ends = jnp.cumsum(padded)
    starts = ends - padded
    idx_f, probs_f, rank_f = idx.reshape(-1), probs.reshape(-1), rank.reshape(-1)
    expert = jnp.arange(E, dtype=jnp.int32)
    dest_f = rank_f + jnp.sum(jnp.where(idx_f[None, :] == expert[:, None], starts[:, None], 0), axis=0)
    n_rows = T * TOP_K + E * tm
    nt = n_rows // tm
    tile_start = jnp.arange(nt, dtype=jnp.int32) * tm
    tile_expert = jnp.minimum(jnp.sum((ends[None, :] <= tile_start[:, None]).astype(jnp.int32), axis=1),
                              E - 1)
    order = jnp.argsort(idx_f, stable=True).astype(jnp.int32)
    within = (tile_start - starts[tile_expert])[:, None] + jnp.arange(tm, dtype=jnp.int32)[None, :]
    valid = (within < sizes[tile_expert][:, None]).reshape(-1)
    first_of = (jnp.cumsum(sizes) - sizes)[tile_expert]
    src = order[jnp.clip(first_of[:, None] + within, 0, T * TOP_K - 1).reshape(-1)]
    row_token = jnp.where(valid, src // TOP_K, jnp.arange(n_rows, dtype=jnp.int32) % T)
    row_gate = jnp.where(valid, probs_f[src], 0.0)
    xs = t[row_token]
    n_used = (ends[-1] // tm).astype(jnp.int32).reshape(1)
    ys = moe_grouped(xs, tile_expert, n_used, sizes, layer, w_gu, b_gu, w_dn, b_dn,
                     row_gate.reshape(n_rows, 1))
    return ys[dest_f.reshape(T, TOP_K).T]


HY_COLS = 512
HY_ROWS = 512


def _twiddle_kernel(c1_ref, s1_ref, c2_ref, s2_ref, cs_ref, ss_ref):
    c1, s1 = c1_ref[0], s1_ref[0]
    c2, s2 = c2_ref[...], s2_ref[...]
    cs_ref[...] = (c1 * c2 - s1 * s2).astype(BF16)
    ss_ref[...] = (s1 * c2 + c1 * s2).astype(BF16)


def dft_matrices(L):
    R = min(64, L)
    n_hi = L // R
    period = 8 * L
    a = 2 * jnp.arange(L, dtype=jnp.int32) + 1
    m1 = (a[None, :] * (2 * R * jnp.arange(n_hi, dtype=jnp.int32))[:, None]) % period
    m2 = (a[None, :] * (2 * jnp.arange(R, dtype=jnp.int32) + 1)[:, None]) % period
    ang = lambda m: (m.astype(F32) - jnp.where(m >= period // 2, period, 0).astype(F32)) * (math.pi / (4 * L))
    c1, s1 = jnp.cos(ang(m1)).reshape(n_hi, 1, L), jnp.sin(ang(m1)).reshape(n_hi, 1, L)
    c2, s2 = jnp.cos(ang(m2)), jnp.sin(ang(m2))
    hi = pl.BlockSpec((1, 1, L), lambda i: (i, 0, 0))
    lo = pl.BlockSpec((R, L), lambda i: (0, 0))
    out = pl.BlockSpec((R, L), lambda i: (i, 0))
    return pl.pallas_call(
        _twiddle_kernel,
        grid=(n_hi,),
        in_specs=[hi, hi, lo, lo],
        out_specs=[out, out],
        out_shape=[jax.ShapeDtypeStruct((L, L), BF16)] * 2,
        compiler_params=_params("parallel"),
        name="dft_matrices",
    )(c1, s1, c2, s2)


def _dft_pair_kernel(cs_ref, ss_ref, x_ref, oc_ref, os_ref):
    x = x_ref[...]
    oc_ref[...] = jnp.dot(cs_ref[...], x, preferred_element_type=F32)
    os_ref[...] = jnp.dot(ss_ref[...], x, preferred_element_type=F32)


def dft_pair(cs, ss, x_bf16):
    L, M = x_bf16.shape
    tm = min(HY_ROWS, L)
    tn = min(512, M)
    mat = pl.BlockSpec((tm, L), lambda j, i: (i, 0))
    out = pl.BlockSpec((tm, tn), lambda j, i: (i, j))
    return pl.pallas_call(
        _dft_pair_kernel,
        grid=(M // tn, L // tm),
        in_specs=[mat, mat, pl.BlockSpec((L, tn), lambda j, i: (0, j))],
        out_specs=[out, out],
        out_shape=[jax.ShapeDtypeStruct((L, M), F32)] * 2,
        compiler_params=_params("parallel", "arbitrary"),
        name="dft_pair",
    )(cs, ss, x_bf16)


def _hyena_conv_kernel(z_ref, gate_ref, cs_ref, ss_ref, hre_ref, him_ref, bias_ref, o_ref,
                       zb_ref, yre_ref, yim_ref):
    phase = pl.program_id(1)
    m = pl.program_id(2)
    tm = cs_ref.shape[0]
    L = z_ref.shape[1]
    rows = pl.ds(pl.multiple_of(m * tm, tm), tm)

    @pl.when((phase == 0) & (m == 0))
    def _():
        zb_ref[...] = z_ref[0].astype(BF16)

    @pl.when(phase == 0)
    def _():
        xc = jnp.dot(cs_ref[...], zb_ref[...], preferred_element_type=F32)
        xs = jnp.dot(ss_ref[...], zb_ref[...], preferred_element_type=F32)
        hre, him = hre_ref[...], him_ref[...]
        yre_ref[rows, :] = (xc * hre + xs * him).astype(BF16)
        yim_ref[rows, :] = (xc * him - xs * hre).astype(BF16)

    @pl.when(phase == 1)
    def _():
        y = (jnp.dot(cs_ref[...], yre_ref[...], preferred_element_type=F32)
             - jnp.dot(ss_ref[...], yim_ref[...], preferred_element_type=F32)) * (1.0 / L)
        o_ref[0] = gate_ref[0] * (y + bias_ref[...] * z_ref[0, rows, :])


def hyena_conv(z_src, z_col, gate_src, gate_col, cs, ss, hre, him, bias):
    B, L, _ = z_src.shape
    G = bias.shape[0]
    tc = HY_COLS
    tm = min(HY_ROWS, L)
    ncb = G // tc
    nm = L // tm
    pin = lambda phase, m, keep: jnp.where(phase == keep, m, (nm - 1) * (1 - keep))
    return pl.pallas_call(
        _hyena_conv_kernel,
        grid=(B * ncb, 2, nm),
        in_specs=[
            pl.BlockSpec((1, L, tc), lambda i, p, m: (i // ncb, 0, z_col * ncb + i % ncb)),
            pl.BlockSpec((1, tm, tc), lambda i, p, m: (i // ncb, pin(p, m, 1), gate_col * ncb + i % ncb)),
            pl.BlockSpec((tm, L), lambda i, p, m: (m, 0)),
            pl.BlockSpec((tm, L), lambda i, p, m: (m, 0)),
            pl.BlockSpec((tm, tc), lambda i, p, m: (pin(p, m, 0), i % ncb)),
            pl.BlockSpec((tm, tc), lambda i, p, m: (pin(p, m, 0), i % ncb)),
            pl.BlockSpec((1, tc), lambda i, p, m: (0, i % ncb)),
        ],
        out_specs=pl.BlockSpec((1, tm, tc), lambda i, p, m: (i // ncb, pin(p, m, 1), i % ncb)),
        out_shape=jax.ShapeDtypeStruct((B, L, G), F32),
        scratch_shapes=[pltpu.VMEM((L, tc), BF16), pltpu.VMEM((L, tc), BF16), pltpu.VMEM((L, tc), BF16)],
        compiler_params=_params("parallel", "arbitrary", "arbitrary"),
        name="hyena_conv",
    )(z_src, gate_src, cs, ss, hre, him, bias.reshape(1, G))


def hyena_filter_spectrum(L, G, cs, ss, w1, b1, w2, b2, w3, b3, w4, freq):
    t = jnp.linspace(0.0, 1.0, L, dtype=F32)[:, None]
    bands = (HY_EMB - 1) // 2
    fr = jnp.linspace(1e-4, bands - 1, bands, dtype=F32)
    ang = (2.0 * math.pi / L) * jnp.arange(L, dtype=F32)[:, None] * fr[None, :]
    z = jnp.concatenate([t, jnp.cos(ang), -jnp.sin(ang)], axis=-1)
    h = jnp.sin(freq * (z @ w1 + b1))
    h = jnp.sin(freq * (h @ w2 + b2))
    h = jnp.sin(freq * (h @ w3 + b3))
    h = (h @ w4).reshape(L, 2, HY_ORDER * G)
    deltas = jnp.abs(jnp.linspace(HY_MIN_DECAY, HY_MAX_DECAY, G, dtype=F32))
    h = h * jnp.tile(jnp.exp(-t * deltas), (1, HY_ORDER))[:, None, :]
    fwd = h[:, 0]
    bwd = jnp.concatenate([h[1:, 1], jnp.zeros((1, HY_ORDER * G), F32)], axis=0)
    norm = jnp.sum(jnp.abs(fwd), axis=0) + jnp.sum(jnp.abs(bwd), axis=0)
    xc, xs = dft_pair(cs, ss, jnp.concatenate([fwd, bwd], axis=1).astype(BF16))
    M = HY_ORDER * G
    a = xc[:, :M] + xc[:, M:]
    b = xs[:, M:] - xs[:, :M]
    ph = (math.pi / (2 * L)) * (jnp.arange(L, dtype=F32)[:, None] + 0.5)
    hre = (jnp.cos(ph) * a - jnp.sin(ph) * b) / norm
    him = (jnp.sin(ph) * a + jnp.cos(ph) * b) / norm
    split = lambda t_: jnp.moveaxis(t_.reshape(L, HY_ORDER, G), 1, 0)
    return split(hre), split(him)


def hyena_mixer(u, cs, ss, short_w, short_b, fw1, fb1, fw2, fb2, fw3, fb3, fw4, ffreq, bias):
    L = u.shape[1]
    G = u.shape[2] // 3
    pad = jnp.pad(u, ((0, 0), (1, 1), (0, 0)))
    us = short_w[0] * pad[:, :L] + short_w[1] * u + short_w[2] * pad[:, 2:] + short_b
    hre, him = hyena_filter_spectrum(L, G, cs, ss, fw1, fb1, fw2, fb2, fw3, fb3, fw4, ffreq)
    z = hyena_conv(us, 0, us, 1, cs, ss, hre[0], him[0], bias[0])
    return hyena_conv(z, 0, us, 2, cs, ss, hre[1], him[1], bias[1])


def rms_norm(x, w):
    y = x * lax.rsqrt(jnp.mean(x * x, axis=-1, keepdims=True) + RMS_EPS)
    return y * w


def kernel(x, c, ctx, c_ctx, ada_w, ada_b, norm1_w, norm2_w, w_in, w_out, hy_short_w, hy_short_b, hy_f_w1, hy_f_b1, hy_f_w2, hy_f_b2, hy_f_w3, hy_f_b3, hy_f_w4, hy_f_freq, hy_bias, rg_conv_w, rg_conv_b, rg_wa, rg_ba, rg_wx, rg_bx, rg_lambda, rw_mu, rw_w0, rw_w1, rw_w2, rw_a0, rw_a1, rw_a2, rw_g1, rw_g2, rw_k_k, rw_k_a, rw_r_k, rw_ln_w, rw_ln_b, rw_v0, rw_v1, rw_v2, rt_decay, rt_gn_w, moe_router_w, moe_router_b, moe_w_gu, moe_b_gu, moe_w_dn, moe_b_dn, final_norm_w):
    depth = ada_w.shape[0]
    B, L, D = x.shape
    n_ctx = ctx.shape[1]
    G = D // N_GROUPS
    cond = jnp.concatenate([jnp.broadcast_to(jax.nn.silu(c_ctx.astype(F32))[None, :], (B, D)),
                            jax.nn.silu(c.astype(F32))], axis=0)
    xs = jnp.concatenate([ctx.astype(x.dtype), x], axis=1)
    cos_tab, sin_tab = rope_tables(n_ctx, L, G)
    dft_lat = dft_matrices(L)
    dft_ctx = dft_matrices(n_ctx)
    w_in_b, w_out_b = cast_bf16(w_in), cast_bf16(w_out)
    u_first = None

    for l in range(depth):
        last = l == depth - 1
        mod = cond @ ada_w[l].astype(F32) + ada_b[l]
        mod = jnp.stack([mod[:B], mod[B:]], axis=1)
        mods = jnp.split(mod, N_MOD, axis=-1)

        u = modulated_matmul(xs, norm1_w[l], mods[0], mods[1], w_in_b, l, n_ctx, tn=w_in.shape[2] // 2)
        if l == 0:
            u_first = u

        hy_p = (hy_short_w[l], hy_short_b[l], hy_f_w1[l], hy_f_b1[l], hy_f_w2[l], hy_f_b2[l],
                hy_f_w3[l], hy_f_b3[l], hy_f_w4[l], hy_f_freq[l], hy_bias[l])
        hy_l = hyena_mixer(u[:, n_ctx:, : 3 * G], *dft_lat, *hy_p)
        if last:
            hy_c = jnp.zeros((B, n_ctx, G), F32)
        else:
            hy_c = hyena_mixer(u[:, :n_ctx, : 3 * G], *dft_ctx, *hy_p)
        hy = jnp.concatenate([hy_c, hy_l], axis=1)

        rg = [rglru_direction(u, 3, n_ctx, rg_conv_w[l, d], rg_conv_b[l, d], rg_wa[l, d], rg_ba[l, d],
                              rg_wx[l, d], rg_bx[l, d], rg_lambda[l, d], reverse=(d == 1))
              for d in range(2)]
        vmix = None if l == 0 else (rw_v0[l - 1], rw_v1[l - 1], rw_v2[l - 1])
        rw = [rwkv_direction(u, 5, n_ctx, u_first, rw_mu[l, d], rw_w0[l, d], rw_w1[l, d], rw_w2[l, d],
                             rw_a0[l, d], rw_a1[l, d], rw_a2[l, d], rw_g1[l], rw_g2[l], rw_k_k[l],
                             rw_k_a[l], rw_r_k[l], rw_ln_w[l], rw_ln_b[l], vmix, reverse=(d == 1))
              for d in range(2)]
        lg = -jax.nn.softplus(rt_decay[l].astype(F32))
        rt0 = retention_direction(u, 9, n_ctx, cos_tab, sin_tab, lg[0], rt_gn_w[l], None, reverse=False)
        rt = retention_direction(u, 9, n_ctx, cos_tab, sin_tab, lg[1], rt_gn_w[l], rt0, reverse=True)

        xs = gated_out_proj((hy, rg[0], rg[1], rw[0], rw[1], rt), w_out_b, l, xs, mods[2], n_ctx)

        row0 = n_ctx if last else 0
        n_rows = xs.shape[1] - row0
        h, idx, probs, rank, counts = moe_router(xs, row0, n_rows, n_ctx, norm2_w[l], mods[3], mods[4],
                                                 moe_router_w[l], moe_router_b[l])
        flat = lambda t_: t_.reshape(-1, TOP_K)
        g = moe(h.reshape(-1, D), flat(idx), flat(probs), flat(rank), counts.reshape(-1), l,
                moe_w_gu, moe_b_gu[l], moe_w_dn, moe_b_dn[l])
        xs = moe_combine(g.reshape(TOP_K, B, n_rows, D), xs, row0, n_ctx, mods[5],
                         final_norm_w if last else None)
    return xs
```
